```python
import jax, jax.numpy as jnp
from jax import lax
import numpy as np

D_MODEL = 2048
BATCH = 2
SEQ = 16384
DEPTH = 4

CTX_LEN = 256
GRID_W = 64
N_MIXERS = 2
HEAD_DIM = 128
N_HEADS = D_MODEL // HEAD_DIM
N_KV_HEADS = N_HEADS // 4
GROUP = N_HEADS // N_KV_HEADS
Q_DIM = N_HEADS * HEAD_DIM
KV_DIM = N_KV_HEADS * HEAD_DIM
Q_BLOCK = 128
ROPE_THETA = 10000.0
AXIS_ROT_DIM = HEAD_DIM // 2
REC_EXPAND = 128
REC_HEADS = D_MODEL // REC_EXPAND
REC_DK = REC_EXPAND
REC_DV = D_MODEL // REC_HEADS
CHUNK = 64
D_FF = 5632
N_EXPERTS = 8
TOP_K = 2
D_FF_EXPERT = 4096
N_MOD = 6
EPS = 1e-6
N_EVEN = (DEPTH + 1) // 2
N_ODD = DEPTH // 2

kernel_name = 'hybrid_attn_hgrn2_moe_diffusion_trunk'


def rms_norm(x, gain):
    xf = x.astype(jnp.float32)
    y = xf * lax.rsqrt(jnp.mean(xf * xf, axis=-1, keepdims=True) + EPS)
    return (y * gain.astype(jnp.float32)).astype(x.dtype)


def modulation(cond, w, b):
    m = jax.nn.silu(cond) @ w + b
    return jnp.split(m, N_MOD, axis=-1)


def modulate(y, shift, scale):
    return y * (1.0 + scale) + shift


def axial_rope_angles(n_tokens):
    rows = n_tokens // GRID_W
    row = jnp.repeat(jnp.arange(rows, dtype=jnp.float32), GRID_W)
    col = jnp.tile(jnp.arange(GRID_W, dtype=jnp.float32), rows)
    inv_freq = ROPE_THETA ** (-jnp.arange(0, AXIS_ROT_DIM, 2, dtype=jnp.float32) / AXIS_ROT_DIM)
    return row[:, None] * inv_freq, col[:, None] * inv_freq


def _rotate_section(x, ang):
    f = ang.shape[-1]
    cos = jnp.cos(ang)[:, None, :]
    sin = jnp.sin(ang)[:, None, :]
    x1, x2 = x[..., :f], x[..., f:]
    return jnp.concatenate([x1 * cos - x2 * sin, x1 * sin + x2 * cos], axis=-1)


def apply_axial_rope(x, angles):
    ang_row, ang_col = angles
    xf = x.astype(jnp.float32)
    out = jnp.concatenate([_rotate_section(xf[..., :AXIS_ROT_DIM], ang_row),
                           _rotate_section(xf[..., AXIS_ROT_DIM:], ang_col)], axis=-1)
    return out.astype(x.dtype)


def gqa_attend(q, k, v):
    b, tq = q.shape[:2]
    qg = q.reshape(b, tq, N_KV_HEADS, GROUP, HEAD_DIM)
    s = jnp.einsum('bqkgd,btkd->bkgqt', qg, k, preferred_element_type=jnp.float32) * HEAD_DIM ** -0.5
    p = jax.nn.softmax(s, axis=-1).astype(v.dtype)
    o = jnp.einsum('bkgqt,btkd->bqkgd', p, v)
    return o.reshape(b, tq, N_HEADS, HEAD_DIM)


def blocked_attention(q, k, v):
    b, s = q.shape[:2]
    nb = s // Q_BLOCK
    qb = q.reshape(b, nb, Q_BLOCK, N_HEADS, HEAD_DIM).transpose(1, 0, 2, 3, 4)
    ob = lax.map(lambda qi: gqa_attend(qi, k, v), qb)
    return ob.transpose(1, 0, 2, 3, 4).reshape(b, s, N_HEADS, HEAD_DIM)


def attention_mixer(y_lat, y_ctx, w_in, w_o, q_gain, k_gain, angles, with_ctx_out):
    def qkv(y):
        b, t, _ = y.shape
        q, k, v = jnp.split(y @ w_in, [Q_DIM, Q_DIM + KV_DIM], axis=-1)
        q = rms_norm(q.reshape(b, t, N_HEADS, HEAD_DIM), q_gain)
        k = rms_norm(k.reshape(b, t, N_KV_HEADS, HEAD_DIM), k_gain)
        return q, k, v.reshape(b, t, N_KV_HEADS, HEAD_DIM)
    b, s, _ = y_lat.shape
    q_l, k_l, v_l = qkv(y_lat)
    q_c, k_c, v_c = qkv(y_ctx)
    q_l = apply_axial_rope(q_l, angles)
    k_l = apply_axial_rope(k_l, angles)
    k_all = jnp.concatenate([k_c, k_l], axis=1)
    v_all = jnp.concatenate([v_c, v_l], axis=1)
    out_lat = blocked_attention(q_l, k_all, v_all).reshape(b, s, D_MODEL) @ w_o
    out_ctx = None
    if with_ctx_out:
        out_ctx = gqa_attend(q_c, k_c, v_c).reshape(b, y_ctx.shape[1], D_MODEL) @ w_o
    return out_lat, out_ctx


def hgrn_lower_bounds(logits):
    p = jax.nn.softmax(logits.astype(jnp.float32), axis=0)
    return jnp.cumsum(p, axis=0) - p[0]


def forget_gate(f_logit, lb):
    fg = lb + (1.0 - lb) * jax.nn.sigmoid(f_logit.astype(jnp.float32))
    return 1.0 - fg, jnp.log(fg)


def chunk_gated_scan(q, k, v, log_f, s0):
    b, t, h, _ = q.shape
    n = t // CHUNK
    def to_chunks(a):
        return a.reshape(b, n, CHUNK, h, a.shape[-1]).transpose(1, 0, 3, 2, 4)
    causal = jnp.tril(jnp.ones((CHUNK, CHUNK), dtype=bool))[:, :, None]
    def step(s, inp):
        qc, kc, vc, lfc = inp
        qf, kf, vf = qc.astype(jnp.float32), kc.astype(jnp.float32), vc.astype(jnp.float32)
        cb = jnp.cumsum(lfc.astype(jnp.float32), axis=2)
        o_inter = jnp.einsum('bhtd,bhde->bhte', qf * jnp.exp(cb), s)
        diff = cb[:, :, :, None, :] - cb[:, :, None, :, :]
        decay = jnp.exp(jnp.where(causal, diff, -jnp.inf))
        a = jnp.einsum('bhtd,bhsd,bhtsd->bhts', qf, kf, decay)
        o = o_inter + jnp.einsum('bhts,bhse->bhte', a, vf)
        last = cb[:, :, -1:, :]
        s_new = jnp.exp(last[:, :, 0, :])[..., None] * s + \
            jnp.einsum('bhsd,bhse->bhde', kf * jnp.exp(last - cb), vf)
        return s_new, o.astype(qc.dtype)
    s_fin, o = lax.scan(step, s0, (to_chunks(q), to_chunks(k), to_chunks(v), to_chunks(log_f)))
    o = o.transpose(1, 0, 3, 2, 4).reshape(b, t, h, v.shape[-1])
    return o, s_fin


def hgrn2_mixer(y_lat, y_ctx, w_in, w_o, lb, o_gain, with_ctx_out):
    def project(y):
        b, t, _ = y.shape
        q, i, f_fw, f_bw, g = jnp.split(y @ w_in, 5, axis=-1)
        q = (jax.nn.silu(q) * REC_DK ** -0.5).reshape(b, t, REC_HEADS, REC_DK)
        v = i.reshape(b, t, REC_HEADS, REC_DV)
        k_fw, lf_fw = forget_gate(f_fw, lb[0])
        k_bw, lf_bw = forget_gate(f_bw, lb[1])
        heads = lambda a: a.reshape(b, t, REC_HEADS, REC_DK)
        return (q, v, heads(k_fw.astype(y.dtype)), heads(lf_fw),
                heads(k_bw.astype(y.dtype)), heads(lf_bw), g)
    def readout(o, g):
        b, t = o.shape[:2]
        o = rms_norm(o, o_gain).reshape(b, t, D_MODEL)
        return (o * jax.nn.silu(g)) @ w_o
    flip = lambda a: jnp.flip(a, axis=1)
    qc, vc, kcf, lcf, kcb, lcb, gc = project(y_ctx)
    ql, vl, klf, llf, klb, llb, gl = project(y_lat)
    s0 = jnp.zeros((y_lat.shape[0], REC_HEADS, REC_DK, REC_DV), jnp.float32)
    o_cf, s_cf = chunk_gated_scan(qc, kcf, vc, lcf, s0)
    o_cb, s_cb = chunk_gated_scan(flip(qc), flip(kcb), flip(vc), flip(lcb), s0)
    o_lf, _ = chunk_gated_scan(ql, klf, vl, llf, s_cf)
    o_lb, _ = chunk_gated_scan(flip(ql), flip(klb), flip(vl), flip(llb), s_cb)
    out_lat = readout(o_lf + flip(o_lb), gl)
    out_ctx = readout(o_cf + flip(o_cb), gc) if with_ctx_out else None
    return out_lat, out_ctx


def swiglu(y, w1, w3, w2):
    return (jax.nn.silu(y @ w1) * (y @ w3)) @ w2


def moe_swiglu(y, w_router, w1, w3, w2):
    logits = (y @ w_router).astype(jnp.float32)
    top_val, top_idx = lax.top_k(logits, TOP_K)
    top_w = jax.nn.softmax(top_val, axis=-1)
    combine = jnp.sum(jax.nn.one_hot(top_idx, N_EXPERTS, dtype=jnp.float32) * top_w[..., None], axis=-2)
    combine = combine.astype(y.dtype)
    out = jnp.zeros_like(y)
    for e in range(N_EXPERTS):
        out = out + combine[..., e:e + 1] * swiglu(y, w1[e], w3[e], w2[e])
    return out


def setup_inputs(seed: int = 0) -> dict:
    key = jax.random.key(seed)
    ks = jax.random.split(key, 22)
    f32 = jnp.float32
    nrm = lambda k, shape, scale: jax.random.normal(k, shape, f32) * scale
    d = D_MODEL
    return {
        'x': nrm(ks[0], (BATCH, SEQ, d), 1.0),
        'c': nrm(ks[1], (BATCH, d), 1.0),
        'ctx': nrm(ks[2], (BATCH, CTX_LEN, d), 1.0),
        'c_ctx': nrm(ks[3], (d,), 1.0),
        'w_mod': nrm(ks[4], (DEPTH, d, N_MOD * d), 0.5 * d ** -0.5),
        'b_mod': nrm(ks[5], (DEPTH, N_MOD * d), 0.02),
        'norm_gains': 1.0 + nrm(ks[6], (DEPTH, 4, d), 0.05),
        'attn_w_in': nrm(ks[7], (N_EVEN, d, Q_DIM + 2 * KV_DIM), d ** -0.5),
        'attn_w_o': nrm(ks[8], (N_EVEN, Q_DIM, d), Q_DIM ** -0.5),
        'attn_q_gain': 1.0 + nrm(ks[9], (N_EVEN, HEAD_DIM), 0.05),
        'attn_k_gain': 1.0 + nrm(ks[10], (N_EVEN, HEAD_DIM), 0.05),
        'rec_w_in': nrm(ks[11], (N_ODD, d, 5 * d), d ** -0.5),
        'rec_w_o': nrm(ks[12], (N_ODD, d, d), d ** -0.5),
        'rec_lb_logits': nrm(ks[13], (DEPTH, 2, d), 0.1),
        'rec_o_gain': 1.0 + nrm(ks[14], (N_ODD, REC_DV), 0.05),
        'ffn_w1': nrm(ks[15], (N_EVEN, d, D_FF), d ** -0.5),
        'ffn_w3': nrm(ks[16], (N_EVEN, d, D_FF), d ** -0.5),
        'ffn_w2': nrm(ks[17], (N_EVEN, D_FF, d), D_FF ** -0.5),
        'moe_w_router': nrm(ks[18], (N_ODD, d, N_EXPERTS), d ** -0.5),
        'moe_w1': nrm(ks[19], (N_ODD, N_EXPERTS, d, D_FF_EXPERT), d ** -0.5),
        'moe_w3': nrm(ks[20], (N_ODD, N_EXPERTS, d, D_FF_EXPERT), d ** -0.5),
        'moe_w2': nrm(ks[21], (N_ODD, N_EXPERTS, D_FF_EXPERT, d), D_FF_EXPERT ** -0.5),
    }


def reference(x, c, ctx, c_ctx, w_mod, b_mod, norm_gains, attn_w_in, attn_w_o, attn_q_gain, attn_k_gain,
              rec_w_in, rec_w_o, rec_lb_logits, rec_o_gain, ffn_w1, ffn_w3, ffn_w2,
              moe_w_router, moe_w1, moe_w3, moe_w2):
    angles = axial_rope_angles(x.shape[1])
    lower_bounds = hgrn_lower_bounds(rec_lb_logits)
    h, g = x, ctx
    for layer in range(DEPTH):
        last = layer == DEPTH - 1
        j = layer // N_MIXERS
        ng = norm_gains[layer]
        sh1, sc1, gt1, sh2, sc2, gt2 = [m[:, None, :] for m in modulation(c, w_mod[layer], b_mod[layer])]
        csh1, csc1, cgt1, csh2, csc2, cgt2 = modulation(c_ctx, w_mod[layer], b_mod[layer])
        y_l = modulate(rms_norm(h, ng[0]), sh1, sc1)
        y_c = modulate(rms_norm(g, ng[0]), csh1, csc1)
        if layer % N_MIXERS == 0:
            mix_l, mix_c = attention_mixer(y_l, y_c, attn_w_in[j], attn_w_o[j], attn_q_gain[j],
                                           attn_k_gain[j], angles, not last)
        else:
            mix_l, mix_c = hgrn2_mixer(y_l, y_c, rec_w_in[j], rec_w_o[j], lower_bounds[layer],
                                       rec_o_gain[j], not last)
        h = h + gt1 * rms_norm(mix_l, ng[1])
        if not last:
            g = g + cgt1 * rms_norm(mix_c, ng[1])
        y_l = modulate(rms_norm(h, ng[2]), sh2, sc2)
        if layer % 2 == 0:
            ffn = lambda y: swiglu(y, ffn_w1[j], ffn_w3[j], ffn_w2[j])
        else:
            ffn = lambda y: moe_swiglu(y, moe_w_router[j], moe_w1[j], moe_w3[j], moe_w2[j])
        h = h + gt2 * rms_norm(ffn(y_l), ng[3])
        if not last:
            y_c = modulate(rms_norm(g, ng[2]), csh2, csc2)
            g = g + cgt2 * rms_norm(ffn(y_c), ng[3])
    return h
```

```python
import functools

import numpy as np
import jax
import jax.numpy as jnp
from jax import lax
from jax.experimental import pallas as pl
from jax.experimental.pallas import tpu as pltpu

F32 = jnp.float32
BF16 = jnp.bfloat16

HEAD_DIM = 128
N_KV_HEADS = 4
GROUP = 4
GRID_W = 64
ROPE_THETA = 10000.0
AXIS_ROT_DIM = HEAD_DIM // 2
REC_DK = 128
N_EXPERTS = 8
TOP_K = 2
N_MOD = 6
EPS = 1e-6
CHUNK = 64
N_LEVELS = 6

LANES = 128
VMEM_LIMIT = 56 * 2**20

ROW_TILE = 512
MOD_ROWS = 8


def _cparams(*sem):
    return pltpu.CompilerParams(dimension_semantics=sem, vmem_limit_bytes=VMEM_LIMIT)


def _mod_row(i, tm, seq, batch):
    return jnp.minimum((i * tm) // seq, batch)


def _mod_kernel(c_ref, w_ref, b_ref, o_ref):
    c = c_ref[...]
    s = (c * jax.nn.sigmoid(c)).astype(BF16)
    o_ref[...] = jnp.dot(s, w_ref[...].astype(BF16), preferred_element_type=F32) + b_ref[...]


def modulation_table(cond, w_mod, b_mod, tn=1024):
    depth, d, n = w_mod.shape
    out = pl.pallas_call(
        _mod_kernel,
        grid=(depth, n // tn),
        in_specs=[
            pl.BlockSpec((MOD_ROWS, d), lambda l, j: (0, 0)),
            pl.BlockSpec((None, d, tn), lambda l, j: (l, 0, j)),
            pl.BlockSpec((None, 1, tn), lambda l, j: (l, 0, j)),
        ],
        out_specs=pl.BlockSpec((None, MOD_ROWS, tn), lambda l, j: (l, 0, j)),
        out_shape=jax.ShapeDtypeStruct((depth, MOD_ROWS, n), F32),
        compiler_params=_cparams("arbitrary", "arbitrary"),
        name="modulation_table",
    )(cond, w_mod, b_mod.reshape(depth, 1, n))
    return out.reshape(depth, MOD_ROWS, 1, n)


def _norm_mod_kernel(h_ref, g_ref, sh_ref, sc_ref, o_ref):
    x = h_ref[...]
    ms = jnp.mean(x * x, axis=-1, keepdims=True)
    y = x * lax.rsqrt(ms + EPS) * g_ref[...]
    o_ref[...] = (y * (1.0 + sc_ref[...]) + sh_ref[...]).astype(o_ref.dtype)


def norm_modulate(h, gain, mods, shift_sec, scale_sec, seq, batch, out_dtype):
    rows, d = h.shape
    tm = ROW_TILE
    row = functools.partial(_mod_row, tm=tm, seq=seq, batch=batch)
    return pl.pallas_call(
        _norm_mod_kernel,
        grid=(rows // tm,),
        in_specs=[
            pl.BlockSpec((tm, d), lambda i: (i, 0)),
            pl.BlockSpec((1, d), lambda i: (0, 0)),
            pl.BlockSpec((None, 1, d), lambda i: (row(i), 0, shift_sec)),
            pl.BlockSpec((None, 1, d), lambda i: (row(i), 0, scale_sec)),
        ],
        out_specs=pl.BlockSpec((tm, d), lambda i: (i, 0)),
        out_shape=jax.ShapeDtypeStruct((rows, d), out_dtype),
        compiler_params=_cparams("arbitrary"),
        name="norm_modulate",
    )(h, gain.reshape(1, d), mods, mods)


def _mm_kernel(a_ref, b_ref, o_ref):
    o_ref[...] = jnp.dot(a_ref[...], b_ref[...], preferred_element_type=F32).astype(o_ref.dtype)


def matmul(a, b, out_dtype, tn):
    m, k = a.shape
    n = b.shape[1]
    tm = ROW_TILE
    return pl.pallas_call(
        _mm_kernel,
        grid=(m // tm, n // tn),
        in_specs=[pl.BlockSpec((tm, k), lambda i, j: (i, 0)),
                  pl.BlockSpec((k, tn), lambda i, j: (0, j))],
        out_specs=pl.BlockSpec((tm, tn), lambda i, j: (i, j)),
        out_shape=jax.ShapeDtypeStruct((m, n), out_dtype),
        compiler_params=_cparams("arbitrary", "arbitrary"),
        name="matmul",
    )(a, b)


def _swiglu_up_kernel(a_ref, w1_ref, w3_ref, o_ref):
    a = a_ref[...]
    u = jnp.dot(a, w1_ref[...], preferred_element_type=F32)
    g = jnp.dot(a, w3_ref[...], preferred_element_type=F32)
    o_ref[...] = (u * jax.nn.sigmoid(u) * g).astype(o_ref.dtype)


def swiglu_up(a, w1, w3, tn=512):
    m, k = a.shape
    n = w1.shape[1]
    tm = ROW_TILE
    return pl.pallas_call(
        _swiglu_up_kernel,
        grid=(m // tm, n // tn),
        in_specs=[pl.BlockSpec((tm, k), lambda i, j: (i, 0)),
                  pl.BlockSpec((k, tn), lambda i, j: (0, j)),
                  pl.BlockSpec((k, tn), lambda i, j: (0, j))],
        out_specs=pl.BlockSpec((tm, tn), lambda i, j: (i, j)),
        out_shape=jax.ShapeDtypeStruct((m, n), BF16),
        compiler_params=_cparams("arbitrary", "arbitrary"),
        name="swiglu_up",
    )(a, w1, w3)


def _residual_epilogue(r, h, gate, gain):
    ms = jnp.mean(r * r, axis=-1, keepdims=True)
    return h + gate * (r * lax.rsqrt(ms + EPS) * gain)


def _mm_resnorm_kernel(a_ref, b_ref, h_ref, gate_ref, gain_ref, o_ref, acc_ref, *, nk):
    kk = pl.program_id(1)
    part = jnp.dot(a_ref[...], b_ref[...], preferred_element_type=F32)

    if nk == 1:
        o_ref[...] = _residual_epilogue(part, h_ref[...], gate_ref[...], gain_ref[...])
        return

    @pl.when(kk == 0)
    def _():
        acc_ref[...] = part

    @pl.when(kk > 0)
    def _():
        acc_ref[...] += part

    @pl.when(kk == nk - 1)
    def _():
        o_ref[...] = _residual_epilogue(acc_ref[...], h_ref[...], gate_ref[...], gain_ref[...])


def matmul_resnorm(a, b, h, gain, mods, gate_sec, seq, batch, tk):
    m, k = a.shape
    d = b.shape[1]
    tm = ROW_TILE
    nk = k // tk
    row = functools.partial(_mod_row, tm=tm, seq=seq, batch=batch)
    return pl.pallas_call(
        functools.partial(_mm_resnorm_kernel, nk=nk),
        grid=(m // tm, nk),
        in_specs=[
            pl.BlockSpec((tm, tk), lambda i, kk: (i, kk)),
            pl.BlockSpec((tk, d), lambda i, kk: (kk, 0)),
            pl.BlockSpec((tm, d), lambda i, kk: (i, 0)),
            pl.BlockSpec((None, 1, d), lambda i, kk: (row(i), 0, gate_sec)),
            pl.BlockSpec((1, d), lambda i, kk: (0, 0)),
        ],
        out_specs=pl.BlockSpec((tm, d), lambda i, kk: (i, 0)),
        out_shape=jax.ShapeDtypeStruct((m, d), F32),
        scratch_shapes=[pltpu.VMEM((tm, d), F32)],
        compiler_params=_cparams("arbitrary", "arbitrary"),
        name="matmul_resnorm",
    )(a, b, h, mods, gain.reshape(1, d))


def rope_tables(seq, tile):
    rows = seq // GRID_W
    row = jnp.repeat(jnp.arange(rows, dtype=F32), GRID_W)
    col = jnp.tile(jnp.arange(GRID_W, dtype=F32), rows)
    inv_freq = ROPE_THETA ** (-jnp.arange(0, AXIS_ROT_DIM, 2, dtype=F32) / AXIS_ROT_DIM)
    ar, ac = row[:, None] * inv_freq, col[:, None] * inv_freq
    zeros = jnp.zeros_like(ar)
    cos = jnp.concatenate([jnp.cos(ar), jnp.cos(ar), jnp.cos(ac), jnp.cos(ac)], axis=-1)
    sin_lo = jnp.concatenate([-jnp.sin(ar), zeros, -jnp.sin(ac), zeros], axis=-1)
    sin_hi = jnp.concatenate([zeros, jnp.sin(ar), zeros, jnp.sin(ac)], axis=-1)
    pad0 = jnp.zeros((tile, HEAD_DIM), F32)
    return (jnp.concatenate([cos, jnp.ones((tile, HEAD_DIM), F32)], axis=0),
            jnp.concatenate([sin_lo, pad0], axis=0),
            jnp.concatenate([sin_hi, pad0], axis=0))


def _qk_prep_kernel(qkv_ref, qg_ref, kg_ref, cos_ref, slo_ref, shi_ref, q_ref, k_ref, *, n_q, n_k, q_scale):
    cos, slo, shi = cos_ref[...], slo_ref[...], shi_ref[...]
    quarter = AXIS_ROT_DIM // 2

    def head(x, gain):
        x = x.astype(F32)
        ms = jnp.mean(x * x, axis=-1, keepdims=True)
        xn = x * lax.rsqrt(ms + EPS) * gain
        return (xn * cos + pltpu.roll(xn, HEAD_DIM - quarter, axis=1) * slo
                + pltpu.roll(xn, quarter, axis=1) * shi)

    for hh in range(n_q):
        sl = slice(hh * HEAD_DIM, (hh + 1) * HEAD_DIM)
        q_ref[:, sl] = (head(qkv_ref[:, sl], qg_ref[...]) * q_scale).astype(q_ref.dtype)
    for hh in range(n_k):
        src = slice((n_q + hh) * HEAD_DIM, (n_q + hh + 1) * HEAD_DIM)
        k_ref[:, hh * HEAD_DIM:(hh + 1) * HEAD_DIM] = head(qkv_ref[:, src], kg_ref[...]).astype(k_ref.dtype)


def qk_prep(qkv, q_gain, k_gain, tables, seq, n_lat_rows):
    rows = qkv.shape[0]
    tm = ROW_TILE
    n_q = 4 * N_KV_HEADS
    per_batch = seq // tm
    n_lat_tiles = n_lat_rows // tm
    tab = lambda i: (jnp.where(i < n_lat_tiles, i % per_batch, per_batch), 0)
    q_scale = HEAD_DIM ** -0.5 * float(np.log2(np.e))
    return pl.pallas_call(
        functools.partial(_qk_prep_kernel, n_q=n_q, n_k=N_KV_HEADS, q_scale=q_scale),
        grid=(rows // tm,),
        in_specs=[
            pl.BlockSpec((tm, (n_q + N_KV_HEADS) * HEAD_DIM), lambda i: (i, 0)),
            pl.BlockSpec((1, HEAD_DIM), lambda i: (0, 0)),
            pl.BlockSpec((1, HEAD_DIM), lambda i: (0, 0)),
            pl.BlockSpec((tm, HEAD_DIM), tab),
            pl.BlockSpec((tm, HEAD_DIM), tab),
            pl.BlockSpec((tm, HEAD_DIM), tab),
        ],
        out_specs=[pl.BlockSpec((tm, n_q * HEAD_DIM), lambda i: (i, 0)),
                   pl.BlockSpec((tm, N_KV_HEADS * HEAD_DIM), lambda i: (i, 0))],
        out_shape=[jax.ShapeDtypeStruct((rows, n_q * HEAD_DIM), BF16),
                   jax.ShapeDtypeStruct((rows, N_KV_HEADS * HEAD_DIM), BF16)],
        compiler_params=_cparams("arbitrary"),
        name="qk_prep",
    )(qkv, q_gain.reshape(1, HEAD_DIM), k_gain.reshape(1, HEAD_DIM), *tables)


def _flash_kernel(q_ref, k_ref, v_ref, kc_ref, vc_ref, o_ref, m_ref, l_ref, acc_ref, *, n_lat_tiles, nkv, tq):
    qi = pl.program_id(1)
    ki = pl.program_id(2)
    q = jnp.concatenate([q_ref[:, g * HEAD_DIM:(g + 1) * HEAD_DIM] for g in range(GROUP)], axis=0)

    def update(k, v):
        s = lax.dot_general(q, k, (((1,), (1,)), ((), ())), preferred_element_type=F32)
        m_prev = m_ref[...]
        m_new = jnp.maximum(m_prev, jnp.max(s, axis=-1, keepdims=True))
        alpha = jnp.exp2(m_prev - m_new)
        p = jnp.exp2(s - m_new)
        l_ref[...] = alpha * l_ref[...] + jnp.sum(p, axis=-1, keepdims=True)
        acc_ref[...] = alpha * acc_ref[...] + jnp.dot(p.astype(BF16), v, preferred_element_type=F32)
        m_ref[...] = m_new

    @pl.when(ki == 0)
    def _():
        m_ref[...] = jnp.full(m_ref.shape, -jnp.inf, F32)
        l_ref[...] = jnp.zeros(l_ref.shape, F32)
        acc_ref[...] = jnp.zeros(acc_ref.shape, F32)
        update(kc_ref[...], vc_ref[...])

    @pl.when(qi < n_lat_tiles)
    def _():
        update(k_ref[...], v_ref[...])

    @pl.when(ki == nkv - 1)
    def _():
        o = acc_ref[...] / l_ref[...]
        for g in range(GROUP):
            o_ref[:, g * HEAD_DIM:(g + 1) * HEAD_DIM] = o[g * tq:(g + 1) * tq].astype(o_ref.dtype)


def flash_attention(q, k, qkv, seq, ctx, batch, tq, tk):
    rows = q.shape[0]
    n_q_heads = GROUP * N_KV_HEADS
    per_batch_q = seq // tq
    n_lat_tiles = batch * per_batch_q
    n_ctx_tiles = batch * (ctx // tq)
    nkv = seq // tk
    ctx_blk0 = (batch * seq) // ctx
    v_col0 = n_q_heads + N_KV_HEADS

    def batch_of(qi):
        return jnp.where(qi < n_lat_tiles, qi // per_batch_q, (qi - n_lat_tiles) // (ctx // tq))

    def kv_row(qi, ki):
        return jnp.where(qi < n_lat_tiles, batch_of(qi) * nkv + ki, 0)

    grid = (N_KV_HEADS, n_lat_tiles + n_ctx_tiles, nkv)
    return pl.pallas_call(
        functools.partial(_flash_kernel, n_lat_tiles=n_lat_tiles, nkv=nkv, tq=tq),
        grid=grid,
        in_specs=[
            pl.BlockSpec((tq, GROUP * HEAD_DIM), lambda h, qi, ki: (qi, h)),
            pl.BlockSpec((tk, HEAD_DIM), lambda h, qi, ki: (kv_row(qi, ki), h)),
            pl.BlockSpec((tk, HEAD_DIM), lambda h, qi, ki: (kv_row(qi, ki), v_col0 + h)),
            pl.BlockSpec((ctx, HEAD_DIM), lambda h, qi, ki: (ctx_blk0 + batch_of(qi), h)),
            pl.BlockSpec((ctx, HEAD_DIM), lambda h, qi, ki: (ctx_blk0 + batch_of(qi), v_col0 + h)),
        ],
        out_specs=pl.BlockSpec((tq, GROUP * HEAD_DIM), lambda h, qi, ki: (qi, h)),
        out_shape=jax.ShapeDtypeStruct((rows, n_q_heads * HEAD_DIM), BF16),
        scratch_shapes=[pltpu.VMEM((GROUP * tq, 1), F32), pltpu.VMEM((GROUP * tq, 1), F32),
                        pltpu.VMEM((GROUP * tq, HEAD_DIM), F32)],
        compiler_params=_cparams("arbitrary", "arbitrary", "arbitrary"),
        name="flash_attention",
    )(q, k, qkv, k, qkv)


def _scan_constants(reverse):
    t = np.arange(CHUNK)[:, None]
    u = np.arange(CHUNK)[None, :]
    sel = [(u >= t) if reverse else (u <= t)]
    masks = [t == u]
    for lvl in range(N_LEVELS):
        b = CHUNK >> (lvl + 1)
        base = t & ~(2 * b - 1)
        ref = base + b if reverse else base + b - 1
        sel.append((u >= ref) if reverse else (u <= ref))
        same = (t & ~(2 * b - 1)) == (u & ~(2 * b - 1))
        t_hi, s_hi = (t & b) != 0, (u & b) != 0
        masks.append(same & (~t_hi & s_hi if reverse else t_hi & ~s_hi))
    return (jnp.asarray(np.concatenate(sel, axis=0), F32),
            jnp.asarray(np.stack(masks, axis=0), F32))


def _scan_kernel(q_ref, v_ref, f_ref, lb_ref, sel_ref, mask_ref, o_ref, st_ref, *, reverse, n_chunks):
    step = pl.program_id(2)

    @pl.when(step == 0)
    def _():
        st_ref[...] = jnp.zeros(st_ref.shape, F32)

    lb = lb_ref[...]
    sel = sel_ref[...]
    nt = (((1,), (1,)), ((), ()))
    tn = (((0,), (0,)), ((), ()))
    order = range(n_chunks - 1, -1, -1) if reverse else range(n_chunks)
    for c in order:
        rows = slice(c * CHUNK, (c + 1) * CHUNK)
        qr = q_ref[rows, :].astype(F32)
        v = v_ref[rows, :]
        q = qr * jax.nn.sigmoid(qr) * (REC_DK ** -0.5)
        fg = lb + (1.0 - lb) * jax.nn.sigmoid(f_ref[rows, :].astype(F32))
        k = 1.0 - fg
        sums = jnp.dot(sel, jnp.log(fg), preferred_element_type=F32, precision=lax.Precision.HIGHEST)
        cb = sums[0:CHUNK]
        total = cb[0:1] if reverse else cb[CHUNK - 1:CHUNK]
        st = st_ref[...]

        o = lax.dot_general((q * jnp.exp(cb)).astype(BF16), st.astype(BF16), nt, preferred_element_type=F32)
        qb, kb = q.astype(BF16), k.astype(BF16)
        a = mask_ref[0] * lax.dot_general(qb, kb, nt, preferred_element_type=F32)
        for lvl in range(N_LEVELS):
            ref = sums[(lvl + 1) * CHUNK:(lvl + 2) * CHUNK]
            ql = (q * jnp.exp(jnp.minimum(cb - ref, 0.0))).astype(BF16)
            kl = (k * jnp.exp(jnp.minimum(ref - cb, 0.0))).astype(BF16)
            a = a + mask_ref[lvl + 1] * lax.dot_general(ql, kl, nt, preferred_element_type=F32)
        o = o + jnp.dot(a.astype(BF16), v, preferred_element_type=F32)
        o_ref[rows, :] = o.astype(o_ref.dtype)

        kd = (k * jnp.exp(total - cb)).astype(BF16)
        st_ref[...] = st * jnp.exp(total) + lax.dot_general(v, kd, tn, preferred_element_type=F32)


def hgrn_scan(proj, lb_dir, f_sec, reverse, seq, ctx, batch, n_heads):
    rows = proj.shape[0]
    tb = ctx
    per_batch = seq // tb
    ctx_blk0 = (batch * seq) // tb
    sel, masks = _scan_constants(reverse)

    def row_blk(b, s):
        lat = b * per_batch + (per_batch - s if reverse else s - 1)
        return jnp.where(s == 0, ctx_blk0 + b, lat)

    return pl.pallas_call(
        functools.partial(_scan_kernel, reverse=reverse, n_chunks=tb // CHUNK),
        grid=(batch, n_heads, 1 + per_batch),
        in_specs=[
            pl.BlockSpec((tb, REC_DK), lambda b, h, s: (row_blk(b, s), h)),
            pl.BlockSpec((tb, REC_DK), lambda b, h, s: (row_blk(b, s), n_heads + h)),
            pl.BlockSpec((tb, REC_DK), lambda b, h, s: (row_blk(b, s), f_sec * n_heads + h)),
            pl.BlockSpec((None, 1, REC_DK), lambda b, h, s: (h, 0, 0)),
            pl.BlockSpec(sel.shape, lambda b, h, s: (0, 0)),
            pl.BlockSpec(masks.shape, lambda b, h, s: (0, 0, 0)),
        ],
        out_specs=pl.BlockSpec((tb, REC_DK), lambda b, h, s: (row_blk(b, s), h)),
        out_shape=jax.ShapeDtypeStruct((rows, n_heads * REC_DK), BF16),
        scratch_shapes=[pltpu.VMEM((REC_DK, REC_DK), F32)],
        compiler_params=_cparams("arbitrary", "arbitrary", "arbitrary"),
        name="hgrn_scan_bw" if reverse else "hgrn_scan_fw",
    )(proj, proj, proj, lb_dir.reshape(n_heads, 1, REC_DK), sel, masks)


def _readout_kernel(of_ref, ob_ref, g_ref, gain_ref, o_ref, *, n_heads):
    gain = gain_ref[...]
    for hh in range(n_heads):
        sl = slice(hh * REC_DK, (hh + 1) * REC_DK)
        o = of_ref[:, sl].astype(F32) + ob_ref[:, sl].astype(F32)
        ms = jnp.mean(o * o, axis=-1, keepdims=True)
        g = g_ref[:, sl].astype(F32)
        o_ref[:, sl] = (o * lax.rsqrt(ms + EPS) * gain * (g * jax.nn.sigmoid(g))).astype(o_ref.dtype)


def hgrn_readout(o_fw, o_bw, proj, o_gain, gate_sec, n_heads):
    rows, d = o_fw.shape
    tm = ROW_TILE
    return pl.pallas_call(
        functools.partial(_readout_kernel, n_heads=n_heads),
        grid=(rows // tm,),
        in_specs=[pl.BlockSpec((tm, d), lambda i: (i, 0)),
                  pl.BlockSpec((tm, d), lambda i: (i, 0)),
                  pl.BlockSpec((tm, d), lambda i: (i, gate_sec)),
                  pl.BlockSpec((1, REC_DK), lambda i: (0, 0))],
        out_specs=pl.BlockSpec((tm, d), lambda i: (i, 0)),
        out_shape=jax.ShapeDtypeStruct((rows, d), BF16),
        compiler_params=_cparams("arbitrary"),
        name="hgrn_readout",
    )(o_fw, o_bw, proj, o_gain.reshape(1, REC_DK))


def _router_kernel(y_ref, w_ref, o_ref):
    logits = jnp.dot(y_ref[...], w_ref[...], preferred_element_type=F32, precision=lax.Precision.HIGHEST)
    lane = lax.broadcasted_iota(jnp.int32, logits.shape, 1)
    neg = -jnp.inf
    logits = jnp.where(lane < N_EXPERTS, logits, neg)
    m1 = jnp.max(logits, axis=-1, keepdims=True)
    i1 = jnp.min(jnp.where(logits == m1, lane, LANES), axis=-1, keepdims=True)
    rest = jnp.where(lane == i1, neg, logits)
    m2 = jnp.max(rest, axis=-1, keepdims=True)
    i2 = jnp.min(jnp.where(rest == m2, lane, LANES), axis=-1, keepdims=True)
    e = jnp.exp(m2 - m1)
    w1 = 1.0 / (1.0 + e)
    w2 = e / (1.0 + e)
    out = jnp.where(lane == 0, i1.astype(F32),
                    jnp.where(lane == 1, i2.astype(F32),
                              jnp.where(lane == 2, w1, jnp.where(lane == 3, w2, 0.0))))
    o_ref[...] = out


def moe_router(y, w_router):
    rows, d = y.shape
    tm = ROW_TILE
    w_pad = jnp.zeros((d, LANES), F32).at[:, :N_EXPERTS].set(w_router)
    return pl.pallas_call(
        _router_kernel,
        grid=(rows // tm,),
        in_specs=[pl.BlockSpec((tm, d), lambda i: (i, 0)), pl.BlockSpec((d, LANES), lambda i: (0, 0))],
        out_specs=pl.BlockSpec((tm, LANES), lambda i: (i, 0)),
        out_shape=jax.ShapeDtypeStruct((rows, LANES), F32),
        compiler_params=_cparams("arbitrary"),
        name="moe_router",
    )(y, w_pad)


def _row_copy(src_ref, dst_ref, sem, src_row, dst_row):
    return pltpu.make_async_copy(src_ref.at[pl.ds(src_row, 1), :], dst_ref.at[pl.ds(dst_row, 1), :], sem)


def _gather_rows_into(idx_ref, src_ref, dst_ref, sem):
    n = dst_ref.shape[0]

    def start(r, carry):
        _row_copy(src_ref, dst_ref, sem, idx_ref[0, r], r).start()
        return carry

    def wait(r, carry):
        _row_copy(src_ref, dst_ref, sem, 0, r).wait()
        return carry

    lax.fori_loop(0, n, start, 0)
    lax.fori_loop(0, n, wait, 0)


def _gather_kernel(idx_ref, src_ref, o_ref, sem):
    _gather_rows_into(idx_ref, src_ref, o_ref, sem)


def gather_rows(src, idx, tile):
    n = idx.shape[0]
    d = src.shape[1]
    return pl.pallas_call(
        _gather_kernel,
        grid=(n // tile,),
        in_specs=[pl.BlockSpec((None, 1, tile), lambda i: (i, 0, 0), memory_space=pltpu.SMEM),
                  pl.BlockSpec(memory_space=pl.ANY)],
        out_specs=pl.BlockSpec((tile, d), lambda i: (i, 0)),
        out_shape=jax.ShapeDtypeStruct((n, d), src.dtype),
        scratch_shapes=[pltpu.SemaphoreType.DMA(())],
        compiler_params=_cparams("arbitrary"),
        name="moe_gather_rows",
    )(idx.reshape(n // tile, 1, tile), src)


def _expert_up_kernel(te_ref, ta_ref, x_ref, w1_ref, w3_ref, o_ref):
    i = pl.program_id(0)

    @pl.when(ta_ref[i] > 0)
    def _():
        x = x_ref[...].astype(BF16)
        u = jnp.dot(x, w1_ref[...], preferred_element_type=F32)
        g = jnp.dot(x, w3_ref[...], preferred_element_type=F32)
        o_ref[...] = (u * jax.nn.sigmoid(u) * g).astype(o_ref.dtype)

    @pl.when(ta_ref[i] == 0)
    def _():
        o_ref[...] = jnp.zeros(o_ref.shape, o_ref.dtype)


def expert_up(x, w1, w3, tile_expert, tile_active, tn=1024):
    p, d = x.shape
    f = w1.shape[2]
    tm = ROW_TILE
    return pl.pallas_call(
        _expert_up_kernel,
        grid_spec=pltpu.PrefetchScalarGridSpec(
            num_scalar_prefetch=2,
            grid=(p // tm, f // tn),
            in_specs=[pl.BlockSpec((tm, d), lambda i, j, te, ta: (i, 0)),
                      pl.BlockSpec((None, d, tn), lambda i, j, te, ta: (te[i], 0, j)),
                      pl.BlockSpec((None, d, tn), lambda i, j, te, ta: (te[i], 0, j))],
            out_specs=pl.BlockSpec((tm, tn), lambda i, j, te, ta: (i, j)),
        ),
        out_shape=jax.ShapeDtypeStruct((p, f), BF16),
        compiler_params=_cparams("arbitrary", "arbitrary"),
        name="moe_expert_up",
    )(tile_expert, tile_active, x, w1, w3)


def _expert_down_kernel(te_ref, ta_ref, a_ref, w_ref, rw_ref, o_ref, acc_ref, *, nk):
    i = pl.program_id(0)
    kk = pl.program_id(1)

    @pl.when(ta_ref[i] > 0)
    def _():
        part = jnp.dot(a_ref[...], w_ref[...], preferred_element_type=F32)

        @pl.when(kk == 0)
        def _():
            acc_ref[...] = part

        @pl.when(kk > 0)
        def _():
            acc_ref[...] += part

        @pl.when(kk == nk - 1)
        def _():
            o_ref[...] = acc_ref[...] * rw_ref[...]

    @pl.when(jnp.logical_and(ta_ref[i] == 0, kk == nk - 1))
    def _():
        o_ref[...] = jnp.zeros(o_ref.shape, o_ref.dtype)


def expert_down(a, w2, row_weight, tile_expert, tile_active, tk=2048):
    p, f = a.shape
    d = w2.shape[2]
    tm = ROW_TILE
    nk = f // tk
    return pl.pallas_call(
        functools.partial(_expert_down_kernel, nk=nk),
        grid_spec=pltpu.PrefetchScalarGridSpec(
            num_scalar_prefetch=2,
            grid=(p // tm, nk),
            in_specs=[pl.BlockSpec((tm, tk), lambda i, kk, te, ta: (i, kk)),
                      pl.BlockSpec((None, tk, d), lambda i, kk, te, ta: (te[i], kk, 0)),
                      pl.BlockSpec((tm, 1), lambda i, kk, te, ta: (i, 0))],
            out_specs=pl.BlockSpec((tm, d), lambda i, kk, te, ta: (i, 0)),
            scratch_shapes=[pltpu.VMEM((tm, d), F32)],
        ),
        out_shape=jax.ShapeDtypeStruct((p, d), F32),
        compiler_params=_cparams("arbitrary", "arbitrary"),
        name="moe_expert_down",
    )(tile_expert, tile_active, a, w2, row_weight)


def _combine_kernel(s0_ref, s1_ref, src_ref, h_ref, gate_ref, gain_ref, o_ref, buf0, buf1, sem):
    _gather_rows_into(s0_ref, src_ref, buf0, sem.at[0])
    _gather_rows_into(s1_ref, src_ref, buf1, sem.at[1])
    o_ref[...] = _residual_epilogue(buf0[...] + buf1[...], h_ref[...], gate_ref[...], gain_ref[...])


def moe_combine_resnorm(expert_out, slot0, slot1, h, gain, mods, gate_sec, seq, batch, tile=256):
    rows, d = h.shape
    row = functools.partial(_mod_row, tm=tile, seq=seq, batch=batch)
    idx_spec = pl.BlockSpec((None, 1, tile), lambda i: (i, 0, 0), memory_space=pltpu.SMEM)
    return pl.pallas_call(
        _combine_kernel,
        grid=(rows // tile,),
        in_specs=[idx_spec, idx_spec,
                  pl.BlockSpec(memory_space=pl.ANY),
                  pl.BlockSpec((tile, d), lambda i: (i, 0)),
                  pl.BlockSpec((None, 1, d), lambda i: (row(i), 0, gate_sec)),
                  pl.BlockSpec((1, d), lambda i: (0, 0))],
        out_specs=pl.BlockSpec((tile, d), lambda i: (i, 0)),
        out_shape=jax.ShapeDtypeStruct((rows, d), F32),
        scratch_shapes=[pltpu.VMEM((tile, d), F32), pltpu.VMEM((tile, d), F32), pltpu.SemaphoreType.DMA((2,))],
        compiler_params=_cparams("arbitrary"),
        name="moe_combine_resnorm",
    )(slot0.reshape(rows // tile, 1, tile), slot1.reshape(rows // tile, 1, tile), expert_out, h, mods,
      gain.reshape(1, d))


def _routing_plan(route, tm):
    rows = route.shape[0]
    experts = route[:, :TOP_K].astype(jnp.int32).reshape(-1)
    weights = route[:, TOP_K:2 * TOP_K].reshape(-1)
    onehot = (experts[:, None] == jnp.arange(N_EXPERTS)[None, :]).astype(jnp.int32)
    rank = jnp.take_along_axis(jnp.cumsum(onehot, axis=0) - onehot, experts[:, None], axis=1)[:, 0]
    counts = jnp.sum(onehot, axis=0)
    tiles_per = (counts + tm - 1) // tm
    tile_end = jnp.cumsum(tiles_per)
    start = (tile_end - tiles_per) * tm
    slot = start[experts] + rank
    n_slots = rows * TOP_K + N_EXPERTS * tm
    n_tiles = n_slots // tm
    token_of_slot = jnp.zeros((n_slots,), jnp.int32).at[slot].set(jnp.arange(rows * TOP_K, dtype=jnp.int32) // TOP_K)
    weight_of_slot = jnp.zeros((n_slots,), F32).at[slot].set(weights)
    tile_ids = jnp.arange(n_tiles, dtype=jnp.int32)
    tile_expert = jnp.minimum(jnp.searchsorted(tile_end, tile_ids, side="right"), N_EXPERTS - 1).astype(jnp.int32)
    tile_active = (tile_ids < tile_end[-1]).astype(jnp.int32)
    slots = slot.reshape(rows, TOP_K).astype(jnp.int32)
    return token_of_slot, weight_of_slot.reshape(n_slots, 1), tile_expert, tile_active, slots[:, 0], slots[:, 1]


def moe_ffn_resnorm(y, h, w_router, w1, w3, w2, gain, mods, gate_sec, seq, batch):
    route = moe_router(y, w_router)
    token_of_slot, weight_of_slot, tile_expert, tile_active, slot0, slot1 = _routing_plan(route, ROW_TILE)
    x = gather_rows(y, token_of_slot, tile=256)
    hid = expert_up(x, w1, w3, tile_expert, tile_active)
    out = expert_down(hid, w2, weight_of_slot, tile_expert, tile_active)
    return moe_combine_resnorm(out, slot0, slot1, h, gain, mods, gate_sec, seq, batch)


def _lower_bounds(logits):
    p = jax.nn.softmax(logits.astype(F32), axis=0)
    return jnp.cumsum(p, axis=0) - p[0]


def kernel(x, c, ctx, c_ctx, w_mod, b_mod, norm_gains, attn_w_in, attn_w_o, attn_q_gain, attn_k_gain,
           rec_w_in, rec_w_o, rec_lb_logits, rec_o_gain, ffn_w1, ffn_w3, ffn_w2,
           moe_w_router, moe_w1, moe_w3, moe_w2):
    batch, seq, d = x.shape
    ctx_len = ctx.shape[1]
    depth = w_mod.shape[0]
    n_lat = batch * seq
    n_heads = d // REC_DK
    assert seq % ROW_TILE == 0 and (batch * ctx_len) % ROW_TILE == 0 and batch + 1 <= MOD_ROWS

    hg = jnp.concatenate([x.reshape(n_lat, d), ctx.reshape(batch * ctx_len, d)], axis=0)
    cond = jnp.zeros((MOD_ROWS, d), F32).at[:batch].set(c).at[batch].set(c_ctx)
    mods_all = modulation_table(cond, w_mod, b_mod)
    lower = _lower_bounds(rec_lb_logits)
    tables = rope_tables(seq, ROW_TILE)
    tq = min(256, ctx_len)
    tk = min(1024, seq)

    for layer in range(depth):
        j = layer // 2
        ng = norm_gains[layer]
        mods = mods_all[layer]
        y = norm_modulate(hg, ng[0], mods, 0, 1, seq, batch, BF16)
        if layer % 2 == 0:
            qkv = matmul(y, attn_w_in[j].astype(BF16), BF16, tn=1024)
            q, k = qk_prep(qkv, attn_q_gain[j], attn_k_gain[j], tables, seq, n_lat)
            mix = flash_attention(q, k, qkv, seq, ctx_len, batch, tq, tk)
            hg = matmul_resnorm(mix, attn_w_o[j].astype(BF16), hg, ng[1], mods, 2, seq, batch, tk=d)
        else:
            proj = matmul(y, rec_w_in[j].astype(BF16), BF16, tn=1024)
            o_fw = hgrn_scan(proj, lower[layer, 0], 2, False, seq, ctx_len, batch, n_heads)
            o_bw = hgrn_scan(proj, lower[layer, 1], 3, True, seq, ctx_len, batch, n_heads)
            mix = hgrn_readout(o_fw, o_bw, proj, rec_o_gain[j], 4, n_heads)
            hg = matmul_resnorm(mix, rec_w_o[j].astype(BF16), hg, ng[1], mods, 2, seq, batch, tk=d)
        if layer % 2 == 0:
            y = norm_modulate(hg, ng[2], mods, 3, 4, seq, batch, BF16)
            hid = swiglu_up(y, ffn_w1[j].astype(BF16), ffn_w3[j].astype(BF16))
            f = hid.shape[1]
            hg = matmul_resnorm(hid, ffn_w2[j].astype(BF16), hg, ng[3], mods, 5, seq, batch, tk=f // 4)
        else:
            y = norm_modulate(hg, ng[2], mods, 3, 4, seq, batch, F32)
            hg = moe_ffn_resnorm(y, hg, moe_w_router[j], moe_w1[j].astype(BF16), moe_w3[j].astype(BF16),
                                 moe_w2[j].astype(BF16), ng[3], mods, 5, seq, batch)
    return hg[:n_lat].reshape(batch, seq, d)
```

```python
import functools

import numpy as np
import jax
import jax.numpy as jnp
from jax import lax
from jax.experimental import pallas as pl
from jax.experimental.pallas import tpu as pltpu

F32 = jnp.float32
BF16 = jnp.bfloat16

HEAD_DIM = 128
N_KV_HEADS = 4
GROUP = 4
GRID_W = 64
ROPE_THETA = 10000.0
AXIS_ROT_DIM = HEAD_DIM // 2
REC_DK = 128
N_EXPERTS = 8
TOP_K = 2
N_MOD = 6
EPS = 1e-6
CHUNK = 64
N_LEVELS = 6

LANES = 128
VMEM_LIMIT = 56 * 2**20

ROW_TILE = 512
MOD_ROWS = 8


def _cparams(*sem):
    return pltpu.CompilerParams(dimension_semantics=sem, vmem_limit_bytes=VMEM_LIMIT)


def _mod_row(i, tm, seq, batch):
    return jnp.minimum((i * tm) // seq, batch)


def _mod_kernel(c_ref, w_ref, b_ref, o_ref):
    c = c_ref[...]
    s = (c * jax.nn.sigmoid(c)).astype(BF16)
    o_ref[...] = jnp.dot(s, w_ref[...].astype(BF16), preferred_element_type=F32) + b_ref[...]


def modulation_table(cond, w_mod, b_mod, tn=1024):
    depth, d, n = w_mod.shape
    out = pl.pallas_call(
        _mod_kernel,
        grid=(depth, n // tn),
        in_specs=[
            pl.BlockSpec((MOD_ROWS, d), lambda l, j: (0, 0)),
            pl.BlockSpec((None, d, tn), lambda l, j: (l, 0, j)),
            pl.BlockSpec((None, 1, tn), lambda l, j: (l, 0, j)),
        ],
        out_specs=pl.BlockSpec((None, MOD_ROWS, tn), lambda l, j: (l, 0, j)),
        out_shape=jax.ShapeDtypeStruct((depth, MOD_ROWS, n), F32),
        compiler_params=_cparams("arbitrary", "arbitrary"),
        name="modulation_table",
    )(cond, w_mod, b_mod.reshape(depth, 1, n))
    return out.reshape(depth, MOD_ROWS, 1, n)


def _norm_mod_kernel(h_ref, g_ref, sh_ref, sc_ref, o_ref):
    x = h_ref[...]
    ms = jnp.mean(x * x, axis=-1, keepdims=True)
    y = x * lax.rsqrt(ms + EPS) * g_ref[...]
    o_ref[...] = (y * (1.0 + sc_ref[...]) + sh_ref[...]).astype(o_ref.dtype)


def norm_modulate(h, gain, mods, shift_sec, scale_sec, seq, batch, out_dtype):
    rows, d = h.shape
    tm = ROW_TILE
    row = functools.partial(_mod_row, tm=tm, seq=seq, batch=batch)
    return pl.pallas_call(
        _norm_mod_kernel,
        grid=(rows // tm,),
        in_specs=[
            pl.BlockSpec((tm, d), lambda i: (i, 0)),
            pl.BlockSpec((1, d), lambda i: (0, 0)),
            pl.BlockSpec((None, 1, d), lambda i: (row(i), 0, shift_sec)),
            pl.BlockSpec((None, 1, d), lambda i: (row(i), 0, scale_sec)),
        ],
        out_specs=pl.BlockSpec((tm, d), lambda i: (i, 0)),
        out_shape=jax.ShapeDtypeStruct((rows, d), out_dtype),
        compiler_params=_cparams("arbitrary"),
        name="norm_modulate",
    )(h, gain.reshape(1, d), mods, mods)


def _mm_kernel(a_ref, b_ref, o_ref):
    o_ref[...] = jnp.dot(a_ref[...], b_ref[...], preferred_element_type=F32).astype(o_ref.dtype)


def matmul(a, b, out_dtype, tn):
    m, k = a.shape
    n = b.shape[1]
    tm = ROW_TILE
    return pl.pallas_call(
        _mm_kernel,
        grid=(m // tm, n // tn),
        in_specs=[pl.BlockSpec((tm, k), lambda i, j: (i, 0)),
                  pl.BlockSpec((k, tn), lambda i, j: (0, j))],
        out_specs=pl.BlockSpec((tm, tn), lambda i, j: (i, j)),
        out_shape=jax.ShapeDtypeStruct((m, n), out_dtype),
        compiler_params=_cparams("arbitrary", "arbitrary"),
        name="matmul",
    )(a, b)


def _swiglu_up_kernel(a_ref, w1_ref, w3_ref, o_ref):
    a = a_ref[...]
    u = jnp.dot(a, w1_ref[...], preferred_element_type=F32)
    g = jnp.dot(a, w3_ref[...], preferred_element_type=F32)
    o_ref[...] = (u * jax.nn.sigmoid(u) * g).astype(o_ref.dtype)


def swiglu_up(a, w1, w3, tn=512):
    m, k = a.shape
    n = w1.shape[1]
    tm = ROW_TILE
    return pl.pallas_call(
        _swiglu_up_kernel,
        grid=(m // tm, n // tn),
        in_specs=[pl.BlockSpec((tm, k), lambda i, j: (i, 0)),
                  pl.BlockSpec((k, tn), lambda i, j: (0, j)),
                  pl.BlockSpec((k, tn), lambda i, j: (0, j))],
        out_specs=pl.BlockSpec((tm, tn), lambda i, j: (i, j)),
        out_shape=jax.ShapeDtypeStruct((m, n), BF16),
        compiler_params=_cparams("arbitrary", "arbitrary"),
        name="swiglu_up",
    )(a, w1, w3)


def _residual_epilogue(r, h, gate, gain):
    ms = jnp.mean(r * r, axis=-1, keepdims=True)
    return h + gate * (r * lax.rsqrt(ms + EPS) * gain)


def _mm_resnorm_kernel(a_ref, b_ref, h_ref, gate_ref, gain_ref, o_ref, acc_ref, *, nk):
    kk = pl.program_id(1)
    part = jnp.dot(a_ref[...], b_ref[...], preferred_element_type=F32)

    if nk == 1:
        o_ref[...] = _residual_epilogue(part, h_ref[...], gate_ref[...], gain_ref[...])
        return

    @pl.when(kk == 0)
    def _():
        acc_ref[...] = part

    @pl.when(kk > 0)
    def _():
        acc_ref[...] += part

    @pl.when(kk == nk - 1)
    def _():
        o_ref[...] = _residual_epilogue(acc_ref[...], h_ref[...], gate_ref[...], gain_ref[...])


def matmul_resnorm(a, b, h, gain, mods, gate_sec, seq, batch, tk):
    m, k = a.shape
    d = b.shape[1]
    tm = ROW_TILE
    nk = k // tk
    row = functools.partial(_mod_row, tm=tm, seq=seq, batch=batch)
    return pl.pallas_call(
        functools.partial(_mm_resnorm_kernel, nk=nk),
        grid=(m // tm, nk),
        in_specs=[
            pl.BlockSpec((tm, tk), lambda i, kk: (i, kk)),
            pl.BlockSpec((tk, d), lambda i, kk: (kk, 0)),
            pl.BlockSpec((tm, d), lambda i, kk: (i, 0)),
            pl.BlockSpec((None, 1, d), lambda i, kk: (row(i), 0, gate_sec)),
            pl.BlockSpec((1, d), lambda i, kk: (0, 0)),
        ],
        out_specs=pl.BlockSpec((tm, d), lambda i, kk: (i, 0)),
        out_shape=jax.ShapeDtypeStruct((m, d), F32),
        scratch_shapes=[pltpu.VMEM((tm, d), F32)],
        compiler_params=_cparams("arbitrary", "arbitrary"),
        name="matmul_resnorm",
    )(a, b, h, mods, gain.reshape(1, d))


def rope_tables(seq, tile):
    rows = seq // GRID_W
    row = jnp.repeat(jnp.arange(rows, dtype=F32), GRID_W)
    col = jnp.tile(jnp.arange(GRID_W, dtype=F32), rows)
    inv_freq = ROPE_THETA ** (-jnp.arange(0, AXIS_ROT_DIM, 2, dtype=F32) / AXIS_ROT_DIM)
    ar, ac = row[:, None] * inv_freq, col[:, None] * inv_freq
    zeros = jnp.zeros_like(ar)
    cos = jnp.concatenate([jnp.cos(ar), jnp.cos(ar), jnp.cos(ac), jnp.cos(ac)], axis=-1)
    sin_lo = jnp.concatenate([-jnp.sin(ar), zeros, -jnp.sin(ac), zeros], axis=-1)
    sin_hi = jnp.concatenate([zeros, jnp.sin(ar), zeros, jnp.sin(ac)], axis=-1)
    pad0 = jnp.zeros((tile, HEAD_DIM), F32)
    return (jnp.concatenate([cos, jnp.ones((tile, HEAD_DIM), F32)], axis=0),
            jnp.concatenate([sin_lo, pad0], axis=0),
            jnp.concatenate([sin_hi, pad0], axis=0))


def _qk_prep_kernel(qkv_ref, qg_ref, kg_ref, cos_ref, slo_ref, shi_ref, q_ref, k_ref, *, n_q, n_k, q_scale):
    cos, slo, shi = cos_ref[...], slo_ref[...], shi_ref[...]
    quarter = AXIS_ROT_DIM // 2

    def head(x, gain):
        x = x.astype(F32)
        ms = jnp.mean(x * x, axis=-1, keepdims=True)
        xn = x * lax.rsqrt(ms + EPS) * gain
        return (xn * cos + pltpu.roll(xn, HEAD_DIM - quarter, axis=1) * slo
                + pltpu.roll(xn, quarter, axis=1) * shi)

    for hh in range(n_q):
        sl = slice(hh * HEAD_DIM, (hh + 1) * HEAD_DIM)
        q_ref[:, sl] = (head(qkv_ref[:, sl], qg_ref[...]) * q_scale).astype(q_ref.dtype)
    for hh in range(n_k):
        src = slice((n_q + hh) * HEAD_DIM, (n_q + hh + 1) * HEAD_DIM)
        k_ref[:, hh * HEAD_DIM:(hh + 1) * HEAD_DIM] = head(qkv_ref[:, src], kg_ref[...]).astype(k_ref.dtype)


def qk_prep(qkv, q_gain, k_gain, tables, seq, n_lat_rows):
    rows = qkv.shape[0]
    tm = ROW_TILE
    n_q = 4 * N_KV_HEADS
    per_batch = seq // tm
    n_lat_tiles = n_lat_rows // tm
    tab = lambda i: (jnp.where(i < n_lat_tiles, i % per_batch, per_batch), 0)
    q_scale = HEAD_DIM ** -0.5 * float(np.log2(np.e))
    return pl.pallas_call(
        functools.partial(_qk_prep_kernel, n_q=n_q, n_k=N_KV_HEADS, q_scale=q_scale),
        grid=(rows // tm,),
        in_specs=[
            pl.BlockSpec((tm, (n_q + N_KV_HEADS) * HEAD_DIM), lambda i: (i, 0)),
            pl.BlockSpec((1, HEAD_DIM), lambda i: (0, 0)),
            pl.BlockSpec((1, HEAD_DIM), lambda i: (0, 0)),
            pl.BlockSpec((tm, HEAD_DIM), tab),
            pl.BlockSpec((tm, HEAD_DIM), tab),
            pl.BlockSpec((tm, HEAD_DIM), tab),
        ],
        out_specs=[pl.BlockSpec((tm, n_q * HEAD_DIM), lambda i: (i, 0)),
                   pl.BlockSpec((tm, N_KV_HEADS * HEAD_DIM), lambda i: (i, 0))],
        out_shape=[jax.ShapeDtypeStruct((rows, n_q * HEAD_DIM), BF16),
                   jax.ShapeDtypeStruct((rows, N_KV_HEADS * HEAD_DIM), BF16)],
        compiler_params=_cparams("arbitrary"),
        name="qk_prep",
    )(qkv, q_gain.reshape(1, HEAD_DIM), k_gain.reshape(1, HEAD_DIM), *tables)


def _flash_kernel(q_ref, k_ref, v_ref, kc_ref, vc_ref, o_ref, m_ref, alpha_ref, acc_ref, s_ref, p_ref, mx_ref, *,
                  n_lat_tiles, nkv, tq, tk, ctx):
    qi = pl.program_id(1)
    r = GROUP * tq
    rc = min(128, r)
    q = jnp.concatenate([q_ref[:, g * HEAD_DIM:(g + 1) * HEAD_DIM] for g in range(GROUP)], axis=0)
    nt = (((1,), (1,)), ((), ()))

    def scores(k):
        return lax.dot_general(q, k, nt, preferred_element_type=F32)

    def times_values(p, v):
        return jnp.dot(p, jnp.concatenate([v, jnp.ones_like(v)], axis=1), preferred_element_type=F32)

    def kv_rows(j):
        start = j * tk
        return pl.ds(start if isinstance(j, int) else pl.multiple_of(start, tk), tk)

    def put_scores(slot, k):
        width = k.shape[0]
        s = scores(k)
        s_ref[slot, :, 0:width] = s
        mx = s[:, 0:LANES]
        for c0 in range(LANES, width, LANES):
            mx = jnp.maximum(mx, s[:, c0:c0 + LANES])
        mx_ref[slot] = mx

    def softmax_stage(slot, width, first):
        for r0 in range(0, r, rc):
            rows = slice(r0, r0 + rc)
            row_max = jnp.max(mx_ref[slot, rows, :], axis=-1, keepdims=True)
            if first:
                m_new = jnp.broadcast_to(row_max, (rc, LANES))
            else:
                m_prev = m_ref[rows, :]
                m_new = jnp.maximum(m_prev, row_max)
                alpha_ref[rows, :] = jnp.exp2(m_prev - m_new)
            m_ref[rows, :] = m_new
            for c0 in range(0, width, LANES):
                p_ref[slot, rows, c0:c0 + LANES] = jnp.exp2(s_ref[slot, rows, c0:c0 + LANES] - m_new).astype(BF16)

    def rescale(pv):
        for r0 in range(0, r, rc):
            rows = slice(r0, r0 + rc)
            a = alpha_ref[rows, :]
            for c0 in (0, LANES):
                cols = slice(c0, c0 + LANES)
                acc = acc_ref[rows, cols]
                acc_ref[rows, cols] = (acc if pv is None else acc + pv[rows, cols]) * a

    def stage(j, slot, last):
        pv = times_values(p_ref[1 - slot], v_ref[kv_rows(j - 1), :])
        if not last:
            put_scores(1 - slot, k_ref[kv_rows(j + 1), :])
        softmax_stage(slot, tk, False)
        rescale(pv)

    put_scores(0, kc_ref[...])
    softmax_stage(0, ctx, True)
    acc_ref[...] = times_values(p_ref[0, :, 0:ctx], vc_ref[...])

    @pl.when(qi < n_lat_tiles)
    def _():
        put_scores(0, k_ref[kv_rows(0), :])
        put_scores(1, k_ref[kv_rows(1), :])
        softmax_stage(0, tk, False)
        rescale(None)

        def pair(t, carry):
            j = 2 * t + 1
            stage(j, 1, False)
            stage(j + 1, 0, False)
            return carry

        lax.fori_loop(0, (nkv - 2) // 2, pair, 0)
        stage(nkv - 1, 1, True)
        acc_ref[...] += times_values(p_ref[1], v_ref[kv_rows(nkv - 1), :])

    for g in range(GROUP):
        rows = slice(g * tq, (g + 1) * tq)
        o = acc_ref[rows, 0:HEAD_DIM] / acc_ref[rows, HEAD_DIM:2 * HEAD_DIM]
        o_ref[:, g * HEAD_DIM:(g + 1) * HEAD_DIM] = o.astype(o_ref.dtype)


def flash_attention(q, k, qkv, seq, ctx, batch, tq, tk):
    rows = q.shape[0]
    n_q_heads = GROUP * N_KV_HEADS
    per_batch_q = seq // tq
    n_lat_tiles = batch * per_batch_q
    n_ctx_tiles = batch * (ctx // tq)
    nkv = seq // tk
    assert nkv >= 2 and nkv % 2 == 0 and ctx <= tk and ctx % LANES == 0
    ctx_blk0 = (batch * seq) // ctx
    v_col0 = n_q_heads + N_KV_HEADS
    r = GROUP * tq

    def batch_of(qi):
        return jnp.where(qi < n_lat_tiles, qi // per_batch_q, (qi - n_lat_tiles) // (ctx // tq))

    grid = (N_KV_HEADS, n_lat_tiles + n_ctx_tiles)
    return pl.pallas_call(
        functools.partial(_flash_kernel, n_lat_tiles=n_lat_tiles, nkv=nkv, tq=tq, tk=tk, ctx=ctx),
        grid=grid,
        in_specs=[
            pl.BlockSpec((tq, GROUP * HEAD_DIM), lambda h, qi: (qi, h)),
            pl.BlockSpec((seq, HEAD_DIM), lambda h, qi: (batch_of(qi), h)),
            pl.BlockSpec((seq, HEAD_DIM), lambda h, qi: (batch_of(qi), v_col0 + h)),
            pl.BlockSpec((ctx, HEAD_DIM), lambda h, qi: (ctx_blk0 + batch_of(qi), h)),
            pl.BlockSpec((ctx, HEAD_DIM), lambda h, qi: (ctx_blk0 + batch_of(qi), v_col0 + h)),
        ],
        out_specs=pl.BlockSpec((tq, GROUP * HEAD_DIM), lambda h, qi: (qi, h)),
        out_shape=jax.ShapeDtypeStruct((rows, n_q_heads * HEAD_DIM), BF16),
        scratch_shapes=[pltpu.VMEM((r, LANES), F32), pltpu.VMEM((r, LANES), F32),
                        pltpu.VMEM((r, 2 * HEAD_DIM), F32),
                        pltpu.VMEM((2, r, tk), F32), pltpu.VMEM((2, r, tk), BF16),
                        pltpu.VMEM((2, r, LANES), F32)],
        compiler_params=_cparams("arbitrary", "arbitrary"),
        name="flash_attention",
    )(q, k, qkv, k, qkv)


def _scan_constants(reverse):
    t = np.arange(CHUNK)[:, None]
    u = np.arange(CHUNK)[None, :]
    tri = (u >= t) if reverse else (u <= t)
    level_of = np.where(t == u, 0, -1)
    for lvl in range(N_LEVELS):
        b = CHUNK >> (lvl + 1)
        same = (t & ~(2 * b - 1)) == (u & ~(2 * b - 1))
        t_hi, s_hi = (t & b) != 0, (u & b) != 0
        level_of = np.where(same & (~t_hi & s_hi if reverse else t_hi & ~s_hi), lvl + 1, level_of)
    return jnp.asarray(tri, BF16), jnp.asarray(level_of, jnp.int32)


def _keep_bf16_bits(x):
    bits = lax.bitcast_convert_type(x, jnp.int32) & jnp.int32(-65536)
    return lax.bitcast_convert_type(bits, F32)


def _level_refs(cb, lvl, reverse):
    b = CHUNK >> (lvl + 1)
    off = b if reverse else b - 1
    if 2 * b >= 8:
        parts = [jnp.broadcast_to(cb[base + off:base + off + 1], (2 * b, cb.shape[1]))
                 for base in range(0, CHUNK, 2 * b)]
        return parts[0] if len(parts) == 1 else jnp.concatenate(parts, axis=0)
    within = lax.broadcasted_iota(jnp.int32, cb.shape, 0) & (2 * b - 1)
    out = cb
    for r in range(2 * b):
        if r != off:
            out = jnp.where(within == r, pltpu.roll(cb, (r - off) % CHUNK, axis=0), out)
    return out


def _scan_kernel(q_ref, v_ref, f_ref, lb_ref, tri_ref, lvl_ref, o_ref, st_ref, *, reverse, n_chunks, hb):
    step = pl.program_id(2)

    @pl.when(step == 0)
    def _():
        st_ref[...] = jnp.zeros(st_ref.shape, F32)

    tri = tri_ref[...]
    level_of = lvl_ref[...]
    pair_masks = [level_of == i for i in range(N_LEVELS + 1)]
    nt = (((1,), (1,)), ((), ()))
    tn = (((0,), (0,)), ((), ()))
    order = range(n_chunks - 1, -1, -1) if reverse else range(n_chunks)
    for c in order:
        rows = slice(c * CHUNK, (c + 1) * CHUNK)
        for hh in range(hb):
            cols = slice(hh * REC_DK, (hh + 1) * REC_DK)
            lb = lb_ref[:, cols]
            qr = q_ref[rows, cols].astype(F32)
            v = v_ref[rows, cols]
            q = (qr * (0.5 * REC_DK ** -0.5)) * (1.0 + jnp.tanh(0.5 * qr))
            fg = (0.5 + 0.5 * lb) + (0.5 - 0.5 * lb) * jnp.tanh(0.5 * f_ref[rows, cols].astype(F32))
            k = 1.0 - fg
            lf = jnp.log(fg)
            hi = _keep_bf16_bits(lf)
            rest = lf - hi
            mid = _keep_bf16_bits(rest)
            lo = rest - mid
            sums = jnp.dot(tri, jnp.concatenate([hi.astype(BF16), mid.astype(BF16), lo.astype(BF16)], axis=1),
                           preferred_element_type=F32)
            cb = sums[:, 0:REC_DK] + sums[:, REC_DK:2 * REC_DK] + sums[:, 2 * REC_DK:3 * REC_DK]
            total = cb[0:1] if reverse else cb[CHUNK - 1:CHUNK]
            st = st_ref[hh]

            o = lax.dot_general((q * jnp.exp(cb)).astype(BF16), st.astype(BF16), nt, preferred_element_type=F32)
            a = jnp.where(pair_masks[0],
                          lax.dot_general(q.astype(BF16), k.astype(BF16), nt, preferred_element_type=F32), 0.0)
            for lvl in range(N_LEVELS):
                ref = _level_refs(cb, lvl, reverse)
                ql = (q * jnp.exp(jnp.minimum(cb - ref, 0.0))).astype(BF16)
                kl = (k * jnp.exp(jnp.minimum(ref - cb, 0.0))).astype(BF16)
                a = jnp.where(pair_masks[lvl + 1], lax.dot_general(ql, kl, nt, preferred_element_type=F32), a)
            o = o + jnp.dot(a.astype(BF16), v, preferred_element_type=F32)
            o_ref[rows, cols] = o.astype(o_ref.dtype)

            kd = (k * jnp.exp(total - cb)).astype(BF16)
            st_ref[hh] = st * jnp.exp(total) + lax.dot_general(v, kd, tn, preferred_element_type=F32)


def hgrn_scan(proj, lb_dir, f_sec, reverse, seq, ctx, batch, n_heads, hb=4):
    rows = proj.shape[0]
    tb = ctx
    per_batch = seq // tb
    ctx_blk0 = (batch * seq) // tb
    groups = n_heads // hb
    width = hb * REC_DK
    tri, level_of = _scan_constants(reverse)

    def row_blk(b, s):
        lat = b * per_batch + (per_batch - s if reverse else s - 1)
        return jnp.where(s == 0, ctx_blk0 + b, lat)

    return pl.pallas_call(
        functools.partial(_scan_kernel, reverse=reverse, n_chunks=tb // CHUNK, hb=hb),
        grid=(batch, groups, 1 + per_batch),
        in_specs=[
            pl.BlockSpec((tb, width), lambda b, h, s: (row_blk(b, s), h)),
            pl.BlockSpec((tb, width), lambda b, h, s: (row_blk(b, s), groups + h)),
            pl.BlockSpec((tb, width), lambda b, h, s: (row_blk(b, s), f_sec * groups + h)),
            pl.BlockSpec((None, 1, width), lambda b, h, s: (h, 0, 0)),
            pl.BlockSpec(tri.shape, lambda b, h, s: (0, 0)),
            pl.BlockSpec(level_of.shape, lambda b, h, s: (0, 0)),
        ],
        out_specs=pl.BlockSpec((tb, width), lambda b, h, s: (row_blk(b, s), h)),
        out_shape=jax.ShapeDtypeStruct((rows, n_heads * REC_DK), BF16),
        scratch_shapes=[pltpu.VMEM((hb, REC_DK, REC_DK), F32)],
        compiler_params=_cparams("arbitrary", "arbitrary", "arbitrary"),
        name="hgrn_scan_bw" if reverse else "hgrn_scan_fw",
    )(proj, proj, proj, lb_dir.reshape(groups, 1, width), tri, level_of)


def _readout_kernel(of_ref, ob_ref, g_ref, gain_ref, o_ref, *, n_heads):
    gain = gain_ref[...]
    for hh in range(n_heads):
        sl = slice(hh * REC_DK, (hh + 1) * REC_DK)
        o = of_ref[:, sl].astype(F32) + ob_ref[:, sl].astype(F32)
        ms = jnp.mean(o * o, axis=-1, keepdims=True)
        g = g_ref[:, sl].astype(F32)
        o_ref[:, sl] = (o * lax.rsqrt(ms + EPS) * gain * (g * jax.nn.sigmoid(g))).astype(o_ref.dtype)


def hgrn_readout(o_fw, o_bw, proj, o_gain, gate_sec, n_heads):
    rows, d = o_fw.shape
    tm = ROW_TILE
    return pl.pallas_call(
        functools.partial(_readout_kernel, n_heads=n_heads),
        grid=(rows // tm,),
        in_specs=[pl.BlockSpec((tm, d), lambda i: (i, 0)),
                  pl.BlockSpec((tm, d), lambda i: (i, 0)),
                  pl.BlockSpec((tm, d), lambda i: (i, gate_sec)),
                  pl.BlockSpec((1, REC_DK), lambda i: (0, 0))],
        out_specs=pl.BlockSpec((tm, d), lambda i: (i, 0)),
        out_shape=jax.ShapeDtypeStruct((rows, d), BF16),
        compiler_params=_cparams("arbitrary"),
        name="hgrn_readout",
    )(o_fw, o_bw, proj, o_gain.reshape(1, REC_DK))


def _router_kernel(y_ref, w_ref, o_ref):
    logits = jnp.dot(y_ref[...], w_ref[...], preferred_element_type=F32, precision=lax.Precision.HIGHEST)
    lane = lax.broadcasted_iota(jnp.int32, logits.shape, 1)
    neg = -jnp.inf
    logits = jnp.where(lane < N_EXPERTS, logits, neg)
    m1 = jnp.max(logits, axis=-1, keepdims=True)
    i1 = jnp.min(jnp.where(logits == m1, lane, LANES), axis=-1, keepdims=True)
    rest = jnp.where(lane == i1, neg, logits)
    m2 = jnp.max(rest, axis=-1, keepdims=True)
    i2 = jnp.min(jnp.where(rest == m2, lane, LANES), axis=-1, keepdims=True)
    e = jnp.exp(m2 - m1)
    w1 = 1.0 / (1.0 + e)
    w2 = e / (1.0 + e)
    out = jnp.where(lane == 0, i1.astype(F32),
                    jnp.where(lane == 1, i2.astype(F32),
                              jnp.where(lane == 2, w1, jnp.where(lane == 3, w2, 0.0))))
    o_ref[...] = out


def moe_router(y, w_router):
    rows, d = y.shape
    tm = ROW_TILE
    w_pad = jnp.zeros((d, LANES), F32).at[:, :N_EXPERTS].set(w_router)
    return pl.pallas_call(
        _router_kernel,
        grid=(rows // tm,),
        in_specs=[pl.BlockSpec((tm, d), lambda i: (i, 0)), pl.BlockSpec((d, LANES), lambda i: (0, 0))],
        out_specs=pl.BlockSpec((tm, LANES), lambda i: (i, 0)),
        out_shape=jax.ShapeDtypeStruct((rows, LANES), F32),
        compiler_params=_cparams("arbitrary"),
        name="moe_router",
    )(y, w_pad)


def _row_copy(src_ref, dst_ref, sem, src_row, dst_row):
    return pltpu.make_async_copy(src_ref.at[pl.ds(src_row, 1), :], dst_ref.at[pl.ds(dst_row, 1), :], sem)


def _gather_rows_into(idx_ref, src_ref, dst_ref, sem):
    n = dst_ref.shape[0]

    def start(r, carry):
        _row_copy(src_ref, dst_ref, sem, idx_ref[0, r], r).start()
        return carry

    def wait(r, carry):
        _row_copy(src_ref, dst_ref, sem, 0, r).wait()
        return carry

    lax.fori_loop(0, n, start, 0)
    lax.fori_loop(0, n, wait, 0)


def _gather_kernel(idx_ref, src_ref, o_ref, sem):
    _gather_rows_into(idx_ref, src_ref, o_ref, sem)


def gather_rows(src, idx, tile):
    n = idx.shape[0]
    d = src.shape[1]
    return pl.pallas_call(
        _gather_kernel,
        grid=(n // tile,),
        in_specs=[pl.BlockSpec((None, 1, tile), lambda i: (i, 0, 0), memory_space=pltpu.SMEM),
                  pl.BlockSpec(memory_space=pl.ANY)],
        out_specs=pl.BlockSpec((tile, d), lambda i: (i, 0)),
        out_shape=jax.ShapeDtypeStruct((n, d), src.dtype),
        scratch_shapes=[pltpu.SemaphoreType.DMA(())],
        compiler_params=_cparams("arbitrary"),
        name="moe_gather_rows",
    )(idx.reshape(n // tile, 1, tile), src)


def _expert_up_kernel(te_ref, ta_ref, x_ref, w1_ref, w3_ref, o_ref):
    i = pl.program_id(0)

    @pl.when(ta_ref[i] > 0)
    def _():
        x = x_ref[...].astype(BF16)
        u = jnp.dot(x, w1_ref[...], preferred_element_type=F32)
        g = jnp.dot(x, w3_ref[...], preferred_element_type=F32)
        o_ref[...] = (u * jax.nn.sigmoid(u) * g).astype(o_ref.dtype)

    @pl.when(ta_ref[i] == 0)
    def _():
        o_ref[...] = jnp.zeros(o_ref.shape, o_ref.dtype)


def expert_up(x, w1, w3, tile_expert, tile_active, tn=1024):
    p, d = x.shape
    f = w1.shape[2]
    tm = ROW_TILE
    return pl.pallas_call(
        _expert_up_kernel,
        grid_spec=pltpu.PrefetchScalarGridSpec(
            num_scalar_prefetch=2,
            grid=(p // tm, f // tn),
            in_specs=[pl.BlockSpec((tm, d), lambda i, j, te, ta: (i, 0)),
                      pl.BlockSpec((None, d, tn), lambda i, j, te, ta: (te[i], 0, j)),
                      pl.BlockSpec((None, d, tn), lambda i, j, te, ta: (te[i], 0, j))],
            out_specs=pl.BlockSpec((tm, tn), lambda i, j, te, ta: (i, j)),
        ),
        out_shape=jax.ShapeDtypeStruct((p, f), BF16),
        compiler_params=_cparams("arbitrary", "arbitrary"),
        name="moe_expert_up",
    )(tile_expert, tile_active, x, w1, w3)


def _expert_down_kernel(te_ref, ta_ref, a_ref, w_ref, rw_ref, o_ref, acc_ref, *, nk):
    i = pl.program_id(0)
    kk = pl.program_id(1)

    @pl.when(ta_ref[i] > 0)
    def _():
        part = jnp.dot(a_ref[...], w_ref[...], preferred_element_type=F32)

        @pl.when(kk == 0)
        def _():
            acc_ref[...] = part

        @pl.when(kk > 0)
        def _():
            acc_ref[...] += part

        @pl.when(kk == nk - 1)
        def _():
            o_ref[...] = acc_ref[...] * rw_ref[...]

    @pl.when(jnp.logical_and(ta_ref[i] == 0, kk == nk - 1))
    def _():
        o_ref[...] = jnp.zeros(o_ref.shape, o_ref.dtype)


def expert_down(a, w2, row_weight, tile_expert, tile_active, tk=2048):
    p, f = a.shape
    d = w2.shape[2]
    tm = ROW_TILE
    nk = f // tk
    return pl.pallas_call(
        functools.partial(_expert_down_kernel, nk=nk),
        grid_spec=pltpu.PrefetchScalarGridSpec(
            num_scalar_prefetch=2,
            grid=(p // tm, nk),
            in_specs=[pl.BlockSpec((tm, tk), lambda i, kk, te, ta: (i, kk)),
                      pl.BlockSpec((None, tk, d), lambda i, kk, te, ta: (te[i], kk, 0)),
                      pl.BlockSpec((tm, 1), lambda i, kk, te, ta: (i, 0))],
            out_specs=pl.BlockSpec((tm, d), lambda i, kk, te, ta: (i, 0)),
            scratch_shapes=[pltpu.VMEM((tm, d), F32)],
        ),
        out_shape=jax.ShapeDtypeStruct((p, d), F32),
        compiler_params=_cparams("arbitrary", "arbitrary"),
        name="moe_expert_down",
    )(tile_expert, tile_active, a, w2, row_weight)


def _combine_kernel(s0_ref, s1_ref, src_ref, h_ref, gate_ref, gain_ref, o_ref, buf0, buf1, sem):
    _gather_rows_into(s0_ref, src_ref, buf0, sem.at[0])
    _gather_rows_into(s1_ref, src_ref, buf1, sem.at[1])
    o_ref[...] = _residual_epilogue(buf0[...] + buf1[...], h_ref[...], gate_ref[...], gain_ref[...])


def moe_combine_resnorm(expert_out, slot0, slot1, h, gain, mods, gate_sec, seq, batch, tile=256):
    rows, d = h.shape
    row = functools.partial(_mod_row, tm=tile, seq=seq, batch=batch)
    idx_spec = pl.BlockSpec((None, 1, tile), lambda i: (i, 0, 0), memory_space=pltpu.SMEM)
    return pl.pallas_call(
        _combine_kernel,
        grid=(rows // tile,),
        in_specs=[idx_spec, idx_spec,
                  pl.BlockSpec(memory_space=pl.ANY),
                  pl.BlockSpec((tile, d), lambda i: (i, 0)),
                  pl.BlockSpec((None, 1, d), lambda i: (row(i), 0, gate_sec)),
                  pl.BlockSpec((1, d), lambda i: (0, 0))],
        out_specs=pl.BlockSpec((tile, d), lambda i: (i, 0)),
        out_shape=jax.ShapeDtypeStruct((rows, d), F32),
        scratch_shapes=[pltpu.VMEM((tile, d), F32), pltpu.VMEM((tile, d), F32), pltpu.SemaphoreType.DMA((2,))],
        compiler_params=_cparams("arbitrary"),
        name="moe_combine_resnorm",
    )(slot0.reshape(rows // tile, 1, tile), slot1.reshape(rows // tile, 1, tile), expert_out, h, mods,
      gain.reshape(1, d))


def _routing_plan(route, tm):
    rows = route.shape[0]
    experts = route[:, :TOP_K].astype(jnp.int32).reshape(-1)
    weights = route[:, TOP_K:2 * TOP_K].reshape(-1)
    onehot = (experts[:, None] == jnp.arange(N_EXPERTS)[None, :]).astype(jnp.int32)
    rank = jnp.take_along_axis(jnp.cumsum(onehot, axis=0) - onehot, experts[:, None], axis=1)[:, 0]
    counts = jnp.sum(onehot, axis=0)
    tiles_per = (counts + tm - 1) // tm
    tile_end = jnp.cumsum(tiles_per)
    start = (tile_end - tiles_per) * tm
    slot = start[experts] + rank
    n_slots = rows * TOP_K + N_EXPERTS * tm
    n_tiles = n_slots // tm
    token_of_slot = jnp.zeros((n_slots,), jnp.int32).at[slot].set(jnp.arange(rows * TOP_K, dtype=jnp.int32) // TOP_K)
    weight_of_slot = jnp.zeros((n_slots,), F32).at[slot].set(weights)
    tile_ids = jnp.arange(n_tiles, dtype=jnp.int32)
    tile_expert = jnp.minimum(jnp.searchsorted(tile_end, tile_ids, side="right"), N_EXPERTS - 1).astype(jnp.int32)
    tile_active = (tile_ids < tile_end[-1]).astype(jnp.int32)
    slots = slot.reshape(rows, TOP_K).astype(jnp.int32)
    return token_of_slot, weight_of_slot.reshape(n_slots, 1), tile_expert, tile_active, slots[:, 0], slots[:, 1]


def moe_ffn_resnorm(y, h, w_router, w1, w3, w2, gain, mods, gate_sec, seq, batch):
    route = moe_router(y, w_router)
    token_of_slot, weight_of_slot, tile_expert, tile_active, slot0, slot1 = _routing_plan(route, ROW_TILE)
    x = gather_rows(y, token_of_slot, tile=256)
    hid = expert_up(x, w1, w3, tile_expert, tile_active)
    out = expert_down(hid, w2, weight_of_slot, tile_expert, tile_active)
    return moe_combine_resnorm(out, slot0, slot1, h, gain, mods, gate_sec, seq, batch)


def _lower_bounds(logits):
    p = jax.nn.softmax(logits.astype(F32), axis=0)
    return jnp.cumsum(p, axis=0) - p[0]


def kernel(x, c, ctx, c_ctx, w_mod, b_mod, norm_gains, attn_w_in, attn_w_o, attn_q_gain, attn_k_gain,
           rec_w_in, rec_w_o, rec_lb_logits, rec_o_gain, ffn_w1, ffn_w3, ffn_w2,
           moe_w_router, moe_w1, moe_w3, moe_w2):
    batch, seq, d = x.shape
    ctx_len = ctx.shape[1]
    depth = w_mod.shape[0]
    n_lat = batch * seq
    n_heads = d // REC_DK
    assert seq % ROW_TILE == 0 and (batch * ctx_len) % ROW_TILE == 0 and batch + 1 <= MOD_ROWS

    hg = jnp.concatenate([x.reshape(n_lat, d), ctx.reshape(batch * ctx_len, d)], axis=0)
    cond = jnp.zeros((MOD_ROWS, d), F32).at[:batch].set(c).at[batch].set(c_ctx)
    mods_all = modulation_table(cond, w_mod, b_mod)
    lower = _lower_bounds(rec_lb_logits)
    tables = rope_tables(seq, ROW_TILE)
    tq = min(256, ctx_len)
    tk = min(1024, seq // 4)

    for layer in range(depth):
        j = layer // 2
        ng = norm_gains[layer]
        mods = mods_all[layer]
        y = norm_modulate(hg, ng[0], mods, 0, 1, seq, batch, BF16)
        if layer % 2 == 0:
            qkv = matmul(y, attn_w_in[j].astype(BF16), BF16, tn=1024)
            q, k = qk_prep(qkv, attn_q_gain[j], attn_k_gain[j], tables, seq, n_lat)
            mix = flash_attention(q, k, qkv, seq, ctx_len, batch, tq, tk)
            hg = matmul_resnorm(mix, attn_w_o[j].astype(BF16), hg, ng[1], mods, 2, seq, batch, tk=d)
        else:
            proj = matmul(y, rec_w_in[j].astype(BF16), BF16, tn=1024)
            o_fw = hgrn_scan(proj, lower[layer, 0], 2, False, seq, ctx_len, batch, n_heads)
            o_bw = hgrn_scan(proj, lower[layer, 1], 3, True, seq, ctx_len, batch, n_heads)
            mix = hgrn_readout(o_fw, o_bw, proj, rec_o_gain[j], 4, n_heads)
            hg = matmul_resnorm(mix, rec_w_o[j].astype(BF16), hg, ng[1], mods, 2, seq, batch, tk=d)
        if layer % 2 == 0:
            y = norm_modulate(hg, ng[2], mods, 3, 4, seq, batch, BF16)
            hid = swiglu_up(y, ffn_w1[j].astype(BF16), ffn_w3[j].astype(BF16))
            f = hid.shape[1]
            hg = matmul_resnorm(hid, ffn_w2[j].astype(BF16), hg, ng[3], mods, 5, seq, batch, tk=f // 4)
        else:
            y = norm_modulate(hg, ng[2], mods, 3, 4, seq, batch, F32)
            hg = moe_ffn_resnorm(y, hg, moe_w_router[j], moe_w1[j].astype(BF16), moe_w3[j].astype(BF16),
                                 moe_w2[j].astype(BF16), ng[3], mods, 5, seq, batch)
    return hg[:n_lat].reshape(batch, seq, d)
```

```python
import functools

import numpy as np
import jax
import jax.numpy as jnp
from jax import lax
from jax.experimental import pallas as pl
from jax.experimental.pallas import tpu as pltpu

F32 = jnp.float32
BF16 = jnp.bfloat16

HEAD_DIM = 128
N_KV_HEADS = 4
GROUP = 4
GRID_W = 64
ROPE_THETA = 10000.0
AXIS_ROT_DIM = HEAD_DIM // 2
REC_DK = 128
N_EXPERTS = 8
TOP_K = 2
N_MOD = 6
EPS = 1e-6
CHUNK = 64
N_LEVELS = 6

LANES = 128
VMEM_LIMIT = 56 * 2**20

ROW_TILE = 512
MOD_ROWS = 8


def _cparams(*sem):
    return pltpu.CompilerParams(dimension_semantics=sem, vmem_limit_bytes=VMEM_LIMIT)


def _mod_row(i, tm, seq, batch):
    return jnp.minimum((i * tm) // seq, batch)


def _mod_kernel(c_ref, w_ref, b_ref, o_ref):
    c = c_ref[...]
    s = (c * jax.nn.sigmoid(c)).astype(BF16)
    o_ref[...] = jnp.dot(s, w_ref[...].astype(BF16), preferred_element_type=F32) + b_ref[...]


def modulation_table(cond, w_mod, b_mod, tn=1024):
    depth, d, n = w_mod.shape
    out = pl.pallas_call(
        _mod_kernel,
        grid=(depth, n // tn),
        in_specs=[
            pl.BlockSpec((MOD_ROWS, d), lambda l, j: (0, 0)),
            pl.BlockSpec((None, d, tn), lambda l, j: (l, 0, j)),
            pl.BlockSpec((None, 1, tn), lambda l, j: (l, 0, j)),
        ],
        out_specs=pl.BlockSpec((None, MOD_ROWS, tn), lambda l, j: (l, 0, j)),
        out_shape=jax.ShapeDtypeStruct((depth, MOD_ROWS, n), F32),
        compiler_params=_cparams("arbitrary", "arbitrary"),
        name="modulation_table",
    )(cond, w_mod, b_mod.reshape(depth, 1, n))
    return out.reshape(depth, MOD_ROWS, 1, n)


def _norm_mod_kernel(h_ref, g_ref, sh_ref, sc_ref, o_ref):
    x = h_ref[...]
    ms = jnp.mean(x * x, axis=-1, keepdims=True)
    y = x * lax.rsqrt(ms + EPS) * g_ref[...]
    o_ref[...] = (y * (1.0 + sc_ref[...]) + sh_ref[...]).astype(o_ref.dtype)


def norm_modulate(h, gain, mods, shift_sec, scale_sec, seq, batch, out_dtype):
    rows, d = h.shape
    tm = ROW_TILE
    row = functools.partial(_mod_row, tm=tm, seq=seq, batch=batch)
    return pl.pallas_call(
        _norm_mod_kernel,
        grid=(rows // tm,),
        in_specs=[
            pl.BlockSpec((tm, d), lambda i: (i, 0)),
            pl.BlockSpec((1, d), lambda i: (0, 0)),
            pl.BlockSpec((None, 1, d), lambda i: (row(i), 0, shift_sec)),
            pl.BlockSpec((None, 1, d), lambda i: (row(i), 0, scale_sec)),
        ],
        out_specs=pl.BlockSpec((tm, d), lambda i: (i, 0)),
        out_shape=jax.ShapeDtypeStruct((rows, d), out_dtype),
        compiler_params=_cparams("arbitrary"),
        name="norm_modulate",
    )(h, gain.reshape(1, d), mods, mods)


def _mm_kernel(a_ref, b_ref, o_ref):
    o_ref[...] = jnp.dot(a_ref[...], b_ref[...], preferred_element_type=F32).astype(o_ref.dtype)


def matmul(a, b, out_dtype, tn):
    m, k = a.shape
    n = b.shape[1]
    tm = ROW_TILE
    return pl.pallas_call(
        _mm_kernel,
        grid=(m // tm, n // tn),
        in_specs=[pl.BlockSpec((tm, k), lambda i, j: (i, 0)),
                  pl.BlockSpec((k, tn), lambda i, j: (0, j))],
        out_specs=pl.BlockSpec((tm, tn), lambda i, j: (i, j)),
        out_shape=jax.ShapeDtypeStruct((m, n), out_dtype),
        compiler_params=_cparams("arbitrary", "arbitrary"),
        name="matmul",
    )(a, b)


def _swiglu_up_kernel(a_ref, w1_ref, w3_ref, o_ref):
    a = a_ref[...]
    u = jnp.dot(a, w1_ref[...], preferred_element_type=F32)
    g = jnp.dot(a, w3_ref[...], preferred_element_type=F32)
    o_ref[...] = (u * jax.nn.sigmoid(u) * g).astype(o_ref.dtype)


def swiglu_up(a, w1, w3, tn=512):
    m, k = a.shape
    n = w1.shape[1]
    tm = ROW_TILE
    return pl.pallas_call(
        _swiglu_up_kernel,
        grid=(m // tm, n // tn),
        in_specs=[pl.BlockSpec((tm, k), lambda i, j: (i, 0)),
                  pl.BlockSpec((k, tn), lambda i, j: (0, j)),
                  pl.BlockSpec((k, tn), lambda i, j: (0, j))],
        out_specs=pl.BlockSpec((tm, tn), lambda i, j: (i, j)),
        out_shape=jax.ShapeDtypeStruct((m, n), BF16),
        compiler_params=_cparams("arbitrary", "arbitrary"),
        name="swiglu_up",
    )(a, w1, w3)


def _residual_epilogue(r, h, gate, gain):
    ms = jnp.mean(r * r, axis=-1, keepdims=True)
    return h + gate * (r * lax.rsqrt(ms + EPS) * gain)


def _mm_resnorm_kernel(a_ref, b_ref, h_ref, gate_ref, gain_ref, o_ref, acc_ref, *, nk):
    kk = pl.program_id(1)
    part = jnp.dot(a_ref[...], b_ref[...], preferred_element_type=F32)

    if nk == 1:
        o_ref[...] = _residual_epilogue(part, h_ref[...], gate_ref[...], gain_ref[...])
        return

    @pl.when(kk == 0)
    def _():
        acc_ref[...] = part

    @pl.when(kk > 0)
    def _():
        acc_ref[...] += part

    @pl.when(kk == nk - 1)
    def _():
        o_ref[...] = _residual_epilogue(acc_ref[...], h_ref[...], gate_ref[...], gain_ref[...])


def matmul_resnorm(a, b, h, gain, mods, gate_sec, seq, batch, tk):
    m, k = a.shape
    d = b.shape[1]
    tm = ROW_TILE
    nk = k // tk
    row = functools.partial(_mod_row, tm=tm, seq=seq, batch=batch)
    return pl.pallas_call(
        functools.partial(_mm_resnorm_kernel, nk=nk),
        grid=(m // tm, nk),
        in_specs=[
            pl.BlockSpec((tm, tk), lambda i, kk: (i, kk)),
            pl.BlockSpec((tk, d), lambda i, kk: (kk, 0)),
            pl.BlockSpec((tm, d), lambda i, kk: (i, 0)),
            pl.BlockSpec((None, 1, d), lambda i, kk: (row(i), 0, gate_sec)),
            pl.BlockSpec((1, d), lambda i, kk: (0, 0)),
        ],
        out_specs=pl.BlockSpec((tm, d), lambda i, kk: (i, 0)),
        out_shape=jax.ShapeDtypeStruct((m, d), F32),
        scratch_shapes=[pltpu.VMEM((tm, d), F32)],
        compiler_params=_cparams("arbitrary", "arbitrary"),
        name="matmul_resnorm",
    )(a, b, h, mods, gain.reshape(1, d))


def rope_tables(seq, tile):
    rows = seq // GRID_W
    row = jnp.repeat(jnp.arange(rows, dtype=F32), GRID_W)
    col = jnp.tile(jnp.arange(GRID_W, dtype=F32), rows)
    inv_freq = ROPE_THETA ** (-jnp.arange(0, AXIS_ROT_DIM, 2, dtype=F32) / AXIS_ROT_DIM)
    ar, ac = row[:, None] * inv_freq, col[:, None] * inv_freq
    zeros = jnp.zeros_like(ar)
    cos = jnp.concatenate([jnp.cos(ar), jnp.cos(ar), jnp.cos(ac), jnp.cos(ac)], axis=-1)
    sin_lo = jnp.concatenate([-jnp.sin(ar), zeros, -jnp.sin(ac), zeros], axis=-1)
    sin_hi = jnp.concatenate([zeros, jnp.sin(ar), zeros, jnp.sin(ac)], axis=-1)
    pad0 = jnp.zeros((tile, HEAD_DIM), F32)
    return (jnp.concatenate([cos, jnp.ones((tile, HEAD_DIM), F32)], axis=0),
            jnp.concatenate([sin_lo, pad0], axis=0),
            jnp.concatenate([sin_hi, pad0], axis=0))


def _qk_prep_kernel(qkv_ref, qg_ref, kg_ref, cos_ref, slo_ref, shi_ref, q_ref, k_ref, v_ref, *, n_q, n_k, q_scale):
    cos, slo, shi = cos_ref[...], slo_ref[...], shi_ref[...]
    quarter = AXIS_ROT_DIM // 2

    def head(x, gain):
        x = x.astype(F32)
        ms = jnp.mean(x * x, axis=-1, keepdims=True)
        xn = x * lax.rsqrt(ms + EPS) * gain
        return (xn * cos + pltpu.roll(xn, HEAD_DIM - quarter, axis=1) * slo
                + pltpu.roll(xn, quarter, axis=1) * shi)

    for hh in range(n_q):
        sl = slice(hh * HEAD_DIM, (hh + 1) * HEAD_DIM)
        q_ref[:, sl] = (head(qkv_ref[:, sl], qg_ref[...]) * q_scale).astype(q_ref.dtype)
    for hh in range(n_k):
        src = slice((n_q + hh) * HEAD_DIM, (n_q + hh + 1) * HEAD_DIM)
        k_ref[:, hh * HEAD_DIM:(hh + 1) * HEAD_DIM] = head(qkv_ref[:, src], kg_ref[...]).astype(k_ref.dtype)
    for hh in range(n_k):
        src = slice((n_q + n_k + hh) * HEAD_DIM, (n_q + n_k + hh + 1) * HEAD_DIM)
        v_ref[:, 2 * hh * HEAD_DIM:(2 * hh + 1) * HEAD_DIM] = qkv_ref[:, src]
        v_ref[:, (2 * hh + 1) * HEAD_DIM:(2 * hh + 2) * HEAD_DIM] = jnp.ones((v_ref.shape[0], HEAD_DIM), v_ref.dtype)


def qk_prep(qkv, q_gain, k_gain, tables, seq, n_lat_rows):
    rows = qkv.shape[0]
    tm = ROW_TILE
    n_q = 4 * N_KV_HEADS
    per_batch = seq // tm
    n_lat_tiles = n_lat_rows // tm
    tab = lambda i: (jnp.where(i < n_lat_tiles, i % per_batch, per_batch), 0)
    q_scale = HEAD_DIM ** -0.5 * float(np.log2(np.e))
    return pl.pallas_call(
        functools.partial(_qk_prep_kernel, n_q=n_q, n_k=N_KV_HEADS, q_scale=q_scale),
        grid=(rows // tm,),
        in_specs=[
            pl.BlockSpec((tm, (n_q + 2 * N_KV_HEADS) * HEAD_DIM), lambda i: (i, 0)),
            pl.BlockSpec((1, HEAD_DIM), lambda i: (0, 0)),
            pl.BlockSpec((1, HEAD_DIM), lambda i: (0, 0)),
            pl.BlockSpec((tm, HEAD_DIM), tab),
            pl.BlockSpec((tm, HEAD_DIM), tab),
            pl.BlockSpec((tm, HEAD_DIM), tab),
        ],
        out_specs=[pl.BlockSpec((tm, n_q * HEAD_DIM), lambda i: (i, 0)),
                   pl.BlockSpec((tm, N_KV_HEADS * HEAD_DIM), lambda i: (i, 0)),
                   pl.BlockSpec((tm, 2 * N_KV_HEADS * HEAD_DIM), lambda i: (i, 0))],
        out_shape=[jax.ShapeDtypeStruct((rows, n_q * HEAD_DIM), BF16),
                   jax.ShapeDtypeStruct((rows, N_KV_HEADS * HEAD_DIM), BF16),
                   jax.ShapeDtypeStruct((rows, 2 * N_KV_HEADS * HEAD_DIM), BF16)],
        compiler_params=_cparams("arbitrary"),
        name="qk_prep",
    )(qkv, q_gain.reshape(1, HEAD_DIM), k_gain.reshape(1, HEAD_DIM), *tables)


def _flash_kernel(q_ref, k_ref, v_ref, kc_ref, vc_ref, o_ref, m_ref, acc_ref,
                  s0_ref, s1_ref, p0_ref, p1_ref, mx0_ref, mx1_ref, alpha0_ref, alpha1_ref, *,
                  n_lat_tiles, nkv, tq, tk, ctx):
    qi = pl.program_id(1)
    r = GROUP * tq
    rc = min(128, r)
    s_ref, p_ref = (s0_ref, s1_ref), (p0_ref, p1_ref)
    mx_ref, alpha_ref = (mx0_ref, mx1_ref), (alpha0_ref, alpha1_ref)
    q = jnp.concatenate([q_ref[:, g * HEAD_DIM:(g + 1) * HEAD_DIM] for g in range(GROUP)], axis=0)
    nt = (((1,), (1,)), ((), ()))

    def scores(k):
        return lax.dot_general(q, k, nt, preferred_element_type=F32)

    def times_values(p, v):
        return jnp.dot(p, v, preferred_element_type=F32)

    def kv_rows(j):
        start = j * tk
        return pl.ds(start if isinstance(j, int) else pl.multiple_of(start, tk), tk)

    def put_scores(slot, k):
        width = k.shape[0]
        s = scores(k)
        s_ref[slot][:, 0:width] = s
        mx = s[:, 0:LANES]
        for c0 in range(LANES, width, LANES):
            mx = jnp.maximum(mx, s[:, c0:c0 + LANES])
        mx_ref[slot][...] = mx

    def softmax_stage(slot, width, first):
        for r0 in range(0, r, rc):
            rows = slice(r0, r0 + rc)
            row_max = jnp.max(mx_ref[slot][rows, :], axis=-1, keepdims=True)
            if first:
                m_new = jnp.broadcast_to(row_max, (rc, LANES))
            else:
                m_prev = m_ref[rows, :]
                m_new = jnp.maximum(m_prev, row_max)
                alpha_ref[slot][rows, :] = jnp.exp2(m_prev - m_new)
            m_ref[rows, :] = m_new
            for c0 in range(0, width, LANES):
                cols = slice(c0, c0 + LANES)
                p_ref[slot][rows, cols] = jnp.exp2(s_ref[slot][rows, cols] - m_new).astype(BF16)

    def accumulate(slot, v):
        pv = times_values(p_ref[slot][...], v)
        for r0 in range(0, r, rc):
            rows = slice(r0, r0 + rc)
            a = alpha_ref[slot][rows, :]
            for c0 in (0, LANES):
                cols = slice(c0, c0 + LANES)
                acc_ref[rows, cols] = a * acc_ref[rows, cols] + pv[rows, cols]

    def stage(j, slot, last):
        if not last:
            put_scores(1 - slot, k_ref[kv_rows(j + 1), :])
        softmax_stage(slot, tk, False)
        accumulate(slot, v_ref[kv_rows(j), :])

    put_scores(0, kc_ref[...])
    softmax_stage(0, ctx, True)
    acc_ref[...] = times_values(p_ref[0][:, 0:ctx], vc_ref[...])

    @pl.when(qi < n_lat_tiles)
    def _():
        put_scores(0, k_ref[kv_rows(0), :])

        def pair(t, carry):
            stage(2 * t, 0, False)
            stage(2 * t + 1, 1, False)
            return carry

        lax.fori_loop(0, (nkv - 2) // 2, pair, 0)
        stage(nkv - 2, 0, False)
        stage(nkv - 1, 1, True)

    for g in range(GROUP):
        rows = slice(g * tq, (g + 1) * tq)
        o = acc_ref[rows, 0:HEAD_DIM] / acc_ref[rows, HEAD_DIM:2 * HEAD_DIM]
        o_ref[:, g * HEAD_DIM:(g + 1) * HEAD_DIM] = o.astype(o_ref.dtype)


def flash_attention(q, k, v_ext, seq, ctx, batch, tq, tk):
    rows = q.shape[0]
    n_q_heads = GROUP * N_KV_HEADS
    per_batch_q = seq // tq
    n_lat_tiles = batch * per_batch_q
    n_ctx_tiles = batch * (ctx // tq)
    nkv = seq // tk
    assert nkv >= 2 and nkv % 2 == 0 and ctx <= tk and ctx % LANES == 0
    ctx_blk0 = (batch * seq) // ctx
    r = GROUP * tq

    def batch_of(qi):
        return jnp.where(qi < n_lat_tiles, qi // per_batch_q, (qi - n_lat_tiles) // (ctx // tq))

    grid = (N_KV_HEADS, n_lat_tiles + n_ctx_tiles)
    return pl.pallas_call(
        functools.partial(_flash_kernel, n_lat_tiles=n_lat_tiles, nkv=nkv, tq=tq, tk=tk, ctx=ctx),
        grid=grid,
        in_specs=[
            pl.BlockSpec((tq, GROUP * HEAD_DIM), lambda h, qi: (qi, h)),
            pl.BlockSpec((seq, HEAD_DIM), lambda h, qi: (batch_of(qi), h)),
            pl.BlockSpec((seq, 2 * HEAD_DIM), lambda h, qi: (batch_of(qi), h)),
            pl.BlockSpec((ctx, HEAD_DIM), lambda h, qi: (ctx_blk0 + batch_of(qi), h)),
            pl.BlockSpec((ctx, 2 * HEAD_DIM), lambda h, qi: (ctx_blk0 + batch_of(qi), h)),
        ],
        out_specs=pl.BlockSpec((tq, GROUP * HEAD_DIM), lambda h, qi: (qi, h)),
        out_shape=jax.ShapeDtypeStruct((rows, n_q_heads * HEAD_DIM), BF16),
        scratch_shapes=[pltpu.VMEM((r, LANES), F32), pltpu.VMEM((r, 2 * HEAD_DIM), F32),
                        pltpu.VMEM((r, tk), F32), pltpu.VMEM((r, tk), F32),
                        pltpu.VMEM((r, tk), BF16), pltpu.VMEM((r, tk), BF16),
                        pltpu.VMEM((r, LANES), F32), pltpu.VMEM((r, LANES), F32),
                        pltpu.VMEM((r, LANES), F32), pltpu.VMEM((r, LANES), F32)],
        compiler_params=_cparams("arbitrary", "arbitrary"),
        name="flash_attention",
    )(q, k, v_ext, k, v_ext)


def _scan_constants(reverse):
    t = np.arange(CHUNK)[:, None]
    u = np.arange(CHUNK)[None, :]
    tri = (u >= t) if reverse else (u <= t)
    level_of = np.where(t == u, 0, -1)
    for lvl in range(N_LEVELS):
        b = CHUNK >> (lvl + 1)
        same = (t & ~(2 * b - 1)) == (u & ~(2 * b - 1))
        t_hi, s_hi = (t & b) != 0, (u & b) != 0
        level_of = np.where(same & (~t_hi & s_hi if reverse else t_hi & ~s_hi), lvl + 1, level_of)
    return jnp.asarray(tri, BF16), jnp.asarray(level_of, jnp.int32)


def _keep_bf16_bits(x):
    bits = lax.bitcast_convert_type(x, jnp.int32) & jnp.int32(-65536)
    return lax.bitcast_convert_type(bits, F32)


def _level_refs(cb, lvl, reverse):
    b = CHUNK >> (lvl + 1)
    off = b if reverse else b - 1
    if 2 * b >= 8:
        parts = [jnp.broadcast_to(cb[base + off:base + off + 1], (2 * b, cb.shape[1]))
                 for base in range(0, CHUNK, 2 * b)]
        return parts[0] if len(parts) == 1 else jnp.concatenate(parts, axis=0)
    within = lax.broadcasted_iota(jnp.int32, cb.shape, 0) & (2 * b - 1)
    out = cb
    for r in range(2 * b):
        if r != off:
            out = jnp.where(within == r, pltpu.roll(cb, (r - off) % CHUNK, axis=0), out)
    return out


def _scan_kernel(q_ref, v_ref, f_ref, lb_ref, tri_ref, lvl_ref, o_ref, st_ref, *, reverse, n_chunks, hb):
    step = pl.program_id(2)

    @pl.when(step == 0)
    def _():
        st_ref[...] = jnp.zeros(st_ref.shape, F32)

    tri = tri_ref[...]
    level_of = lvl_ref[...]
    pair_masks = [level_of == i for i in range(N_LEVELS + 1)]
    nt = (((1,), (1,)), ((), ()))
    tn = (((0,), (0,)), ((), ()))
    order = range(n_chunks - 1, -1, -1) if reverse else range(n_chunks)
    for c in order:
        rows = slice(c * CHUNK, (c + 1) * CHUNK)
        for hh in range(hb):
            cols = slice(hh * REC_DK, (hh + 1) * REC_DK)
            lb = lb_ref[:, cols]
            qr = q_ref[rows, cols].astype(F32)
            v = v_ref[rows, cols]
            q = (qr * (0.5 * REC_DK ** -0.5)) * (1.0 + jnp.tanh(0.5 * qr))
            fg = (0.5 + 0.5 * lb) + (0.5 - 0.5 * lb) * jnp.tanh(0.5 * f_ref[rows, cols].astype(F32))
            k = 1.0 - fg
            lf = jnp.log(fg)
            hi = _keep_bf16_bits(lf)
            rest = lf - hi
            mid = _keep_bf16_bits(rest)
            lo = rest - mid
            sums = jnp.dot(tri, jnp.concatenate([hi.astype(BF16), mid.astype(BF16), lo.astype(BF16)], axis=1),
                           preferred_element_type=F32)
            cb = sums[:, 0:REC_DK] + sums[:, REC_DK:2 * REC_DK] + sums[:, 2 * REC_DK:3 * REC_DK]
            total = cb[0:1] if reverse else cb[CHUNK - 1:CHUNK]
            st = st_ref[hh]

            o = lax.dot_general((q * jnp.exp(cb)).astype(BF16), st.astype(BF16), nt, preferred_element_type=F32)
            a = jnp.where(pair_masks[0],
                          lax.dot_general(q.astype(BF16), k.astype(BF16), nt, preferred_element_type=F32), 0.0)
            for lvl in range(N_LEVELS):
                ref = _level_refs(cb, lvl, reverse)
                ql = (q * jnp.exp(jnp.minimum(cb - ref, 0.0))).astype(BF16)
                kl = (k * jnp.exp(jnp.minimum(ref - cb, 0.0))).astype(BF16)
                a = jnp.where(pair_masks[lvl + 1], lax.dot_general(ql, kl, nt, preferred_element_type=F32), a)
            o = o + jnp.dot(a.astype(BF16), v, preferred_element_type=F32)
            o_ref[rows, cols] = o.astype(o_ref.dtype)

            kd = (k * jnp.exp(total - cb)).astype(BF16)
            st_ref[hh] = st * jnp.exp(total) + lax.dot_general(v, kd, tn, preferred_element_type=F32)


def hgrn_scan(proj, lb_dir, f_sec, reverse, seq, ctx, batch, n_heads, hb=4):
    rows = proj.shape[0]
    tb = ctx
    per_batch = seq // tb
    ctx_blk0 = (batch * seq) // tb
    groups = n_heads // hb
    width = hb * REC_DK
    tri, level_of = _scan_constants(reverse)

    def row_blk(b, s):
        lat = b * per_batch + (per_batch - s if reverse else s - 1)
        return jnp.where(s == 0, ctx_blk0 + b, lat)

    return pl.pallas_call(
        functools.partial(_scan_kernel, reverse=reverse, n_chunks=tb // CHUNK, hb=hb),
        grid=(batch, groups, 1 + per_batch),
        in_specs=[
            pl.BlockSpec((tb, width), lambda b, h, s: (row_blk(b, s), h)),
            pl.BlockSpec((tb, width), lambda b, h, s: (row_blk(b, s), groups + h)),
            pl.BlockSpec((tb, width), lambda b, h, s: (row_blk(b, s), f_sec * groups + h)),
            pl.BlockSpec((None, 1, width), lambda b, h, s: (h, 0, 0)),
            pl.BlockSpec(tri.shape, lambda b, h, s: (0, 0)),
            pl.BlockSpec(level_of.shape, lambda b, h, s: (0, 0)),
        ],
        out_specs=pl.BlockSpec((tb, width), lambda b, h, s: (row_blk(b, s), h)),
        out_shape=jax.ShapeDtypeStruct((rows, n_heads * REC_DK), BF16),
        scratch_shapes=[pltpu.VMEM((hb, REC_DK, REC_DK), F32)],
        compiler_params=_cparams("arbitrary", "arbitrary", "arbitrary"),
        name="hgrn_scan_bw" if reverse else "hgrn_scan_fw",
    )(proj, proj, proj, lb_dir.reshape(groups, 1, width), tri, level_of)


def _readout_kernel(of_ref, ob_ref, g_ref, gain_ref, o_ref, *, n_heads):
    gain = gain_ref[...]
    for hh in range(n_heads):
        sl = slice(hh * REC_DK, (hh + 1) * REC_DK)
        o = of_ref[:, sl].astype(F32) + ob_ref[:, sl].astype(F32)
        ms = jnp.mean(o * o, axis=-1, keepdims=True)
        g = g_ref[:, sl].astype(F32)
        o_ref[:, sl] = (o * lax.rsqrt(ms + EPS) * gain * (g * jax.nn.sigmoid(g))).astype(o_ref.dtype)


def hgrn_readout(o_fw, o_bw, proj, o_gain, gate_sec, n_heads):
    rows, d = o_fw.shape
    tm = ROW_TILE
    return pl.pallas_call(
        functools.partial(_readout_kernel, n_heads=n_heads),
        grid=(rows // tm,),
        in_specs=[pl.BlockSpec((tm, d), lambda i: (i, 0)),
                  pl.BlockSpec((tm, d), lambda i: (i, 0)),
                  pl.BlockSpec((tm, d), lambda i: (i, gate_sec)),
                  pl.BlockSpec((1, REC_DK), lambda i: (0, 0))],
        out_specs=pl.BlockSpec((tm, d), lambda i: (i, 0)),
        out_shape=jax.ShapeDtypeStruct((rows, d), BF16),
        compiler_params=_cparams("arbitrary"),
        name="hgrn_readout",
    )(o_fw, o_bw, proj, o_gain.reshape(1, REC_DK))


def _router_kernel(y_ref, w_ref, o_ref):
    logits = jnp.dot(y_ref[...], w_ref[...], preferred_element_type=F32, precision=lax.Precision.HIGHEST)
    lane = lax.broadcasted_iota(jnp.int32, logits.shape, 1)
    neg = -jnp.inf
    logits = jnp.where(lane < N_EXPERTS, logits, neg)
    m1 = jnp.max(logits, axis=-1, keepdims=True)
    i1 = jnp.min(jnp.where(logits == m1, lane, LANES), axis=-1, keepdims=True)
    rest = jnp.where(lane == i1, neg, logits)
    m2 = jnp.max(rest, axis=-1, keepdims=True)
    i2 = jnp.min(jnp.where(rest == m2, lane, LANES), axis=-1, keepdims=True)
    e = jnp.exp(m2 - m1)
    w1 = 1.0 / (1.0 + e)
    w2 = e / (1.0 + e)
    out = jnp.where(lane == 0, i1.astype(F32),
                    jnp.where(lane == 1, i2.astype(F32),
                              jnp.where(lane == 2, w1, jnp.where(lane == 3, w2, 0.0))))
    o_ref[...] = out


def moe_router(y, w_router):
    rows, d = y.shape
    tm = ROW_TILE
    w_pad = jnp.zeros((d, LANES), F32).at[:, :N_EXPERTS].set(w_router)
    return pl.pallas_call(
        _router_kernel,
        grid=(rows // tm,),
        in_specs=[pl.BlockSpec((tm, d), lambda i: (i, 0)), pl.BlockSpec((d, LANES), lambda i: (0, 0))],
        out_specs=pl.BlockSpec((tm, LANES), lambda i: (i, 0)),
        out_shape=jax.ShapeDtypeStruct((rows, LANES), F32),
        compiler_params=_cparams("arbitrary"),
        name="moe_router",
    )(y, w_pad)


def _row_copy(src_ref, dst_ref, sem, src_row, dst_row):
    return pltpu.make_async_copy(src_ref.at[pl.ds(src_row, 1), :], dst_ref.at[pl.ds(dst_row, 1), :], sem)


def _gather_rows_into(idx_ref, src_ref, dst_ref, sem):
    n = dst_ref.shape[0]

    def start(i, carry):
        for prio in range(2):
            r = 2 * i + prio
            _row_copy(src_ref, dst_ref, sem, idx_ref[0, r], r).start(priority=prio)
        return carry

    def wait(r, carry):
        _row_copy(src_ref, dst_ref, sem, 0, r).wait()
        return carry

    lax.fori_loop(0, n // 2, start, 0)
    lax.fori_loop(0, n, wait, 0)


def _gather_kernel(idx_ref, src_ref, o_ref, sem):
    _gather_rows_into(idx_ref, src_ref, o_ref, sem)


def gather_rows(src, idx, tile):
    n = idx.shape[0]
    d = src.shape[1]
    return pl.pallas_call(
        _gather_kernel,
        grid=(n // tile,),
        in_specs=[pl.BlockSpec((None, 1, tile), lambda i: (i, 0, 0), memory_space=pltpu.SMEM),
                  pl.BlockSpec(memory_space=pl.ANY)],
        out_specs=pl.BlockSpec((tile, d), lambda i: (i, 0)),
        out_shape=jax.ShapeDtypeStruct((n, d), src.dtype),
        scratch_shapes=[pltpu.SemaphoreType.DMA(())],
        compiler_params=_cparams("arbitrary"),
        name="moe_gather_rows",
    )(idx.reshape(n // tile, 1, tile), src)


def _expert_up_kernel(te_ref, ta_ref, x_ref, w1_ref, w3_ref, o_ref):
    i = pl.program_id(0)

    @pl.when(ta_ref[i] > 0)
    def _():
        x = x_ref[...].astype(BF16)
        u = jnp.dot(x, w1_ref[...], preferred_element_type=F32)
        g = jnp.dot(x, w3_ref[...], preferred_element_type=F32)
        o_ref[...] = (u * jax.nn.sigmoid(u) * g).astype(o_ref.dtype)

    @pl.when(ta_ref[i] == 0)
    def _():
        o_ref[...] = jnp.zeros(o_ref.shape, o_ref.dtype)


def expert_up(x, w1, w3, tile_expert, tile_active, tn=1024):
    p, d = x.shape
    f = w1.shape[2]
    tm = ROW_TILE
    return pl.pallas_call(
        _expert_up_kernel,
        grid_spec=pltpu.PrefetchScalarGridSpec(
            num_scalar_prefetch=2,
            grid=(p // tm, f // tn),
            in_specs=[pl.BlockSpec((tm, d), lambda i, j, te, ta: (i, 0)),
                      pl.BlockSpec((None, d, tn), lambda i, j, te, ta: (te[i], 0, j)),
                      pl.BlockSpec((None, d, tn), lambda i, j, te, ta: (te[i], 0, j))],
            out_specs=pl.BlockSpec((tm, tn), lambda i, j, te, ta: (i, j)),
        ),
        out_shape=jax.ShapeDtypeStruct((p, f), BF16),
        compiler_params=_cparams("arbitrary", "arbitrary"),
        name="moe_expert_up",
    )(tile_expert, tile_active, x, w1, w3)


def _expert_down_kernel(te_ref, ta_ref, a_ref, w_ref, o_ref, acc_ref, *, nk):
    i = pl.program_id(0)
    kk = pl.program_id(1)

    @pl.when(ta_ref[i] > 0)
    def _():
        part = jnp.dot(a_ref[...], w_ref[...], preferred_element_type=F32)

        @pl.when(kk == 0)
        def _():
            acc_ref[...] = part

        @pl.when(jnp.logical_and(kk > 0, kk < nk - 1))
        def _():
            acc_ref[...] += part

        @pl.when(kk == nk - 1)
        def _():
            o_ref[...] = acc_ref[...] + part

    @pl.when(jnp.logical_and(ta_ref[i] == 0, kk == nk - 1))
    def _():
        o_ref[...] = jnp.zeros(o_ref.shape, o_ref.dtype)


def expert_down(a, w2, tile_expert, tile_active, tk=2048):
    p, f = a.shape
    d = w2.shape[2]
    tm = ROW_TILE
    nk = f // tk
    assert nk >= 2
    return pl.pallas_call(
        functools.partial(_expert_down_kernel, nk=nk),
        grid_spec=pltpu.PrefetchScalarGridSpec(
            num_scalar_prefetch=2,
            grid=(p // tm, nk),
            in_specs=[pl.BlockSpec((tm, tk), lambda i, kk, te, ta: (i, kk)),
                      pl.BlockSpec((None, tk, d), lambda i, kk, te, ta: (te[i], kk, 0))],
            out_specs=pl.BlockSpec((tm, d), lambda i, kk, te, ta: (i, 0)),
            scratch_shapes=[pltpu.VMEM((tm, d), F32)],
        ),
        out_shape=jax.ShapeDtypeStruct((p, d), F32),
        compiler_params=_cparams("arbitrary", "arbitrary"),
        name="moe_expert_down",
    )(tile_expert, tile_active, a, w2)


def _combine_kernel(s0_ref, s1_ref, src_ref, route_ref, h_ref, gate_ref, gain_ref, o_ref, buf0, buf1, sem):
    _gather_rows_into(s0_ref, src_ref, buf0, sem.at[0])
    _gather_rows_into(s1_ref, src_ref, buf1, sem.at[1])
    w0 = route_ref[:, TOP_K:TOP_K + 1]
    w1 = route_ref[:, TOP_K + 1:TOP_K + 2]
    o_ref[...] = _residual_epilogue(w0 * buf0[...] + w1 * buf1[...], h_ref[...], gate_ref[...], gain_ref[...])


def moe_combine_resnorm(expert_out, slot0, slot1, route, h, gain, mods, gate_sec, seq, batch, out_rows, tile=256):
    rows, d = h.shape
    row = functools.partial(_mod_row, tm=tile, seq=seq, batch=batch)
    idx_spec = pl.BlockSpec((None, 1, tile), lambda i: (i, 0, 0), memory_space=pltpu.SMEM)
    return pl.pallas_call(
        _combine_kernel,
        grid=(out_rows // tile,),
        in_specs=[idx_spec, idx_spec,
                  pl.BlockSpec(memory_space=pl.ANY),
                  pl.BlockSpec((tile, LANES), lambda i: (i, 0)),
                  pl.BlockSpec((tile, d), lambda i: (i, 0)),
                  pl.BlockSpec((None, 1, d), lambda i: (row(i), 0, gate_sec)),
                  pl.BlockSpec((1, d), lambda i: (0, 0))],
        out_specs=pl.BlockSpec((tile, d), lambda i: (i, 0)),
        out_shape=jax.ShapeDtypeStruct((out_rows, d), F32),
        scratch_shapes=[pltpu.VMEM((tile, d), F32), pltpu.VMEM((tile, d), F32), pltpu.SemaphoreType.DMA((2,))],
        compiler_params=_cparams("arbitrary"),
        name="moe_combine_resnorm",
    )(slot0.reshape(rows // tile, 1, tile), slot1.reshape(rows // tile, 1, tile), expert_out, route, h, mods,
      gain.reshape(1, d))


def _routing_plan(route, tm):
    rows = route.shape[0]
    experts = route[:, :TOP_K].astype(jnp.int32).reshape(-1)
    onehot = (experts[:, None] == jnp.arange(N_EXPERTS)[None, :]).astype(jnp.int32)
    rank = jnp.sum((jnp.cumsum(onehot, axis=0) - onehot) * onehot, axis=1)
    counts = jnp.sum(onehot, axis=0)
    tiles_per = (counts + tm - 1) // tm
    tile_end = jnp.cumsum(tiles_per)
    start = (tile_end - tiles_per) * tm
    slot = jnp.sum(onehot * start[None, :], axis=1) + rank
    n_slots = rows * TOP_K + N_EXPERTS * tm
    n_tiles = n_slots // tm
    token_of_slot = jnp.zeros((n_slots,), jnp.int32).at[slot].set(jnp.arange(rows * TOP_K, dtype=jnp.int32) // TOP_K)
    tile_ids = jnp.arange(n_tiles, dtype=jnp.int32)
    tile_expert = jnp.minimum(jnp.sum((tile_ids[:, None] >= tile_end[None, :]).astype(jnp.int32), axis=1),
                              N_EXPERTS - 1)
    tile_active = (tile_ids < tile_end[-1]).astype(jnp.int32)
    slots = slot.reshape(rows, TOP_K).astype(jnp.int32)
    return token_of_slot, tile_expert, tile_active, slots[:, 0], slots[:, 1]


def moe_ffn_resnorm(y, h, w_router, w1, w3, w2, gain, mods, gate_sec, seq, batch, out_rows):
    route = moe_router(y, w_router)
    token_of_slot, tile_expert, tile_active, slot0, slot1 = _routing_plan(route, ROW_TILE)
    x = gather_rows(y, token_of_slot, tile=256)
    hid = expert_up(x, w1, w3, tile_expert, tile_active)
    out = expert_down(hid, w2, tile_expert, tile_active)
    return moe_combine_resnorm(out, slot0, slot1, route, h, gain, mods, gate_sec, seq, batch, out_rows)


def _lower_bounds(logits):
    p = jax.nn.softmax(logits.astype(F32), axis=0)
    return jnp.cumsum(p, axis=0) - p[0]


def kernel(x, c, ctx, c_ctx, w_mod, b_mod, norm_gains, attn_w_in, attn_w_o, attn_q_gain, attn_k_gain,
           rec_w_in, rec_w_o, rec_lb_logits, rec_o_gain, ffn_w1, ffn_w3, ffn_w2,
           moe_w_router, moe_w1, moe_w3, moe_w2):
    batch, seq, d = x.shape
    ctx_len = ctx.shape[1]
    depth = w_mod.shape[0]
    n_lat = batch * seq
    n_heads = d // REC_DK
    assert seq % ROW_TILE == 0 and (batch * ctx_len) % ROW_TILE == 0 and batch + 1 <= MOD_ROWS

    hg = jnp.concatenate([x.reshape(n_lat, d), ctx.reshape(batch * ctx_len, d)], axis=0)
    cond = jnp.zeros((MOD_ROWS, d), F32).at[:batch].set(c).at[batch].set(c_ctx)
    mods_all = modulation_table(cond, w_mod, b_mod)
    lower = _lower_bounds(rec_lb_logits)
    tables = rope_tables(seq, ROW_TILE)
    tq = min(256, ctx_len)
    tk = min(1024, seq // 4)

    for layer in range(depth):
        j = layer // 2
        ng = norm_gains[layer]
        mods = mods_all[layer]
        y = norm_modulate(hg, ng[0], mods, 0, 1, seq, batch, BF16)
        if layer % 2 == 0:
            qkv = matmul(y, attn_w_in[j].astype(BF16), BF16, tn=1024)
            q, k, v_ext = qk_prep(qkv, attn_q_gain[j], attn_k_gain[j], tables, seq, n_lat)
            mix = flash_attention(q, k, v_ext, seq, ctx_len, batch, tq, tk)
            hg = matmul_resnorm(mix, attn_w_o[j].astype(BF16), hg, ng[1], mods, 2, seq, batch, tk=d)
        else:
            proj = matmul(y, rec_w_in[j].astype(BF16), BF16, tn=1024)
            o_fw = hgrn_scan(proj, lower[layer, 0], 2, False, seq, ctx_len, batch, n_heads)
            o_bw = hgrn_scan(proj, lower[layer, 1], 3, True, seq, ctx_len, batch, n_heads)
            mix = hgrn_readout(o_fw, o_bw, proj, rec_o_gain[j], 4, n_heads)
            hg = matmul_resnorm(mix, rec_w_o[j].astype(BF16), hg, ng[1], mods, 2, seq, batch, tk=d)
        if layer % 2 == 0:
            y = norm_modulate(hg, ng[2], mods, 3, 4, seq, batch, BF16)
            hid = swiglu_up(y, ffn_w1[j].astype(BF16), ffn_w3[j].astype(BF16))
            f = hid.shape[1]
            hg = matmul_resnorm(hid, ffn_w2[j].astype(BF16), hg, ng[3], mods, 5, seq, batch, tk=f // 4)
        else:
            y = norm_modulate(hg, ng[2], mods, 3, 4, seq, batch, F32)
            out_rows = n_lat if layer == depth - 1 else hg.shape[0]
            hg = moe_ffn_resnorm(y, hg, moe_w_router[j], moe_w1[j].astype(BF16), moe_w3[j].astype(BF16),
                                 moe_w2[j].astype(BF16), ng[3], mods, 5, seq, batch, out_rows)
    return hg[:n_lat].reshape(batch, seq, d)
```

```python
import functools

import numpy as np
import jax
import jax.numpy as jnp
from jax import lax
from jax.experimental import pallas as pl
from jax.experimental.pallas import tpu as pltpu

F32 = jnp.float32
BF16 = jnp.bfloat16

HEAD_DIM = 128
N_KV_HEADS = 4
GROUP = 4
GRID_W = 64
ROPE_THETA = 10000.0
AXIS_ROT_DIM = HEAD_DIM // 2
REC_DK = 128
N_EXPERTS = 8
TOP_K = 2
N_MOD = 6
EPS = 1e-6
CHUNK = 64
N_LEVELS = 6

LANES = 128
VMEM_LIMIT = 56 * 2**20

ROW_TILE = 512
MOD_ROWS = 8


def _cparams(*sem):
    return pltpu.CompilerParams(dimension_semantics=sem, vmem_limit_bytes=VMEM_LIMIT)


def _mod_row(i, tm, seq, batch):
    return jnp.minimum((i * tm) // seq, batch)


def _mod_kernel(c_ref, w_ref, b_ref, o_ref):
    c = c_ref[...]
    s = (c * jax.nn.sigmoid(c)).astype(BF16)
    o_ref[...] = jnp.dot(s, w_ref[...].astype(BF16), preferred_element_type=F32) + b_ref[...]


def modulation_table(cond, w_mod, b_mod, tn=1024):
    depth, d, n = w_mod.shape
    out = pl.pallas_call(
        _mod_kernel,
        grid=(depth, n // tn),
        in_specs=[
            pl.BlockSpec((MOD_ROWS, d), lambda l, j: (0, 0)),
            pl.BlockSpec((None, d, tn), lambda l, j: (l, 0, j)),
            pl.BlockSpec((None, 1, tn), lambda l, j: (l, 0, j)),
        ],
        out_specs=pl.BlockSpec((None, MOD_ROWS, tn), lambda l, j: (l, 0, j)),
        out_shape=jax.ShapeDtypeStruct((depth, MOD_ROWS, n), F32),
        compiler_params=_cparams("arbitrary", "arbitrary"),
        name="modulation_table",
    )(cond, w_mod, b_mod.reshape(depth, 1, n))
    return out.reshape(depth, MOD_ROWS, 1, n)


def _norm_mod_kernel(h_ref, g_ref, sh_ref, sc_ref, o_ref):
    x = h_ref[...]
    ms = jnp.mean(x * x, axis=-1, keepdims=True)
    y = x * lax.rsqrt(ms + EPS) * g_ref[...]
    o_ref[...] = (y * (1.0 + sc_ref[...]) + sh_ref[...]).astype(o_ref.dtype)


def norm_modulate(h, gain, mods, shift_sec, scale_sec, seq, batch, out_dtype):
    rows, d = h.shape
    tm = ROW_TILE
    row = functools.partial(_mod_row, tm=tm, seq=seq, batch=batch)
    return pl.pallas_call(
        _norm_mod_kernel,
        grid=(rows // tm,),
        in_specs=[
            pl.BlockSpec((tm, d), lambda i: (i, 0)),
            pl.BlockSpec((1, d), lambda i: (0, 0)),
            pl.BlockSpec((None, 1, d), lambda i: (row(i), 0, shift_sec)),
            pl.BlockSpec((None, 1, d), lambda i: (row(i), 0, scale_sec)),
        ],
        out_specs=pl.BlockSpec((tm, d), lambda i: (i, 0)),
        out_shape=jax.ShapeDtypeStruct((rows, d), out_dtype),
        compiler_params=_cparams("arbitrary"),
        name="norm_modulate",
    )(h, gain.reshape(1, d), mods, mods)


def _mm_kernel(a_ref, b_ref, o_ref):
    o_ref[...] = jnp.dot(a_ref[...], b_ref[...], preferred_element_type=F32).astype(o_ref.dtype)


def matmul(a, b, out_dtype, tn):
    m, k = a.shape
    n = b.shape[1]
    tm = ROW_TILE
    return pl.pallas_call(
        _mm_kernel,
        grid=(m // tm, n // tn),
        in_specs=[pl.BlockSpec((tm, k), lambda i, j: (i, 0)),
                  pl.BlockSpec((k, tn), lambda i, j: (0, j))],
        out_specs=pl.BlockSpec((tm, tn), lambda i, j: (i, j)),
        out_shape=jax.ShapeDtypeStruct((m, n), out_dtype),
        compiler_params=_cparams("arbitrary", "arbitrary"),
        name="matmul",
    )(a, b)


def _swiglu_up_kernel(a_ref, w1_ref, w3_ref, o_ref):
    a = a_ref[...]
    u = jnp.dot(a, w1_ref[...], preferred_element_type=F32)
    g = jnp.dot(a, w3_ref[...], preferred_element_type=F32)
    o_ref[...] = (u * jax.nn.sigmoid(u) * g).astype(o_ref.dtype)


def swiglu_up(a, w1, w3, tn=512):
    m, k = a.shape
    n = w1.shape[1]
    tm = ROW_TILE
    return pl.pallas_call(
        _swiglu_up_kernel,
        grid=(m // tm, n // tn),
        in_specs=[pl.BlockSpec((tm, k), lambda i, j: (i, 0)),
                  pl.BlockSpec((k, tn), lambda i, j: (0, j)),
                  pl.BlockSpec((k, tn), lambda i, j: (0, j))],
        out_specs=pl.BlockSpec((tm, tn), lambda i, j: (i, j)),
        out_shape=jax.ShapeDtypeStruct((m, n), BF16),
        compiler_params=_cparams("arbitrary", "arbitrary"),
        name="swiglu_up",
    )(a, w1, w3)


def _residual_epilogue(r, h, gate, gain):
    ms = jnp.mean(r * r, axis=-1, keepdims=True)
    return h + gate * (r * lax.rsqrt(ms + EPS) * gain)


def _mm_resnorm_kernel(a_ref, b_ref, h_ref, gate_ref, gain_ref, o_ref, acc_ref, *, nk):
    kk = pl.program_id(1)
    part = jnp.dot(a_ref[...], b_ref[...], preferred_element_type=F32)

    if nk == 1:
        o_ref[...] = _residual_epilogue(part, h_ref[...], gate_ref[...], gain_ref[...])
        return

    @pl.when(kk == 0)
    def _():
        acc_ref[...] = part

    @pl.when(kk > 0)
    def _():
        acc_ref[...] += part

    @pl.when(kk == nk - 1)
    def _():
        o_ref[...] = _residual_epilogue(acc_ref[...], h_ref[...], gate_ref[...], gain_ref[...])


def matmul_resnorm(a, b, h, gain, mods, gate_sec, seq, batch, tk):
    m, k = a.shape
    d = b.shape[1]
    tm = ROW_TILE
    nk = k // tk
    row = functools.partial(_mod_row, tm=tm, seq=seq, batch=batch)
    return pl.pallas_call(
        functools.partial(_mm_resnorm_kernel, nk=nk),
        grid=(m // tm, nk),
        in_specs=[
            pl.BlockSpec((tm, tk), lambda i, kk: (i, kk)),
            pl.BlockSpec((tk, d), lambda i, kk: (kk, 0)),
            pl.BlockSpec((tm, d), lambda i, kk: (i, 0)),
            pl.BlockSpec((None, 1, d), lambda i, kk: (row(i), 0, gate_sec)),
            pl.BlockSpec((1, d), lambda i, kk: (0, 0)),
        ],
        out_specs=pl.BlockSpec((tm, d), lambda i, kk: (i, 0)),
        out_shape=jax.ShapeDtypeStruct((m, d), F32),
        scratch_shapes=[pltpu.VMEM((tm, d), F32)],
        compiler_params=_cparams("arbitrary", "arbitrary"),
        name="matmul_resnorm",
    )(a, b, h, mods, gain.reshape(1, d))


def rope_tables(seq, tile):
    rows = seq // GRID_W
    row = jnp.repeat(jnp.arange(rows, dtype=F32), GRID_W)
    col = jnp.tile(jnp.arange(GRID_W, dtype=F32), rows)
    inv_freq = ROPE_THETA ** (-jnp.arange(0, AXIS_ROT_DIM, 2, dtype=F32) / AXIS_ROT_DIM)
    ar, ac = row[:, None] * inv_freq, col[:, None] * inv_freq
    zeros = jnp.zeros_like(ar)
    cos = jnp.concatenate([jnp.cos(ar), jnp.cos(ar), jnp.cos(ac), jnp.cos(ac)], axis=-1)
    sin_lo = jnp.concatenate([-jnp.sin(ar), zeros, -jnp.sin(ac), zeros], axis=-1)
    sin_hi = jnp.concatenate([zeros, jnp.sin(ar), zeros, jnp.sin(ac)], axis=-1)
    pad0 = jnp.zeros((tile, HEAD_DIM), F32)
    return (jnp.concatenate([cos, jnp.ones((tile, HEAD_DIM), F32)], axis=0),
            jnp.concatenate([sin_lo, pad0], axis=0),
            jnp.concatenate([sin_hi, pad0], axis=0))


def _qk_prep_kernel(qkv_ref, qg_ref, kg_ref, cos_ref, slo_ref, shi_ref, q_ref, k_ref, v_ref, *, n_q, n_k, q_scale):
    cos, slo, shi = cos_ref[...], slo_ref[...], shi_ref[...]
    quarter = AXIS_ROT_DIM // 2

    def head(x, gain):
        x = x.astype(F32)
        ms = jnp.mean(x * x, axis=-1, keepdims=True)
        xn = x * lax.rsqrt(ms + EPS) * gain
        return (xn * cos + pltpu.roll(xn, HEAD_DIM - quarter, axis=1) * slo
                + pltpu.roll(xn, quarter, axis=1) * shi)

    for hh in range(n_q):
        sl = slice(hh * HEAD_DIM, (hh + 1) * HEAD_DIM)
        q_ref[:, sl] = (head(qkv_ref[:, sl], qg_ref[...]) * q_scale).astype(q_ref.dtype)
    for hh in range(n_k):
        src = slice((n_q + hh) * HEAD_DIM, (n_q + hh + 1) * HEAD_DIM)
        k_ref[:, hh * HEAD_DIM:(hh + 1) * HEAD_DIM] = head(qkv_ref[:, src], kg_ref[...]).astype(k_ref.dtype)
    for hh in range(n_k):
        src = slice((n_q + n_k + hh) * HEAD_DIM, (n_q + n_k + hh + 1) * HEAD_DIM)
        v_ref[:, 2 * hh * HEAD_DIM:(2 * hh + 1) * HEAD_DIM] = qkv_ref[:, src]
        v_ref[:, (2 * hh + 1) * HEAD_DIM:(2 * hh + 2) * HEAD_DIM] = jnp.ones((v_ref.shape[0], HEAD_DIM), v_ref.dtype)


def qk_prep(qkv, q_gain, k_gain, tables, seq, n_lat_rows):
    rows = qkv.shape[0]
    tm = ROW_TILE
    n_q = 4 * N_KV_HEADS
    per_batch = seq // tm
    n_lat_tiles = n_lat_rows // tm
    tab = lambda i: (jnp.where(i < n_lat_tiles, i % per_batch, per_batch), 0)
    q_scale = HEAD_DIM ** -0.5 * float(np.log2(np.e))
    return pl.pallas_call(
        functools.partial(_qk_prep_kernel, n_q=n_q, n_k=N_KV_HEADS, q_scale=q_scale),
        grid=(rows // tm,),
        in_specs=[
            pl.BlockSpec((tm, (n_q + 2 * N_KV_HEADS) * HEAD_DIM), lambda i: (i, 0)),
            pl.BlockSpec((1, HEAD_DIM), lambda i: (0, 0)),
            pl.BlockSpec((1, HEAD_DIM), lambda i: (0, 0)),
            pl.BlockSpec((tm, HEAD_DIM), tab),
            pl.BlockSpec((tm, HEAD_DIM), tab),
            pl.BlockSpec((tm, HEAD_DIM), tab),
        ],
        out_specs=[pl.BlockSpec((tm, n_q * HEAD_DIM), lambda i: (i, 0)),
                   pl.BlockSpec((tm, N_KV_HEADS * HEAD_DIM), lambda i: (i, 0)),
                   pl.BlockSpec((tm, 2 * N_KV_HEADS * HEAD_DIM), lambda i: (i, 0))],
        out_shape=[jax.ShapeDtypeStruct((rows, n_q * HEAD_DIM), BF16),
                   jax.ShapeDtypeStruct((rows, N_KV_HEADS * HEAD_DIM), BF16),
                   jax.ShapeDtypeStruct((rows, 2 * N_KV_HEADS * HEAD_DIM), BF16)],
        compiler_params=_cparams("arbitrary"),
        name="qk_prep",
    )(qkv, q_gain.reshape(1, HEAD_DIM), k_gain.reshape(1, HEAD_DIM), *tables)


def _flash_kernel(q_ref, k_ref, v_ref, kc_ref, vc_ref, o_ref, m_ref, acc_ref,
                  s0_ref, s1_ref, p0_ref, p1_ref, mx0_ref, mx1_ref, alpha0_ref, alpha1_ref, *,
                  n_lat_tiles, nkv, tq, tk, ctx):
    qi = pl.program_id(1)
    r = GROUP * tq
    rc = min(128, r)
    s_ref, p_ref = (s0_ref, s1_ref), (p0_ref, p1_ref)
    mx_ref, alpha_ref = (mx0_ref, mx1_ref), (alpha0_ref, alpha1_ref)
    q = jnp.concatenate([q_ref[:, g * HEAD_DIM:(g + 1) * HEAD_DIM] for g in range(GROUP)], axis=0)
    nt = (((1,), (1,)), ((), ()))

    def scores(k):
        return lax.dot_general(q, k, nt, preferred_element_type=F32)

    def times_values(p, v):
        return jnp.dot(p, v, preferred_element_type=F32)

    def kv_rows(j):
        start = j * tk
        return pl.ds(start if isinstance(j, int) else pl.multiple_of(start, tk), tk)

    def put_scores(slot, k):
        width = k.shape[0]
        s = scores(k)
        s_ref[slot][:, 0:width] = s
        mx = s[:, 0:LANES]
        for c0 in range(LANES, width, LANES):
            mx = jnp.maximum(mx, s[:, c0:c0 + LANES])
        mx_ref[slot][...] = mx

    def softmax_stage(slot, width, first):
        for r0 in range(0, r, rc):
            rows = slice(r0, r0 + rc)
            row_max = jnp.max(mx_ref[slot][rows, :], axis=-1, keepdims=True)
            if first:
                m_new = jnp.broadcast_to(row_max, (rc, LANES))
            else:
                m_prev = m_ref[rows, :]
                m_new = jnp.maximum(m_prev, row_max)
                alpha_ref[slot][rows, :] = jnp.exp2(m_prev - m_new)
            m_ref[rows, :] = m_new
            for c0 in range(0, width, LANES):
                cols = slice(c0, c0 + LANES)
                p_ref[slot][rows, cols] = jnp.exp2(s_ref[slot][rows, cols] - m_new).astype(BF16)

    def accumulate(slot, v):
        pv = times_values(p_ref[slot][...], v)
        for r0 in range(0, r, rc):
            rows = slice(r0, r0 + rc)
            a = alpha_ref[slot][rows, :]
            for c0 in (0, LANES):
                cols = slice(c0, c0 + LANES)
                acc_ref[rows, cols] = a * acc_ref[rows, cols] + pv[rows, cols]

    def stage(j, slot, last):
        if not last:
            put_scores(1 - slot, k_ref[kv_rows(j + 1), :])
        softmax_stage(slot, tk, False)
        accumulate(slot, v_ref[kv_rows(j), :])

    put_scores(0, kc_ref[...])
    softmax_stage(0, ctx, True)
    acc_ref[...] = times_values(p_ref[0][:, 0:ctx], vc_ref[...])

    @pl.when(qi < n_lat_tiles)
    def _():
        put_scores(0, k_ref[kv_rows(0), :])

        def pair(t, carry):
            stage(2 * t, 0, False)
            stage(2 * t + 1, 1, False)
            return carry

        lax.fori_loop(0, (nkv - 2) // 2, pair, 0)
        stage(nkv - 2, 0, False)
        stage(nkv - 1, 1, True)

    for g in range(GROUP):
        rows = slice(g * tq, (g + 1) * tq)
        o = acc_ref[rows, 0:HEAD_DIM] / acc_ref[rows, HEAD_DIM:2 * HEAD_DIM]
        o_ref[:, g * HEAD_DIM:(g + 1) * HEAD_DIM] = o.astype(o_ref.dtype)


def flash_attention(q, k, v_ext, seq, ctx, batch, tq, tk):
    rows = q.shape[0]
    n_q_heads = GROUP * N_KV_HEADS
    per_batch_q = seq // tq
    n_lat_tiles = batch * per_batch_q
    n_ctx_tiles = batch * (ctx // tq)
    nkv = seq // tk
    assert nkv >= 2 and nkv % 2 == 0 and ctx <= tk and ctx % LANES == 0
    ctx_blk0 = (batch * seq) // ctx
    r = GROUP * tq

    def batch_of(qi):
        return jnp.where(qi < n_lat_tiles, qi // per_batch_q, (qi - n_lat_tiles) // (ctx // tq))

    grid = (N_KV_HEADS, n_lat_tiles + n_ctx_tiles)
    return pl.pallas_call(
        functools.partial(_flash_kernel, n_lat_tiles=n_lat_tiles, nkv=nkv, tq=tq, tk=tk, ctx=ctx),
        grid=grid,
        in_specs=[
            pl.BlockSpec((tq, GROUP * HEAD_DIM), lambda h, qi: (qi, h)),
            pl.BlockSpec((seq, HEAD_DIM), lambda h, qi: (batch_of(qi), h)),
            pl.BlockSpec((seq, 2 * HEAD_DIM), lambda h, qi: (batch_of(qi), h)),
            pl.BlockSpec((ctx, HEAD_DIM), lambda h, qi: (ctx_blk0 + batch_of(qi), h)),
            pl.BlockSpec((ctx, 2 * HEAD_DIM), lambda h, qi: (ctx_blk0 + batch_of(qi), h)),
        ],
        out_specs=pl.BlockSpec((tq, GROUP * HEAD_DIM), lambda h, qi: (qi, h)),
        out_shape=jax.ShapeDtypeStruct((rows, n_q_heads * HEAD_DIM), BF16),
        scratch_shapes=[pltpu.VMEM((r, LANES), F32), pltpu.VMEM((r, 2 * HEAD_DIM), F32),
                        pltpu.VMEM((r, tk), F32), pltpu.VMEM((r, tk), F32),
                        pltpu.VMEM((r, tk), BF16), pltpu.VMEM((r, tk), BF16),
                        pltpu.VMEM((r, LANES), F32), pltpu.VMEM((r, LANES), F32),
                        pltpu.VMEM((r, LANES), F32), pltpu.VMEM((r, LANES), F32)],
        compiler_params=_cparams("arbitrary", "arbitrary"),
        name="flash_attention",
    )(q, k, v_ext, k, v_ext)


def _scan_constants(reverse, n_chunks):
    t = np.arange(CHUNK)[:, None]
    u = np.arange(CHUNK)[None, :]
    tri = (u >= t) if reverse else (u <= t)
    level_of = np.where(t == u, 0, -1)
    for lvl in range(N_LEVELS):
        b = CHUNK >> (lvl + 1)
        same = (t & ~(2 * b - 1)) == (u & ~(2 * b - 1))
        t_hi, s_hi = (t & b) != 0, (u & b) != 0
        level_of = np.where(same & (~t_hi & s_hi if reverse else t_hi & ~s_hi), lvl + 1, level_of)
    return jnp.asarray(np.kron(np.eye(n_chunks), tri), BF16), jnp.asarray(level_of, jnp.int32)


def _keep_bf16_bits(x):
    bits = lax.bitcast_convert_type(x, jnp.int32) & jnp.int32(-65536)
    return lax.bitcast_convert_type(bits, F32)


def _block_rows(x, block, off):
    rows, width = x.shape
    if block >= 8:
        parts = [jnp.broadcast_to(x[base + off:base + off + 1], (block, width)) for base in range(0, rows, block)]
        return parts[0] if len(parts) == 1 else jnp.concatenate(parts, axis=0)
    within = lax.broadcasted_iota(jnp.int32, x.shape, 0) & (block - 1)
    out = x
    for r in range(block):
        if r != off:
            out = jnp.where(within == r, pltpu.roll(x, (r - off) % rows, axis=0), out)
    return out


def _scan_kernel(q_ref, v_ref, f_ref, lb_ref, tri_ref, lvl_ref, o_ref,
                 st_ref, q32_ref, k32_ref, cb_ref, ql_ref, kl_ref, oi_ref, kv_ref, a_ref, *, reverse, n_chunks, hb):
    step = pl.program_id(2)

    @pl.when(step == 0)
    def _():
        st_ref[...] = jnp.zeros(st_ref.shape, F32)

    width = hb * REC_DK
    nt = (((1,), (1,)), ((), ()))
    tn = (((0,), (0,)), ((), ()))

    lb = lb_ref[...]
    qr = q_ref[...].astype(F32)
    q = (qr * (0.5 * REC_DK ** -0.5)) * (1.0 + jnp.tanh(0.5 * qr))
    fg = (0.5 + 0.5 * lb) + (0.5 - 0.5 * lb) * jnp.tanh(0.5 * f_ref[...].astype(F32))
    k = 1.0 - fg
    lf = jnp.log2(fg)
    hi = _keep_bf16_bits(lf)
    rest = lf - hi
    mid = _keep_bf16_bits(rest)
    lo = rest - mid
    sums = jnp.dot(tri_ref[...], jnp.concatenate([hi.astype(BF16), mid.astype(BF16), lo.astype(BF16)], axis=1),
                   preferred_element_type=F32)
    q32_ref[...] = q
    k32_ref[...] = k
    cb_ref[...] = sums[:, 0:width] + sums[:, width:2 * width] + sums[:, 2 * width:3 * width]
    ql_ref[0] = q.astype(BF16)
    kl_ref[0] = k.astype(BF16)

    tiles = [(c * hb + hh, slice(c * CHUNK, (c + 1) * CHUNK), slice(hh * REC_DK, (hh + 1) * REC_DK))
             for c in range(n_chunks) for hh in range(hb)]

    total = _block_rows(cb_ref[...], CHUNK, 0 if reverse else CHUNK - 1)
    qd = (q32_ref[...] * jnp.exp2(cb_ref[...])).astype(BF16)
    kd = (k32_ref[...] * jnp.exp2(total - cb_ref[...])).astype(BF16)
    for idx, rows, cols in tiles:
        kv_ref[idx] = lax.dot_general(v_ref[rows, cols], kd[rows, cols], tn, preferred_element_type=F32)

    level_of = lvl_ref[...]
    for lvl in range(N_LEVELS + 1):
        if lvl > 0:
            b = CHUNK >> lvl
            d = cb_ref[...] - _block_rows(cb_ref[...], 2 * b, b if reverse else b - 1)
            ql_ref[lvl] = (q32_ref[...] * jnp.exp2(jnp.minimum(d, 0.0))).astype(BF16)
            kl_ref[lvl] = (k32_ref[...] * jnp.exp2(jnp.minimum(-d, 0.0))).astype(BF16)
        for idx, rows, cols in tiles:
            pairs = lax.dot_general(ql_ref[lvl, rows, cols], kl_ref[lvl, rows, cols], nt,
                                    preferred_element_type=F32)
            a_ref[idx] = jnp.where(level_of == lvl, pairs, 0.0 if lvl == 0 else a_ref[idx])

    for idx, rows, cols in tiles:
        oi_ref[rows, cols] = jnp.dot(a_ref[idx].astype(BF16), v_ref[rows, cols], preferred_element_type=F32)

    decay = jnp.exp2(total)
    for c in (range(n_chunks - 1, -1, -1) if reverse else range(n_chunks)):
        rows = slice(c * CHUNK, (c + 1) * CHUNK)
        for hh in range(hb):
            cols = slice(hh * REC_DK, (hh + 1) * REC_DK)
            st = st_ref[hh]
            o = oi_ref[rows, cols] + lax.dot_general(qd[rows, cols], st.astype(BF16), nt,
                                                     preferred_element_type=F32)
            o_ref[rows, cols] = o.astype(o_ref.dtype)
            st_ref[hh] = st * decay[c * CHUNK:c * CHUNK + 1, cols] + kv_ref[c * hb + hh]


def hgrn_scan(proj, lb_dir, f_sec, reverse, seq, ctx, batch, n_heads, hb=4):
    rows = proj.shape[0]
    tb = ctx
    per_batch = seq // tb
    ctx_blk0 = (batch * seq) // tb
    groups = n_heads // hb
    width = hb * REC_DK
    n_chunks = tb // CHUNK
    tri, level_of = _scan_constants(reverse, n_chunks)

    def row_blk(b, s):
        lat = b * per_batch + (per_batch - s if reverse else s - 1)
        return jnp.where(s == 0, ctx_blk0 + b, lat)

    return pl.pallas_call(
        functools.partial(_scan_kernel, reverse=reverse, n_chunks=n_chunks, hb=hb),
        grid=(batch, groups, 1 + per_batch),
        in_specs=[
            pl.BlockSpec((tb, width), lambda b, h, s: (row_blk(b, s), h)),
            pl.BlockSpec((tb, width), lambda b, h, s: (row_blk(b, s), groups + h)),
            pl.BlockSpec((tb, width), lambda b, h, s: (row_blk(b, s), f_sec * groups + h)),
            pl.BlockSpec((None, 1, width), lambda b, h, s: (h, 0, 0)),
            pl.BlockSpec(tri.shape, lambda b, h, s: (0, 0)),
            pl.BlockSpec(level_of.shape, lambda b, h, s: (0, 0)),
        ],
        out_specs=pl.BlockSpec((tb, width), lambda b, h, s: (row_blk(b, s), h)),
        out_shape=jax.ShapeDtypeStruct((rows, n_heads * REC_DK), BF16),
        scratch_shapes=[pltpu.VMEM((hb, REC_DK, REC_DK), F32),
                        pltpu.VMEM((tb, width), F32), pltpu.VMEM((tb, width), F32), pltpu.VMEM((tb, width), F32),
                        pltpu.VMEM((N_LEVELS + 1, tb, width), BF16), pltpu.VMEM((N_LEVELS + 1, tb, width), BF16),
                        pltpu.VMEM((tb, width), F32), pltpu.VMEM((n_chunks * hb, REC_DK, REC_DK), F32),
                        pltpu.VMEM((n_chunks * hb, CHUNK, CHUNK), F32)],
        compiler_params=_cparams("arbitrary", "arbitrary", "arbitrary"),
        name="hgrn_scan_bw" if reverse else "hgrn_scan_fw",
    )(proj, proj, proj, lb_dir.reshape(groups, 1, width), tri, level_of)


def _readout_kernel(of_ref, ob_ref, g_ref, gain_ref, o_ref, *, n_heads):
    gain = gain_ref[...]
    for hh in range(n_heads):
        sl = slice(hh * REC_DK, (hh + 1) * REC_DK)
        o = of_ref[:, sl].astype(F32) + ob_ref[:, sl].astype(F32)
        ms = jnp.mean(o * o, axis=-1, keepdims=True)
        g = g_ref[:, sl].astype(F32)
        o_ref[:, sl] = (o * lax.rsqrt(ms + EPS) * gain * (g * jax.nn.sigmoid(g))).astype(o_ref.dtype)


def hgrn_readout(o_fw, o_bw, proj, o_gain, gate_sec, n_heads):
    rows, d = o_fw.shape
    tm = ROW_TILE
    return pl.pallas_call(
        functools.partial(_readout_kernel, n_heads=n_heads),
        grid=(rows // tm,),
        in_specs=[pl.BlockSpec((tm, d), lambda i: (i, 0)),
                  pl.BlockSpec((tm, d), lambda i: (i, 0)),
                  pl.BlockSpec((tm, d), lambda i: (i, gate_sec)),
                  pl.BlockSpec((1, REC_DK), lambda i: (0, 0))],
        out_specs=pl.BlockSpec((tm, d), lambda i: (i, 0)),
        out_shape=jax.ShapeDtypeStruct((rows, d), BF16),
        compiler_params=_cparams("arbitrary"),
        name="hgrn_readout",
    )(o_fw, o_bw, proj, o_gain.reshape(1, REC_DK))


def _router_kernel(y_ref, w_ref, o_ref):
    logits = jnp.dot(y_ref[...], w_ref[...], preferred_element_type=F32, precision=lax.Precision.HIGHEST)
    lane = lax.broadcasted_iota(jnp.int32, logits.shape, 1)
    neg = -jnp.inf
    logits = jnp.where(lane < N_EXPERTS, logits, neg)
    m1 = jnp.max(logits, axis=-1, keepdims=True)
    i1 = jnp.min(jnp.where(logits == m1, lane, LANES), axis=-1, keepdims=True)
    rest = jnp.where(lane == i1, neg, logits)
    m2 = jnp.max(rest, axis=-1, keepdims=True)
    i2 = jnp.min(jnp.where(rest == m2, lane, LANES), axis=-1, keepdims=True)
    e = jnp.exp(m2 - m1)
    w1 = 1.0 / (1.0 + e)
    w2 = e / (1.0 + e)
    out = jnp.where(lane == 0, i1.astype(F32),
                    jnp.where(lane == 1, i2.astype(F32),
                              jnp.where(lane == 2, w1, jnp.where(lane == 3, w2, 0.0))))
    o_ref[...] = out


def moe_router(y, w_router):
    rows, d = y.shape
    tm = ROW_TILE
    w_pad = jnp.zeros((d, LANES), F32).at[:, :N_EXPERTS].set(w_router)
    return pl.pallas_call(
        _router_kernel,
        grid=(rows // tm,),
        in_specs=[pl.BlockSpec((tm, d), lambda i: (i, 0)), pl.BlockSpec((d, LANES), lambda i: (0, 0))],
        out_specs=pl.BlockSpec((tm, LANES), lambda i: (i, 0)),
        out_shape=jax.ShapeDtypeStruct((rows, LANES), F32),
        compiler_params=_cparams("arbitrary"),
        name="moe_router",
    )(y, w_pad)


def _row_copy(src_ref, dst_ref, sem, src_row, dst_row):
    return pltpu.make_async_copy(src_ref.at[pl.ds(src_row, 1), :], dst_ref.at[pl.ds(dst_row, 1), :], sem)


def _gather_rows_into(idx_ref, src_ref, dst_ref, sem):
    n = dst_ref.shape[0]

    def start(i, carry):
        for prio in range(2):
            r = 2 * i + prio
            _row_copy(src_ref, dst_ref, sem, idx_ref[0, r], r).start(priority=prio)
        return carry

    def wait(r, carry):
        _row_copy(src_ref, dst_ref, sem, 0, r).wait()
        return carry

    lax.fori_loop(0, n // 2, start, 0)
    lax.fori_loop(0, n, wait, 0)


def _gather_kernel(idx_ref, src_ref, o_ref, sem):
    _gather_rows_into(idx_ref, src_ref, o_ref, sem)


def gather_rows(src, idx, tile):
    n = idx.shape[0]
    d = src.shape[1]
    return pl.pallas_call(
        _gather_kernel,
        grid=(n // tile,),
        in_specs=[pl.BlockSpec((None, 1, tile), lambda i: (i, 0, 0), memory_space=pltpu.SMEM),
                  pl.BlockSpec(memory_space=pl.ANY)],
        out_specs=pl.BlockSpec((tile, d), lambda i: (i, 0)),
        out_shape=jax.ShapeDtypeStruct((n, d), src.dtype),
        scratch_shapes=[pltpu.SemaphoreType.DMA(())],
        compiler_params=_cparams("arbitrary"),
        name="moe_gather_rows",
    )(idx.reshape(n // tile, 1, tile), src)


def _expert_up_kernel(te_ref, ta_ref, x_ref, w1_ref, w3_ref, o_ref):
    i = pl.program_id(0)

    @pl.when(ta_ref[i] > 0)
    def _():
        x = x_ref[...].astype(BF16)
        u = jnp.dot(x, w1_ref[...], preferred_element_type=F32)
        g = jnp.dot(x, w3_ref[...], preferred_element_type=F32)
        o_ref[...] = (u * jax.nn.sigmoid(u) * g).astype(o_ref.dtype)

    @pl.when(ta_ref[i] == 0)
    def _():
        o_ref[...] = jnp.zeros(o_ref.shape, o_ref.dtype)


def expert_up(x, w1, w3, tile_expert, tile_active, tn=1024):
    p, d = x.shape
    f = w1.shape[2]
    tm = ROW_TILE
    return pl.pallas_call(
        _expert_up_kernel,
        grid_spec=pltpu.PrefetchScalarGridSpec(
            num_scalar_prefetch=2,
            grid=(p // tm, f // tn),
            in_specs=[pl.BlockSpec((tm, d), lambda i, j, te, ta: (i, 0)),
                      pl.BlockSpec((None, d, tn), lambda i, j, te, ta: (te[i], 0, j)),
                      pl.BlockSpec((None, d, tn), lambda i, j, te, ta: (te[i], 0, j))],
            out_specs=pl.BlockSpec((tm, tn), lambda i, j, te, ta: (i, j)),
        ),
        out_shape=jax.ShapeDtypeStruct((p, f), BF16),
        compiler_params=_cparams("arbitrary", "arbitrary"),
        name="moe_expert_up",
    )(tile_expert, tile_active, x, w1, w3)


def _expert_down_kernel(te_ref, ta_ref, a_ref, w_ref, o_ref, acc_ref, *, nk):
    i = pl.program_id(0)
    kk = pl.program_id(1)

    @pl.when(ta_ref[i] > 0)
    def _():
        part = jnp.dot(a_ref[...], w_ref[...], preferred_element_type=F32)

        @pl.when(kk == 0)
        def _():
            acc_ref[...] = part

        @pl.when(jnp.logical_and(kk > 0, kk < nk - 1))
        def _():
            acc_ref[...] += part

        @pl.when(kk == nk - 1)
        def _():
            o_ref[...] = acc_ref[...] + part

    @pl.when(jnp.logical_and(ta_ref[i] == 0, kk == nk - 1))
    def _():
        o_ref[...] = jnp.zeros(o_ref.shape, o_ref.dtype)


def expert_down(a, w2, tile_expert, tile_active, tk=2048):
    p, f = a.shape
    d = w2.shape[2]
    tm = ROW_TILE
    nk = f // tk
    assert nk >= 2
    return pl.pallas_call(
        functools.partial(_expert_down_kernel, nk=nk),
        grid_spec=pltpu.PrefetchScalarGridSpec(
            num_scalar_prefetch=2,
            grid=(p // tm, nk),
            in_specs=[pl.BlockSpec((tm, tk), lambda i, kk, te, ta: (i, kk)),
                      pl.BlockSpec((None, tk, d), lambda i, kk, te, ta: (te[i], kk, 0))],
            out_specs=pl.BlockSpec((tm, d), lambda i, kk, te, ta: (i, 0)),
            scratch_shapes=[pltpu.VMEM((tm, d), F32)],
        ),
        out_shape=jax.ShapeDtypeStruct((p, d), F32),
        compiler_params=_cparams("arbitrary", "arbitrary"),
        name="moe_expert_down",
    )(tile_expert, tile_active, a, w2)


def _combine_kernel(s0_ref, s1_ref, src_ref, route_ref, h_ref, gate_ref, gain_ref, o_ref, buf0, buf1, sem):
    _gather_rows_into(s0_ref, src_ref, buf0, sem.at[0])
    _gather_rows_into(s1_ref, src_ref, buf1, sem.at[1])
    w0 = route_ref[:, TOP_K:TOP_K + 1]
    w1 = route_ref[:, TOP_K + 1:TOP_K + 2]
    o_ref[...] = _residual_epilogue(w0 * buf0[...] + w1 * buf1[...], h_ref[...], gate_ref[...], gain_ref[...])


def moe_combine_resnorm(expert_out, slot0, slot1, route, h, gain, mods, gate_sec, seq, batch, out_rows, tile=256):
    rows, d = h.shape
    row = functools.partial(_mod_row, tm=tile, seq=seq, batch=batch)
    idx_spec = pl.BlockSpec((None, 1, tile), lambda i: (i, 0, 0), memory_space=pltpu.SMEM)
    return pl.pallas_call(
        _combine_kernel,
        grid=(out_rows // tile,),
        in_specs=[idx_spec, idx_spec,
                  pl.BlockSpec(memory_space=pl.ANY),
                  pl.BlockSpec((tile, LANES), lambda i: (i, 0)),
                  pl.BlockSpec((tile, d), lambda i: (i, 0)),
                  pl.BlockSpec((None, 1, d), lambda i: (row(i), 0, gate_sec)),
                  pl.BlockSpec((1, d), lambda i: (0, 0))],
        out_specs=pl.BlockSpec((tile, d), lambda i: (i, 0)),
        out_shape=jax.ShapeDtypeStruct((out_rows, d), F32),
        scratch_shapes=[pltpu.VMEM((tile, d), F32), pltpu.VMEM((tile, d), F32), pltpu.SemaphoreType.DMA((2,))],
        compiler_params=_cparams("arbitrary"),
        name="moe_combine_resnorm",
    )(slot0.reshape(rows // tile, 1, tile), slot1.reshape(rows // tile, 1, tile), expert_out, route, h, mods,
      gain.reshape(1, d))


def _routing_plan(route, tm):
    rows = route.shape[0]
    experts = route[:, :TOP_K].astype(jnp.int32).reshape(-1)
    onehot = (experts[:, None] == jnp.arange(N_EXPERTS)[None, :]).astype(jnp.int32)
    rank = jnp.sum((jnp.cumsum(onehot, axis=0) - onehot) * onehot, axis=1)
    counts = jnp.sum(onehot, axis=0)
    tiles_per = (counts + tm - 1) // tm
    tile_end = jnp.cumsum(tiles_per)
    start = (tile_end - tiles_per) * tm
    slot = jnp.sum(onehot * start[None, :], axis=1) + rank
    n_slots = rows * TOP_K + N_EXPERTS * tm
    n_tiles = n_slots // tm
    token_of_slot = jnp.zeros((n_slots,), jnp.int32).at[slot].set(jnp.arange(rows * TOP_K, dtype=jnp.int32) // TOP_K)
    tile_ids = jnp.arange(n_tiles, dtype=jnp.int32)
    tile_expert = jnp.minimum(jnp.sum((tile_ids[:, None] >= tile_end[None, :]).astype(jnp.int32), axis=1),
                              N_EXPERTS - 1)
    tile_active = (tile_ids < tile_end[-1]).astype(jnp.int32)
    slots = slot.reshape(rows, TOP_K).astype(jnp.int32)
    return token_of_slot, tile_expert, tile_active, slots[:, 0], slots[:, 1]


def moe_ffn_resnorm(y, h, w_router, w1, w3, w2, gain, mods, gate_sec, seq, batch, out_rows):
    route = moe_router(y, w_router)
    token_of_slot, tile_expert, tile_active, slot0, slot1 = _routing_plan(route, ROW_TILE)
    x = gather_rows(y, token_of_slot, tile=256)
    hid = expert_up(x, w1, w3, tile_expert, tile_active)
    out = expert_down(hid, w2, tile_expert, tile_active)
    return moe_combine_resnorm(out, slot0, slot1, route, h, gain, mods, gate_sec, seq, batch, out_rows)


def _lower_bounds(logits):
    p = jax.nn.softmax(logits.astype(F32), axis=0)
    return jnp.cumsum(p, axis=0) - p[0]


def kernel(x, c, ctx, c_ctx, w_mod, b_mod, norm_gains, attn_w_in, attn_w_o, attn_q_gain, attn_k_gain,
           rec_w_in, rec_w_o, rec_lb_logits, rec_o_gain, ffn_w1, ffn_w3, ffn_w2,
           moe_w_router, moe_w1, moe_w3, moe_w2):
    batch, seq, d = x.shape
    ctx_len = ctx.shape[1]
    depth = w_mod.shape[0]
    n_lat = batch * seq
    n_heads = d // REC_DK
    assert seq % ROW_TILE == 0 and (batch * ctx_len) % ROW_TILE == 0 and batch + 1 <= MOD_ROWS

    hg = jnp.concatenate([x.reshape(n_lat, d), ctx.reshape(batch * ctx_len, d)], axis=0)
    cond = jnp.zeros((MOD_ROWS, d), F32).at[:batch].set(c).at[batch].set(c_ctx)
    mods_all = modulation_table(cond, w_mod, b_mod)
    lower = _lower_bounds(rec_lb_logits)
    tables = rope_tables(seq, ROW_TILE)
    tq = min(256, ctx_len)
    tk = min(2048, seq // 4)

    for layer in range(depth):
        j = layer // 2
        ng = norm_gains[layer]
        mods = mods_all[layer]
        y = norm_modulate(hg, ng[0], mods, 0, 1, seq, batch, BF16)
        if layer % 2 == 0:
            qkv = matmul(y, attn_w_in[j].astype(BF16), BF16, tn=1024)
            q, k, v_ext = qk_prep(qkv, attn_q_gain[j], attn_k_gain[j], tables, seq, n_lat)
            mix = flash_attention(q, k, v_ext, seq, ctx_len, batch, tq, tk)
            hg = matmul_resnorm(mix, attn_w_o[j].astype(BF16), hg, ng[1], mods, 2, seq, batch, tk=d)
        else:
            proj = matmul(y, rec_w_in[j].astype(BF16), BF16, tn=1024)
            o_fw = hgrn_scan(proj, lower[layer, 0], 2, False, seq, ctx_len, batch, n_heads)
            o_bw = hgrn_scan(proj, lower[layer, 1], 3, True, seq, ctx_len, batch, n_heads)
            mix = hgrn_readout(o_fw, o_bw, proj, rec_o_gain[j], 4, n_heads)
            hg = matmul_resnorm(mix, rec_w_o[j].astype(BF16), hg, ng[1], mods, 2, seq, batch, tk=d)
        if layer % 2 == 0:
            y = norm_modulate(hg, ng[2], mods, 3, 4, seq, batch, BF16)
            hid = swiglu_up(y, ffn_w1[j].astype(BF16), ffn_w3[j].astype(BF16))
            f = hid.shape[1]
            hg = matmul_resnorm(hid, ffn_w2[j].astype(BF16), hg, ng[3], mods, 5, seq, batch, tk=f // 4)
        else:
            y = norm_modulate(hg, ng[2], mods, 3, 4, seq, batch, F32)
            out_rows = n_lat if layer == depth - 1 else hg.shape[0]
            hg = moe_ffn_resnorm(y, hg, moe_w_router[j], moe_w1[j].astype(BF16), moe_w3[j].astype(BF16),
                                 moe_w2[j].astype(BF16), ng[3], mods, 5, seq, batch, out_rows)
    return hg[:n_lat].reshape(batch, seq, d)
```

```python
import functools

import numpy as np
import jax
import jax.numpy as jnp
from jax import lax
from jax.experimental import pallas as pl
from jax.experimental.pallas import tpu as pltpu

F32 = jnp.float32
BF16 = jnp.bfloat16

HEAD_DIM = 128
N_KV_HEADS = 4
GROUP = 4
GRID_W = 64
ROPE_THETA = 10000.0
AXIS_ROT_DIM = HEAD_DIM // 2
REC_DK = 128
N_EXPERTS = 8
TOP_K = 2
N_MOD = 6
EPS = 1e-6
CHUNK = 64
N_LEVELS = 6

LANES = 128
VMEM_LIMIT = 56 * 2**20

ROW_TILE = 512
MOD_ROWS = 8


def _cparams(*sem):
    return pltpu.CompilerParams(dimension_semantics=sem, vmem_limit_bytes=VMEM_LIMIT)


def _mod_row(i, tm, seq, batch):
    return jnp.minimum((i * tm) // seq, batch)


def _mod_kernel(c_ref, w_ref, b_ref, o_ref):
    c = c_ref[...]
    s = (c * jax.nn.sigmoid(c)).astype(BF16)
    o_ref[...] = jnp.dot(s, w_ref[...].astype(BF16), preferred_element_type=F32) + b_ref[...]


def modulation_table(cond, w_mod, b_mod, tn=1024):
    depth, d, n = w_mod.shape
    out = pl.pallas_call(
        _mod_kernel,
        grid=(depth, n // tn),
        in_specs=[
            pl.BlockSpec((MOD_ROWS, d), lambda l, j: (0, 0)),
            pl.BlockSpec((None, d, tn), lambda l, j: (l, 0, j)),
            pl.BlockSpec((None, 1, tn), lambda l, j: (l, 0, j)),
        ],
        out_specs=pl.BlockSpec((None, MOD_ROWS, tn), lambda l, j: (l, 0, j)),
        out_shape=jax.ShapeDtypeStruct((depth, MOD_ROWS, n), F32),
        compiler_params=_cparams("arbitrary", "arbitrary"),
        name="modulation_table",
    )(cond, w_mod, b_mod.reshape(depth, 1, n))
    return out.reshape(depth, MOD_ROWS, 1, n)


def _norm_mod_kernel(h_ref, g_ref, sh_ref, sc_ref, o_ref):
    x = h_ref[...]
    ms = jnp.mean(x * x, axis=-1, keepdims=True)
    y = x * lax.rsqrt(ms + EPS) * g_ref[...]
    o_ref[...] = (y * (1.0 + sc_ref[...]) + sh_ref[...]).astype(o_ref.dtype)


def norm_modulate(h, gain, mods, shift_sec, scale_sec, seq, batch, out_dtype):
    rows, d = h.shape
    tm = ROW_TILE
    row = functools.partial(_mod_row, tm=tm, seq=seq, batch=batch)
    return pl.pallas_call(
        _norm_mod_kernel,
        grid=(rows // tm,),
        in_specs=[
            pl.BlockSpec((tm, d), lambda i: (i, 0)),
            pl.BlockSpec((1, d), lambda i: (0, 0)),
            pl.BlockSpec((None, 1, d), lambda i: (row(i), 0, shift_sec)),
            pl.BlockSpec((None, 1, d), lambda i: (row(i), 0, scale_sec)),
        ],
        out_specs=pl.BlockSpec((tm, d), lambda i: (i, 0)),
        out_shape=jax.ShapeDtypeStruct((rows, d), out_dtype),
        compiler_params=_cparams("arbitrary"),
        name="norm_modulate",
    )(h, gain.reshape(1, d), mods, mods)


def _normed_lhs(h_ref, g_ref, sh_ref, sc_ref, y_ref):
    @pl.when(pl.program_id(1) == 0)
    def _():
        _norm_mod_kernel(h_ref, g_ref, sh_ref, sc_ref, y_ref)


def _norm_specs(tm, d, row, shift_sec, scale_sec):
    return [pl.BlockSpec((tm, d), lambda i, j: (i, 0)),
            pl.BlockSpec((1, d), lambda i, j: (0, 0)),
            pl.BlockSpec((None, 1, d), lambda i, j: (row(i), 0, shift_sec)),
            pl.BlockSpec((None, 1, d), lambda i, j: (row(i), 0, scale_sec))]


def _norm_mm_kernel(h_ref, g_ref, sh_ref, sc_ref, b_ref, o_ref, y_ref):
    _normed_lhs(h_ref, g_ref, sh_ref, sc_ref, y_ref)
    o_ref[...] = jnp.dot(y_ref[...], b_ref[...], preferred_element_type=F32).astype(o_ref.dtype)


def norm_matmul(h, gain, mods, shift_sec, scale_sec, seq, batch, b, out_dtype, tn):
    m, d = h.shape
    n = b.shape[1]
    tm = ROW_TILE
    row = functools.partial(_mod_row, tm=tm, seq=seq, batch=batch)
    return pl.pallas_call(
        _norm_mm_kernel,
        grid=(m // tm, n // tn),
        in_specs=_norm_specs(tm, d, row, shift_sec, scale_sec) + [pl.BlockSpec((d, tn), lambda i, j: (0, j))],
        out_specs=pl.BlockSpec((tm, tn), lambda i, j: (i, j)),
        out_shape=jax.ShapeDtypeStruct((m, n), out_dtype),
        scratch_shapes=[pltpu.VMEM((tm, d), BF16)],
        compiler_params=_cparams("arbitrary", "arbitrary"),
        name="norm_matmul",
    )(h, gain.reshape(1, d), mods, mods, b)


def _norm_swiglu_up_kernel(h_ref, g_ref, sh_ref, sc_ref, w1_ref, w3_ref, o_ref, y_ref):
    _normed_lhs(h_ref, g_ref, sh_ref, sc_ref, y_ref)
    a = y_ref[...]
    u = jnp.dot(a, w1_ref[...], preferred_element_type=F32)
    g = jnp.dot(a, w3_ref[...], preferred_element_type=F32)
    o_ref[...] = (u * jax.nn.sigmoid(u) * g).astype(o_ref.dtype)


def norm_swiglu_up(h, gain, mods, shift_sec, scale_sec, seq, batch, w1, w3, tn=512):
    m, d = h.shape
    n = w1.shape[1]
    tm = ROW_TILE
    row = functools.partial(_mod_row, tm=tm, seq=seq, batch=batch)
    return pl.pallas_call(
        _norm_swiglu_up_kernel,
        grid=(m // tm, n // tn),
        in_specs=_norm_specs(tm, d, row, shift_sec, scale_sec) + [pl.BlockSpec((d, tn), lambda i, j: (0, j)),
                                                                  pl.BlockSpec((d, tn), lambda i, j: (0, j))],
        out_specs=pl.BlockSpec((tm, tn), lambda i, j: (i, j)),
        out_shape=jax.ShapeDtypeStruct((m, n), BF16),
        scratch_shapes=[pltpu.VMEM((tm, d), BF16)],
        compiler_params=_cparams("arbitrary", "arbitrary"),
        name="norm_swiglu_up",
    )(h, gain.reshape(1, d), mods, mods, w1, w3)


def _residual_epilogue(r, h, gate, gain):
    ms = jnp.mean(r * r, axis=-1, keepdims=True)
    return h + gate * (r * lax.rsqrt(ms + EPS) * gain)


def _mm_resnorm_kernel(a_ref, b_ref, h_ref, gate_ref, gain_ref, o_ref, acc_ref, *, nk):
    kk = pl.program_id(1)
    part = jnp.dot(a_ref[...], b_ref[...], preferred_element_type=F32)

    if nk == 1:
        o_ref[...] = _residual_epilogue(part, h_ref[...], gate_ref[...], gain_ref[...])
        return

    @pl.when(kk == 0)
    def _():
        acc_ref[...] = part

    @pl.when(kk > 0)
    def _():
        acc_ref[...] += part

    @pl.when(kk == nk - 1)
    def _():
        o_ref[...] = _residual_epilogue(acc_ref[...], h_ref[...], gate_ref[...], gain_ref[...])


def matmul_resnorm(a, b, h, gain, mods, gate_sec, seq, batch, tk):
    m, k = a.shape
    d = b.shape[1]
    tm = ROW_TILE
    nk = k // tk
    row = functools.partial(_mod_row, tm=tm, seq=seq, batch=batch)
    return pl.pallas_call(
        functools.partial(_mm_resnorm_kernel, nk=nk),
        grid=(m // tm, nk),
        in_specs=[
            pl.BlockSpec((tm, tk), lambda i, kk: (i, kk)),
            pl.BlockSpec((tk, d), lambda i, kk: (kk, 0)),
            pl.BlockSpec((tm, d), lambda i, kk: (i, 0)),
            pl.BlockSpec((None, 1, d), lambda i, kk: (row(i), 0, gate_sec)),
            pl.BlockSpec((1, d), lambda i, kk: (0, 0)),
        ],
        out_specs=pl.BlockSpec((tm, d), lambda i, kk: (i, 0)),
        out_shape=jax.ShapeDtypeStruct((m, d), F32),
        scratch_shapes=[pltpu.VMEM((tm, d), F32)],
        compiler_params=_cparams("arbitrary", "arbitrary"),
        name="matmul_resnorm",
    )(a, b, h, mods, gain.reshape(1, d))


def rope_tables(seq, tile):
    rows = seq // GRID_W
    row = jnp.repeat(jnp.arange(rows, dtype=F32), GRID_W)
    col = jnp.tile(jnp.arange(GRID_W, dtype=F32), rows)
    inv_freq = ROPE_THETA ** (-jnp.arange(0, AXIS_ROT_DIM, 2, dtype=F32) / AXIS_ROT_DIM)
    ar, ac = row[:, None] * inv_freq, col[:, None] * inv_freq
    zeros = jnp.zeros_like(ar)
    cos = jnp.concatenate([jnp.cos(ar), jnp.cos(ar), jnp.cos(ac), jnp.cos(ac)], axis=-1)
    sin_lo = jnp.concatenate([-jnp.sin(ar), zeros, -jnp.sin(ac), zeros], axis=-1)
    sin_hi = jnp.concatenate([zeros, jnp.sin(ar), zeros, jnp.sin(ac)], axis=-1)
    pad0 = jnp.zeros((tile, HEAD_DIM), F32)
    return (jnp.concatenate([cos, jnp.ones((tile, HEAD_DIM), F32)], axis=0),
            jnp.concatenate([sin_lo, pad0], axis=0),
            jnp.concatenate([sin_hi, pad0], axis=0))


def _qk_prep_kernel(qkv_ref, qg_ref, kg_ref, cos_ref, slo_ref, shi_ref, q_ref, k_ref, v_ref, *, n_q, n_k, q_scale):
    cos, slo, shi = cos_ref[...], slo_ref[...], shi_ref[...]
    quarter = AXIS_ROT_DIM // 2

    def head(x, gain):
        x = x.astype(F32)
        ms = jnp.mean(x * x, axis=-1, keepdims=True)
        xn = x * lax.rsqrt(ms + EPS) * gain
        return (xn * cos + pltpu.roll(xn, HEAD_DIM - quarter, axis=1) * slo
                + pltpu.roll(xn, quarter, axis=1) * shi)

    for hh in range(n_q):
        sl = slice(hh * HEAD_DIM, (hh + 1) * HEAD_DIM)
        q_ref[:, sl] = (head(qkv_ref[:, sl], qg_ref[...]) * q_scale).astype(q_ref.dtype)
    for hh in range(n_k):
        src = slice((n_q + hh) * HEAD_DIM, (n_q + hh + 1) * HEAD_DIM)
        k_ref[:, hh * HEAD_DIM:(hh + 1) * HEAD_DIM] = head(qkv_ref[:, src], kg_ref[...]).astype(k_ref.dtype)
    for hh in range(n_k):
        src = slice((n_q + n_k + hh) * HEAD_DIM, (n_q + n_k + hh + 1) * HEAD_DIM)
        v_ref[:, 2 * hh * HEAD_DIM:(2 * hh + 1) * HEAD_DIM] = qkv_ref[:, src]
        v_ref[:, (2 * hh + 1) * HEAD_DIM:(2 * hh + 2) * HEAD_DIM] = jnp.ones((v_ref.shape[0], HEAD_DIM), v_ref.dtype)


def qk_prep(qkv, q_gain, k_gain, tables, seq, n_lat_rows):
    rows = qkv.shape[0]
    tm = ROW_TILE
    n_q = 4 * N_KV_HEADS
    per_batch = seq // tm
    n_lat_tiles = n_lat_rows // tm
    tab = lambda i: (jnp.where(i < n_lat_tiles, i % per_batch, per_batch), 0)
    q_scale = HEAD_DIM ** -0.5 * float(np.log2(np.e))
    return pl.pallas_call(
        functools.partial(_qk_prep_kernel, n_q=n_q, n_k=N_KV_HEADS, q_scale=q_scale),
        grid=(rows // tm,),
        in_specs=[
            pl.BlockSpec((tm, (n_q + 2 * N_KV_HEADS) * HEAD_DIM), lambda i: (i, 0)),
            pl.BlockSpec((1, HEAD_DIM), lambda i: (0, 0)),
            pl.BlockSpec((1, HEAD_DIM), lambda i: (0, 0)),
            pl.BlockSpec((tm, HEAD_DIM), tab),
            pl.BlockSpec((tm, HEAD_DIM), tab),
            pl.BlockSpec((tm, HEAD_DIM), tab),
        ],
        out_specs=[pl.BlockSpec((tm, n_q * HEAD_DIM), lambda i: (i, 0)),
                   pl.BlockSpec((tm, N_KV_HEADS * HEAD_DIM), lambda i: (i, 0)),
                   pl.BlockSpec((tm, 2 * N_KV_HEADS * HEAD_DIM), lambda i: (i, 0))],
        out_shape=[jax.ShapeDtypeStruct((rows, n_q * HEAD_DIM), BF16),
                   jax.ShapeDtypeStruct((rows, N_KV_HEADS * HEAD_DIM), BF16),
                   jax.ShapeDtypeStruct((rows, 2 * N_KV_HEADS * HEAD_DIM), BF16)],
        compiler_params=_cparams("arbitrary"),
        name="qk_prep",
    )(qkv, q_gain.reshape(1, HEAD_DIM), k_gain.reshape(1, HEAD_DIM), *tables)


def _flash_kernel(q_ref, k_ref, v_ref, kc_ref, vc_ref, o_ref, m_ref, acc_ref,
                  s0_ref, s1_ref, p0_ref, p1_ref, mx0_ref, mx1_ref, alpha0_ref, alpha1_ref, *,
                  n_lat_tiles, nkv, tq, tk, ctx):
    qi = pl.program_id(1)
    r = GROUP * tq
    rc = min(128, r)
    s_ref, p_ref = (s0_ref, s1_ref), (p0_ref, p1_ref)
    mx_ref, alpha_ref = (mx0_ref, mx1_ref), (alpha0_ref, alpha1_ref)
    q = jnp.concatenate([q_ref[:, g * HEAD_DIM:(g + 1) * HEAD_DIM] for g in range(GROUP)], axis=0)
    nt = (((1,), (1,)), ((), ()))

    def scores(k):
        return lax.dot_general(q, k, nt, preferred_element_type=F32)

    def times_values(p, v):
        return jnp.dot(p, v, preferred_element_type=F32)

    def kv_rows(j):
        start = j * tk
        return pl.ds(start if isinstance(j, int) else pl.multiple_of(start, tk), tk)

    def put_scores(slot, k):
        width = k.shape[0]
        s = scores(k)
        s_ref[slot][:, 0:width] = s
        mx = s[:, 0:LANES]
        for c0 in range(LANES, width, LANES):
            mx = jnp.maximum(mx, s[:, c0:c0 + LANES])
        mx_ref[slot][...] = mx

    def softmax_stage(slot, width, first):
        for r0 in range(0, r, rc):
            rows = slice(r0, r0 + rc)
            row_max = jnp.max(mx_ref[slot][rows, :], axis=-1, keepdims=True)
            if first:
                m_new = jnp.broadcast_to(row_max, (rc, LANES))
            else:
                m_prev = m_ref[rows, :]
                m_new = jnp.maximum(m_prev, row_max)
                alpha_ref[slot][rows, :] = jnp.exp2(m_prev - m_new)
            m_ref[rows, :] = m_new
            for c0 in range(0, width, LANES):
                cols = slice(c0, c0 + LANES)
                p_ref[slot][rows, cols] = jnp.exp2(s_ref[slot][rows, cols] - m_new).astype(BF16)

    def accumulate(slot, v):
        pv = times_values(p_ref[slot][...], v)
        for r0 in range(0, r, rc):
            rows = slice(r0, r0 + rc)
            a = alpha_ref[slot][rows, :]
            for c0 in (0, LANES):
                cols = slice(c0, c0 + LANES)
                acc_ref[rows, cols] = a * acc_ref[rows, cols] + pv[rows, cols]

    def stage(j, slot, last):
        if not last:
            put_scores(1 - slot, k_ref[kv_rows(j + 1), :])
        softmax_stage(slot, tk, False)
        accumulate(slot, v_ref[kv_rows(j), :])

    put_scores(0, kc_ref[...])
    softmax_stage(0, ctx, True)
    acc_ref[...] = times_values(p_ref[0][:, 0:ctx], vc_ref[...])

    @pl.when(qi < n_lat_tiles)
    def _():
        put_scores(0, k_ref[kv_rows(0), :])

        def pair(t, carry):
            stage(2 * t, 0, False)
            stage(2 * t + 1, 1, False)
            return carry

        lax.fori_loop(0, (nkv - 2) // 2, pair, 0)
        stage(nkv - 2, 0, False)
        stage(nkv - 1, 1, True)

    for g in range(GROUP):
        rows = slice(g * tq, (g + 1) * tq)
        o = acc_ref[rows, 0:HEAD_DIM] / acc_ref[rows, HEAD_DIM:2 * HEAD_DIM]
        o_ref[:, g * HEAD_DIM:(g + 1) * HEAD_DIM] = o.astype(o_ref.dtype)


def flash_attention(q, k, v_ext, seq, ctx, batch, tq, tk):
    rows = q.shape[0]
    n_q_heads = GROUP * N_KV_HEADS
    per_batch_q = seq // tq
    n_lat_tiles = batch * per_batch_q
    n_ctx_tiles = batch * (ctx // tq)
    nkv = seq // tk
    assert nkv >= 2 and nkv % 2 == 0 and ctx <= tk and ctx % LANES == 0
    ctx_blk0 = (batch * seq) // ctx
    r = GROUP * tq

    def batch_of(qi):
        return jnp.where(qi < n_lat_tiles, qi // per_batch_q, (qi - n_lat_tiles) // (ctx // tq))

    grid = (N_KV_HEADS, n_lat_tiles + n_ctx_tiles)
    return pl.pallas_call(
        functools.partial(_flash_kernel, n_lat_tiles=n_lat_tiles, nkv=nkv, tq=tq, tk=tk, ctx=ctx),
        grid=grid,
        in_specs=[
            pl.BlockSpec((tq, GROUP * HEAD_DIM), lambda h, qi: (qi, h)),
            pl.BlockSpec((seq, HEAD_DIM), lambda h, qi: (batch_of(qi), h)),
            pl.BlockSpec((seq, 2 * HEAD_DIM), lambda h, qi: (batch_of(qi), h)),
            pl.BlockSpec((ctx, HEAD_DIM), lambda h, qi: (ctx_blk0 + batch_of(qi), h)),
            pl.BlockSpec((ctx, 2 * HEAD_DIM), lambda h, qi: (ctx_blk0 + batch_of(qi), h)),
        ],
        out_specs=pl.BlockSpec((tq, GROUP * HEAD_DIM), lambda h, qi: (qi, h)),
        out_shape=jax.ShapeDtypeStruct((rows, n_q_heads * HEAD_DIM), BF16),
        scratch_shapes=[pltpu.VMEM((r, LANES), F32), pltpu.VMEM((r, 2 * HEAD_DIM), F32),
                        pltpu.VMEM((r, tk), F32), pltpu.VMEM((r, tk), F32),
                        pltpu.VMEM((r, tk), BF16), pltpu.VMEM((r, tk), BF16),
                        pltpu.VMEM((r, LANES), F32), pltpu.VMEM((r, LANES), F32),
                        pltpu.VMEM((r, LANES), F32), pltpu.VMEM((r, LANES), F32)],
        compiler_params=_cparams("arbitrary", "arbitrary"),
        name="flash_attention",
    )(q, k, v_ext, k, v_ext)


def _scan_constants(reverse, n_chunks):
    t = np.arange(CHUNK)[:, None]
    u = np.arange(CHUNK)[None, :]
    tri = (u >= t) if reverse else (u <= t)
    level_of = np.where(t == u, 0, -1)
    for lvl in range(N_LEVELS):
        b = CHUNK >> (lvl + 1)
        same = (t & ~(2 * b - 1)) == (u & ~(2 * b - 1))
        t_hi, s_hi = (t & b) != 0, (u & b) != 0
        level_of = np.where(same & (~t_hi & s_hi if reverse else t_hi & ~s_hi), lvl + 1, level_of)
    return jnp.asarray(np.kron(np.eye(n_chunks), tri), BF16), jnp.asarray(level_of, jnp.int32)


def _keep_bf16_bits(x):
    bits = lax.bitcast_convert_type(x, jnp.int32) & jnp.int32(-65536)
    return lax.bitcast_convert_type(bits, F32)


def _block_rows(x, block, off):
    rows, width = x.shape
    if block >= 8:
        parts = [jnp.broadcast_to(x[base + off:base + off + 1], (block, width)) for base in range(0, rows, block)]
        return parts[0] if len(parts) == 1 else jnp.concatenate(parts, axis=0)
    within = lax.broadcasted_iota(jnp.int32, x.shape, 0) & (block - 1)
    out = x
    for r in range(block):
        if r != off:
            out = jnp.where(within == r, pltpu.roll(x, (r - off) % rows, axis=0), out)
    return out


def _scan_kernel(q_ref, v_ref, f_ref, lb_ref, tri_ref, lvl_ref, o_ref,
                 st_ref, q32_ref, k32_ref, cb_ref, ql_ref, kl_ref, oi_ref, kv_ref, a_ref, *, reverse, n_chunks, hb):
    step = pl.program_id(2)

    @pl.when(step == 0)
    def _():
        st_ref[...] = jnp.zeros(st_ref.shape, F32)

    width = hb * REC_DK
    nt = (((1,), (1,)), ((), ()))
    tn = (((0,), (0,)), ((), ()))

    lb = lb_ref[...]
    qr = q_ref[...].astype(F32)
    q = (qr * (0.5 * REC_DK ** -0.5)) * (1.0 + jnp.tanh(0.5 * qr))
    fg = (0.5 + 0.5 * lb) + (0.5 - 0.5 * lb) * jnp.tanh(0.5 * f_ref[...].astype(F32))
    k = 1.0 - fg
    lf = jnp.log2(fg)
    hi = _keep_bf16_bits(lf)
    rest = lf - hi
    mid = _keep_bf16_bits(rest)
    lo = rest - mid
    sums = jnp.dot(tri_ref[...], jnp.concatenate([hi.astype(BF16), mid.astype(BF16), lo.astype(BF16)], axis=1),
                   preferred_element_type=F32)
    q32_ref[...] = q
    k32_ref[...] = k
    cb_ref[...] = sums[:, 0:width] + sums[:, width:2 * width] + sums[:, 2 * width:3 * width]
    ql_ref[0] = q.astype(BF16)
    kl_ref[0] = k.astype(BF16)

    tiles = [(c * hb + hh, slice(c * CHUNK, (c + 1) * CHUNK), slice(hh * REC_DK, (hh + 1) * REC_DK))
             for c in range(n_chunks) for hh in range(hb)]

    total = _block_rows(cb_ref[...], CHUNK, 0 if reverse else CHUNK - 1)
    qd = (q32_ref[...] * jnp.exp2(cb_ref[...])).astype(BF16)
    kd = (k32_ref[...] * jnp.exp2(total - cb_ref[...])).astype(BF16)
    for idx, rows, cols in tiles:
        kv_ref[idx] = lax.dot_general(v_ref[rows, cols], kd[rows, cols], tn, preferred_element_type=F32)

    level_of = lvl_ref[...]
    for lvl in range(N_LEVELS + 1):
        if lvl > 0:
            b = CHUNK >> lvl
            d = cb_ref[...] - _block_rows(cb_ref[...], 2 * b, b if reverse else b - 1)
            ql_ref[lvl] = (q32_ref[...] * jnp.exp2(jnp.minimum(d, 0.0))).astype(BF16)
            kl_ref[lvl] = (k32_ref[...] * jnp.exp2(jnp.minimum(-d, 0.0))).astype(BF16)
        for idx, rows, cols in tiles:
            pairs = lax.dot_general(ql_ref[lvl, rows, cols], kl_ref[lvl, rows, cols], nt,
                                    preferred_element_type=F32)
            a_ref[idx] = jnp.where(level_of == lvl, pairs, 0.0 if lvl == 0 else a_ref[idx])

    for idx, rows, cols in tiles:
        oi_ref[rows, cols] = jnp.dot(a_ref[idx].astype(BF16), v_ref[rows, cols], preferred_element_type=F32)

    decay = jnp.exp2(total)
    for c in (range(n_chunks - 1, -1, -1) if reverse else range(n_chunks)):
        rows = slice(c * CHUNK, (c + 1) * CHUNK)
        for hh in range(hb):
            cols = slice(hh * REC_DK, (hh + 1) * REC_DK)
            st = st_ref[hh]
            o = oi_ref[rows, cols] + lax.dot_general(qd[rows, cols], st.astype(BF16), nt,
                                                     preferred_element_type=F32)
            o_ref[rows, cols] = o.astype(o_ref.dtype)
            st_ref[hh] = st * decay[c * CHUNK:c * CHUNK + 1, cols] + kv_ref[c * hb + hh]


def hgrn_scan(proj, lb_dir, f_sec, reverse, seq, ctx, batch, n_heads, hb=4):
    rows = proj.shape[0]
    tb = ctx
    per_batch = seq // tb
    ctx_blk0 = (batch * seq) // tb
    groups = n_heads // hb
    width = hb * REC_DK
    n_chunks = tb // CHUNK
    tri, level_of = _scan_constants(reverse, n_chunks)

    def row_blk(b, s):
        lat = b * per_batch + (per_batch - s if reverse else s - 1)
        return jnp.where(s == 0, ctx_blk0 + b, lat)

    return pl.pallas_call(
        functools.partial(_scan_kernel, reverse=reverse, n_chunks=n_chunks, hb=hb),
        grid=(batch, groups, 1 + per_batch),
        in_specs=[
            pl.BlockSpec((tb, width), lambda b, h, s: (row_blk(b, s), h)),
            pl.BlockSpec((tb, width), lambda b, h, s: (row_blk(b, s), groups + h)),
            pl.BlockSpec((tb, width), lambda b, h, s: (row_blk(b, s), f_sec * groups + h)),
            pl.BlockSpec((None, 1, width), lambda b, h, s: (h, 0, 0)),
            pl.BlockSpec(tri.shape, lambda b, h, s: (0, 0)),
            pl.BlockSpec(level_of.shape, lambda b, h, s: (0, 0)),
        ],
        out_specs=pl.BlockSpec((tb, width), lambda b, h, s: (row_blk(b, s), h)),
        out_shape=jax.ShapeDtypeStruct((rows, n_heads * REC_DK), BF16),
        scratch_shapes=[pltpu.VMEM((hb, REC_DK, REC_DK), F32),
                        pltpu.VMEM((tb, width), F32), pltpu.VMEM((tb, width), F32), pltpu.VMEM((tb, width), F32),
                        pltpu.VMEM((N_LEVELS + 1, tb, width), BF16), pltpu.VMEM((N_LEVELS + 1, tb, width), BF16),
                        pltpu.VMEM((tb, width), F32), pltpu.VMEM((n_chunks * hb, REC_DK, REC_DK), F32),
                        pltpu.VMEM((n_chunks * hb, CHUNK, CHUNK), F32)],
        compiler_params=_cparams("arbitrary", "arbitrary", "arbitrary"),
        name="hgrn_scan_bw" if reverse else "hgrn_scan_fw",
    )(proj, proj, proj, lb_dir.reshape(groups, 1, width), tri, level_of)


def _readout_kernel(of_ref, ob_ref, g_ref, gain_ref, o_ref, *, n_heads):
    gain = gain_ref[...]
    for hh in range(n_heads):
        sl = slice(hh * REC_DK, (hh + 1) * REC_DK)
        o = of_ref[:, sl].astype(F32) + ob_ref[:, sl].astype(F32)
        ms = jnp.mean(o * o, axis=-1, keepdims=True)
        g = g_ref[:, sl].astype(F32)
        o_ref[:, sl] = (o * lax.rsqrt(ms + EPS) * gain * (g * jax.nn.sigmoid(g))).astype(o_ref.dtype)


def hgrn_readout(o_fw, o_bw, proj, o_gain, gate_sec, n_heads):
    rows, d = o_fw.shape
    tm = ROW_TILE
    return pl.pallas_call(
        functools.partial(_readout_kernel, n_heads=n_heads),
        grid=(rows // tm,),
        in_specs=[pl.BlockSpec((tm, d), lambda i: (i, 0)),
                  pl.BlockSpec((tm, d), lambda i: (i, 0)),
                  pl.BlockSpec((tm, d), lambda i: (i, gate_sec)),
                  pl.BlockSpec((1, REC_DK), lambda i: (0, 0))],
        out_specs=pl.BlockSpec((tm, d), lambda i: (i, 0)),
        out_shape=jax.ShapeDtypeStruct((rows, d), BF16),
        compiler_params=_cparams("arbitrary"),
        name="hgrn_readout",
    )(o_fw, o_bw, proj, o_gain.reshape(1, REC_DK))


def _router_kernel(y_ref, w_ref, o_ref):
    logits = jnp.dot(y_ref[...], w_ref[...], preferred_element_type=F32, precision=lax.Precision.HIGHEST)
    lane = lax.broadcasted_iota(jnp.int32, logits.shape, 1)
    neg = -jnp.inf
    logits = jnp.where(lane < N_EXPERTS, logits, neg)
    m1 = jnp.max(logits, axis=-1, keepdims=True)
    i1 = jnp.min(jnp.where(logits == m1, lane, LANES), axis=-1, keepdims=True)
    rest = jnp.where(lane == i1, neg, logits)
    m2 = jnp.max(rest, axis=-1, keepdims=True)
    i2 = jnp.min(jnp.where(rest == m2, lane, LANES), axis=-1, keepdims=True)
    e = jnp.exp(m2 - m1)
    w1 = 1.0 / (1.0 + e)
    w2 = e / (1.0 + e)
    out = jnp.where(lane == 0, i1.astype(F32),
                    jnp.where(lane == 1, i2.astype(F32),
                              jnp.where(lane == 2, w1, jnp.where(lane == 3, w2, 0.0))))
    o_ref[...] = out


def moe_router(y, w_router):
    rows, d = y.shape
    tm = ROW_TILE
    w_pad = jnp.zeros((d, LANES), F32).at[:, :N_EXPERTS].set(w_router)
    return pl.pallas_call(
        _router_kernel,
        grid=(rows // tm,),
        in_specs=[pl.BlockSpec((tm, d), lambda i: (i, 0)), pl.BlockSpec((d, LANES), lambda i: (0, 0))],
        out_specs=pl.BlockSpec((tm, LANES), lambda i: (i, 0)),
        out_shape=jax.ShapeDtypeStruct((rows, LANES), F32),
        compiler_params=_cparams("arbitrary"),
        name="moe_router",
    )(y, w_pad)


def _row_copy(src_ref, dst_ref, sem, src_row, dst_row):
    return pltpu.make_async_copy(src_ref.at[pl.ds(src_row, 1), :], dst_ref.at[pl.ds(dst_row, 1), :], sem)


def _start_row_gather(idx_ref, src_ref, dst_ref, sem):
    def start(i, carry):
        for prio in range(2):
            r = 2 * i + prio
            _row_copy(src_ref, dst_ref, sem, idx_ref[0, r], r).start(priority=prio)
        return carry

    lax.fori_loop(0, dst_ref.shape[0] // 2, start, 0, unroll=4)


def _wait_row_gather(src_ref, dst_ref, sem):
    def wait(r, carry):
        _row_copy(src_ref, dst_ref, sem, 0, r).wait()
        return carry

    lax.fori_loop(0, dst_ref.shape[0], wait, 0, unroll=8)


def _gather_rows_into(idx_ref, src_ref, dst_ref, sem):
    _start_row_gather(idx_ref, src_ref, dst_ref, sem)
    _wait_row_gather(src_ref, dst_ref, sem)


def _expert_up_kernel(te_ref, ta_ref, idx_ref, idx_next_ref, y_ref, w1_ref, w3_ref, o_ref, x_ref, sem, *, n_tiles):
    i = pl.program_id(0)
    j = pl.program_id(1)
    slot = i % 2

    @pl.when(j == 0)
    def _():
        @pl.when(i == 0)
        def _():
            _start_row_gather(idx_ref, y_ref, x_ref.at[0], sem.at[0])

        @pl.when(i + 1 < n_tiles)
        def _():
            _start_row_gather(idx_next_ref, y_ref, x_ref.at[1 - slot], sem.at[1 - slot])

        _wait_row_gather(y_ref, x_ref.at[slot], sem.at[slot])

    @pl.when(ta_ref[i] > 0)
    def _():
        x = x_ref[slot].astype(BF16)
        u = jnp.dot(x, w1_ref[...], preferred_element_type=F32)
        g = jnp.dot(x, w3_ref[...], preferred_element_type=F32)
        o_ref[...] = (u * jax.nn.sigmoid(u) * g).astype(o_ref.dtype)

    @pl.when(ta_ref[i] == 0)
    def _():
        o_ref[...] = jnp.zeros(o_ref.shape, o_ref.dtype)


def expert_up(y, token_of_slot, w1, w3, tile_expert, tile_active, tn=1024):
    p = token_of_slot.shape[0]
    d = y.shape[1]
    f = w1.shape[2]
    tm = ROW_TILE
    n_tiles = p // tm
    idx = token_of_slot.reshape(n_tiles, 1, tm)
    return pl.pallas_call(
        functools.partial(_expert_up_kernel, n_tiles=n_tiles),
        grid_spec=pltpu.PrefetchScalarGridSpec(
            num_scalar_prefetch=2,
            grid=(n_tiles, f // tn),
            in_specs=[pl.BlockSpec((None, 1, tm), lambda i, j, te, ta: (i, 0, 0), memory_space=pltpu.SMEM),
                      pl.BlockSpec((None, 1, tm), lambda i, j, te, ta: (jnp.minimum(i + 1, n_tiles - 1), 0, 0),
                                   memory_space=pltpu.SMEM),
                      pl.BlockSpec(memory_space=pl.ANY),
                      pl.BlockSpec((None, d, tn), lambda i, j, te, ta: (te[i], 0, j)),
                      pl.BlockSpec((None, d, tn), lambda i, j, te, ta: (te[i], 0, j))],
            out_specs=pl.BlockSpec((tm, tn), lambda i, j, te, ta: (i, j)),
            scratch_shapes=[pltpu.VMEM((2, tm, d), y.dtype), pltpu.SemaphoreType.DMA((2,))],
        ),
        out_shape=jax.ShapeDtypeStruct((p, f), BF16),
        compiler_params=_cparams("arbitrary", "arbitrary"),
        name="moe_expert_up",
    )(tile_expert, tile_active, idx, idx, y, w1, w3)


def _expert_down_kernel(te_ref, ta_ref, a_ref, w_ref, o_ref, acc_ref, *, nk):
    i = pl.program_id(0)
    kk = pl.program_id(1)

    @pl.when(ta_ref[i] > 0)
    def _():
        part = jnp.dot(a_ref[...], w_ref[...], preferred_element_type=F32)

        @pl.when(kk == 0)
        def _():
            acc_ref[...] = part

        @pl.when(jnp.logical_and(kk > 0, kk < nk - 1))
        def _():
            acc_ref[...] += part

        @pl.when(kk == nk - 1)
        def _():
            o_ref[...] = acc_ref[...] + part

    @pl.when(jnp.logical_and(ta_ref[i] == 0, kk == nk - 1))
    def _():
        o_ref[...] = jnp.zeros(o_ref.shape, o_ref.dtype)


def expert_down(a, w2, tile_expert, tile_active, tk=2048):
    p, f = a.shape
    d = w2.shape[2]
    tm = ROW_TILE
    nk = f // tk
    assert nk >= 2
    return pl.pallas_call(
        functools.partial(_expert_down_kernel, nk=nk),
        grid_spec=pltpu.PrefetchScalarGridSpec(
            num_scalar_prefetch=2,
            grid=(p // tm, nk),
            in_specs=[pl.BlockSpec((tm, tk), lambda i, kk, te, ta: (i, kk)),
                      pl.BlockSpec((None, tk, d), lambda i, kk, te, ta: (te[i], kk, 0))],
            out_specs=pl.BlockSpec((tm, d), lambda i, kk, te, ta: (i, 0)),
            scratch_shapes=[pltpu.VMEM((tm, d), F32)],
        ),
        out_shape=jax.ShapeDtypeStruct((p, d), F32),
        compiler_params=_cparams("arbitrary", "arbitrary"),
        name="moe_expert_down",
    )(tile_expert, tile_active, a, w2)


def _combine_kernel(s0_ref, s1_ref, src_ref, route_ref, h_ref, gate_ref, gain_ref, o_ref, buf0, buf1, sem):
    _gather_rows_into(s0_ref, src_ref, buf0, sem.at[0])
    _gather_rows_into(s1_ref, src_ref, buf1, sem.at[1])
    w0 = route_ref[:, TOP_K:TOP_K + 1]
    w1 = route_ref[:, TOP_K + 1:TOP_K + 2]
    o_ref[...] = _residual_epilogue(w0 * buf0[...] + w1 * buf1[...], h_ref[...], gate_ref[...], gain_ref[...])


def moe_combine_resnorm(expert_out, slot0, slot1, route, h, gain, mods, gate_sec, seq, batch, out_rows, tile=256):
    rows, d = h.shape
    row = functools.partial(_mod_row, tm=tile, seq=seq, batch=batch)
    idx_spec = pl.BlockSpec((None, 1, tile), lambda i: (i, 0, 0), memory_space=pltpu.SMEM)
    return pl.pallas_call(
        _combine_kernel,
        grid=(out_rows // tile,),
        in_specs=[idx_spec, idx_spec,
                  pl.BlockSpec(memory_space=pl.ANY),
                  pl.BlockSpec((tile, LANES), lambda i: (i, 0)),
                  pl.BlockSpec((tile, d), lambda i: (i, 0)),
                  pl.BlockSpec((None, 1, d), lambda i: (row(i), 0, gate_sec)),
                  pl.BlockSpec((1, d), lambda i: (0, 0))],
        out_specs=pl.BlockSpec((tile, d), lambda i: (i, 0)),
        out_shape=jax.ShapeDtypeStruct((out_rows, d), F32),
        scratch_shapes=[pltpu.VMEM((tile, d), F32), pltpu.VMEM((tile, d), F32), pltpu.SemaphoreType.DMA((2,))],
        compiler_params=_cparams("arbitrary"),
        name="moe_combine_resnorm",
    )(slot0.reshape(rows // tile, 1, tile), slot1.reshape(rows // tile, 1, tile), expert_out, route, h, mods,
      gain.reshape(1, d))


def _routing_plan(route, tm):
    rows = route.shape[0]
    experts = route[:, :TOP_K].astype(jnp.int32).reshape(-1)
    onehot = (experts[:, None] == jnp.arange(N_EXPERTS)[None, :]).astype(jnp.int32)
    rank = jnp.sum((jnp.cumsum(onehot, axis=0) - onehot) * onehot, axis=1)
    counts = jnp.sum(onehot, axis=0)
    tiles_per = (counts + tm - 1) // tm
    tile_end = jnp.cumsum(tiles_per)
    start = (tile_end - tiles_per) * tm
    slot = jnp.sum(onehot * start[None, :], axis=1) + rank
    n_slots = rows * TOP_K + N_EXPERTS * tm
    n_tiles = n_slots // tm
    token_of_slot = jnp.zeros((n_slots,), jnp.int32).at[slot].set(jnp.arange(rows * TOP_K, dtype=jnp.int32) // TOP_K)
    tile_ids = jnp.arange(n_tiles, dtype=jnp.int32)
    tile_expert = jnp.minimum(jnp.sum((tile_ids[:, None] >= tile_end[None, :]).astype(jnp.int32), axis=1),
                              N_EXPERTS - 1)
    tile_active = (tile_ids < tile_end[-1]).astype(jnp.int32)
    slots = slot.reshape(rows, TOP_K).astype(jnp.int32)
    return token_of_slot, tile_expert, tile_active, slots[:, 0], slots[:, 1]


def moe_ffn_resnorm(y, h, w_router, w1, w3, w2, gain, mods, gate_sec, seq, batch, out_rows):
    route = moe_router(y, w_router)
    token_of_slot, tile_expert, tile_active, slot0, slot1 = _routing_plan(route, ROW_TILE)
    hid = expert_up(y, token_of_slot, w1, w3, tile_expert, tile_active)
    out = expert_down(hid, w2, tile_expert, tile_active)
    return moe_combine_resnorm(out, slot0, slot1, route, h, gain, mods, gate_sec, seq, batch, out_rows)


def _lower_bounds(logits):
    p = jax.nn.softmax(logits.astype(F32), axis=0)
    return jnp.cumsum(p, axis=0) - p[0]


def kernel(x, c, ctx, c_ctx, w_mod, b_mod, norm_gains, attn_w_in, attn_w_o, attn_q_gain, attn_k_gain,
           rec_w_in, rec_w_o, rec_lb_logits, rec_o_gain, ffn_w1, ffn_w3, ffn_w2,
           moe_w_router, moe_w1, moe_w3, moe_w2):
    batch, seq, d = x.shape
    ctx_len = ctx.shape[1]
    depth = w_mod.shape[0]
    n_lat = batch * seq
    n_heads = d // REC_DK
    assert seq % ROW_TILE == 0 and (batch * ctx_len) % ROW_TILE == 0 and batch + 1 <= MOD_ROWS

    hg = jnp.concatenate([x.reshape(n_lat, d), ctx.reshape(batch * ctx_len, d)], axis=0)
    cond = jnp.zeros((MOD_ROWS, d), F32).at[:batch].set(c).at[batch].set(c_ctx)
    mods_all = modulation_table(cond, w_mod, b_mod)
    lower = _lower_bounds(rec_lb_logits)
    tables = rope_tables(seq, ROW_TILE)
    tq = min(256, ctx_len)
    tk = min(2048, seq // 4)

    for layer in range(depth):
        j = layer // 2
        ng = norm_gains[layer]
        mods = mods_all[layer]
        if layer % 2 == 0:
            qkv = norm_matmul(hg, ng[0], mods, 0, 1, seq, batch, attn_w_in[j].astype(BF16), BF16, tn=1024)
            q, k, v_ext = qk_prep(qkv, attn_q_gain[j], attn_k_gain[j], tables, seq, n_lat)
            mix = flash_attention(q, k, v_ext, seq, ctx_len, batch, tq, tk)
            hg = matmul_resnorm(mix, attn_w_o[j].astype(BF16), hg, ng[1], mods, 2, seq, batch, tk=d)
        else:
            proj = norm_matmul(hg, ng[0], mods, 0, 1, seq, batch, rec_w_in[j].astype(BF16), BF16, tn=1024)
            o_fw = hgrn_scan(proj, lower[layer, 0], 2, False, seq, ctx_len, batch, n_heads)
            o_bw = hgrn_scan(proj, lower[layer, 1], 3, True, seq, ctx_len, batch, n_heads)
            mix = hgrn_readout(o_fw, o_bw, proj, rec_o_gain[j], 4, n_heads)
            hg = matmul_resnorm(mix, rec_w_o[j].astype(BF16), hg, ng[1], mods, 2, seq, batch, tk=d)
        if layer % 2 == 0:
            hid = norm_swiglu_up(hg, ng[2], mods, 3, 4, seq, batch, ffn_w1[j].astype(BF16), ffn_w3[j].astype(BF16))
            f = hid.shape[1]
            hg = matmul_resnorm(hid, ffn_w2[j].astype(BF16), hg, ng[3], mods, 5, seq, batch, tk=f // 2)
        else:
            y = norm_modulate(hg, ng[2], mods, 3, 4, seq, batch, F32)
            out_rows = n_lat if layer == depth - 1 else hg.shape[0]
            hg = moe_ffn_resnorm(y, hg, moe_w_router[j], moe_w1[j].astype(BF16), moe_w3[j].astype(BF16),
                                 moe_w2[j].astype(BF16), ng[3], mods, 5, seq, batch, out_rows)
    return hg[:n_lat].reshape(batch, seq, d)
```

```python
import functools

import numpy as np
import jax
import jax.numpy as jnp
from jax import lax
from jax.experimental import pallas as pl
from jax.experimental.pallas import tpu as pltpu

F32 = jnp.float32
BF16 = jnp.bfloat16

HEAD_DIM = 128
N_KV_HEADS = 4
GROUP = 4
GRID_W = 64
ROPE_THETA = 10000.0
AXIS_ROT_DIM = HEAD_DIM // 2
REC_DK = 128
N_EXPERTS = 8
TOP_K = 2
N_MOD = 6
EPS = 1e-6
CHUNK = 64
N_LEVELS = 6

LANES = 128
VMEM_LIMIT = 56 * 2**20

ROW_TILE = 512
MOD_ROWS = 8


def _cparams(*sem):
    return pltpu.CompilerParams(dimension_semantics=sem, vmem_limit_bytes=VMEM_LIMIT)


def _mod_row(i, tm, seq, batch):
    return jnp.minimum((i * tm) // seq, batch)


def _mod_kernel(c_ref, w_ref, b_ref, o_ref):
    c = c_ref[...]
    s = (c * jax.nn.sigmoid(c)).astype(BF16)
    o_ref[...] = jnp.dot(s, w_ref[...].astype(BF16), preferred_element_type=F32) + b_ref[...]


def modulation_table(cond, w_mod, b_mod, tn=1024):
    depth, d, n = w_mod.shape
    out = pl.pallas_call(
        _mod_kernel,
        grid=(depth, n // tn),
        in_specs=[
            pl.BlockSpec((MOD_ROWS, d), lambda l, j: (0, 0)),
            pl.BlockSpec((None, d, tn), lambda l, j: (l, 0, j)),
            pl.BlockSpec((None, 1, tn), lambda l, j: (l, 0, j)),
        ],
        out_specs=pl.BlockSpec((None, MOD_ROWS, tn), lambda l, j: (l, 0, j)),
        out_shape=jax.ShapeDtypeStruct((depth, MOD_ROWS, n), F32),
        compiler_params=_cparams("arbitrary", "arbitrary"),
        name="modulation_table",
    )(cond, w_mod, b_mod.reshape(depth, 1, n))
    return out.reshape(depth, MOD_ROWS, 1, n)


def _norm_mod_kernel(h_ref, g_ref, sh_ref, sc_ref, o_ref):
    x = h_ref[...]
    ms = jnp.mean(x * x, axis=-1, keepdims=True)
    y = x * lax.rsqrt(ms + EPS) * g_ref[...]
    o_ref[...] = (y * (1.0 + sc_ref[...]) + sh_ref[...]).astype(o_ref.dtype)


def norm_modulate(h, gain, mods, shift_sec, scale_sec, seq, batch, out_dtype):
    rows, d = h.shape
    tm = ROW_TILE
    row = functools.partial(_mod_row, tm=tm, seq=seq, batch=batch)
    return pl.pallas_call(
        _norm_mod_kernel,
        grid=(rows // tm,),
        in_specs=[
            pl.BlockSpec((tm, d), lambda i: (i, 0)),
            pl.BlockSpec((1, d), lambda i: (0, 0)),
            pl.BlockSpec((None, 1, d), lambda i: (row(i), 0, shift_sec)),
            pl.BlockSpec((None, 1, d), lambda i: (row(i), 0, scale_sec)),
        ],
        out_specs=pl.BlockSpec((tm, d), lambda i: (i, 0)),
        out_shape=jax.ShapeDtypeStruct((rows, d), out_dtype),
        compiler_params=_cparams("arbitrary"),
        name="norm_modulate",
    )(h, gain.reshape(1, d), mods, mods)


def _normed_lhs(h_ref, g_ref, sh_ref, sc_ref, y_ref):
    @pl.when(pl.program_id(1) == 0)
    def _():
        _norm_mod_kernel(h_ref, g_ref, sh_ref, sc_ref, y_ref)


def _norm_specs(tm, d, row, shift_sec, scale_sec):
    return [pl.BlockSpec((tm, d), lambda i, j: (i, 0)),
            pl.BlockSpec((1, d), lambda i, j: (0, 0)),
            pl.BlockSpec((None, 1, d), lambda i, j: (row(i), 0, shift_sec)),
            pl.BlockSpec((None, 1, d), lambda i, j: (row(i), 0, scale_sec))]


def _norm_mm_kernel(h_ref, g_ref, sh_ref, sc_ref, b_ref, o_ref, y_ref):
    _normed_lhs(h_ref, g_ref, sh_ref, sc_ref, y_ref)
    o_ref[...] = jnp.dot(y_ref[...], b_ref[...], preferred_element_type=F32).astype(o_ref.dtype)


def norm_matmul(h, gain, mods, shift_sec, scale_sec, seq, batch, b, out_dtype, tn):
    m, d = h.shape
    n = b.shape[1]
    tm = ROW_TILE
    row = functools.partial(_mod_row, tm=tm, seq=seq, batch=batch)
    return pl.pallas_call(
        _norm_mm_kernel,
        grid=(m // tm, n // tn),
        in_specs=_norm_specs(tm, d, row, shift_sec, scale_sec) + [pl.BlockSpec((d, tn), lambda i, j: (0, j))],
        out_specs=pl.BlockSpec((tm, tn), lambda i, j: (i, j)),
        out_shape=jax.ShapeDtypeStruct((m, n), out_dtype),
        scratch_shapes=[pltpu.VMEM((tm, d), BF16)],
        compiler_params=_cparams("arbitrary", "arbitrary"),
        name="norm_matmul",
    )(h, gain.reshape(1, d), mods, mods, b)


def _norm_swiglu_up_kernel(h_ref, g_ref, sh_ref, sc_ref, w1_ref, w3_ref, o_ref, y_ref):
    _normed_lhs(h_ref, g_ref, sh_ref, sc_ref, y_ref)
    a = y_ref[...]
    u = jnp.dot(a, w1_ref[...], preferred_element_type=F32)
    g = jnp.dot(a, w3_ref[...], preferred_element_type=F32)
    o_ref[...] = (u * jax.nn.sigmoid(u) * g).astype(o_ref.dtype)


def norm_swiglu_up(h, gain, mods, shift_sec, scale_sec, seq, batch, w1, w3, tn=512):
    m, d = h.shape
    n = w1.shape[1]
    tm = ROW_TILE
    row = functools.partial(_mod_row, tm=tm, seq=seq, batch=batch)
    return pl.pallas_call(
        _norm_swiglu_up_kernel,
        grid=(m // tm, n // tn),
        in_specs=_norm_specs(tm, d, row, shift_sec, scale_sec) + [pl.BlockSpec((d, tn), lambda i, j: (0, j)),
                                                                  pl.BlockSpec((d, tn), lambda i, j: (0, j))],
        out_specs=pl.BlockSpec((tm, tn), lambda i, j: (i, j)),
        out_shape=jax.ShapeDtypeStruct((m, n), BF16),
        scratch_shapes=[pltpu.VMEM((tm, d), BF16)],
        compiler_params=_cparams("arbitrary", "arbitrary"),
        name="norm_swiglu_up",
    )(h, gain.reshape(1, d), mods, mods, w1, w3)


def _residual_epilogue(r, h, gate, gain):
    ms = jnp.mean(r * r, axis=-1, keepdims=True)
    return h + gate * (r * lax.rsqrt(ms + EPS) * gain)


def _mm_resnorm_kernel(a_ref, b_ref, h_ref, gate_ref, gain_ref, o_ref, acc_ref, *, nk):
    kk = pl.program_id(1)
    part = jnp.dot(a_ref[...], b_ref[...], preferred_element_type=F32)

    if nk == 1:
        o_ref[...] = _residual_epilogue(part, h_ref[...], gate_ref[...], gain_ref[...])
        return

    @pl.when(kk == 0)
    def _():
        acc_ref[...] = part

    @pl.when(kk > 0)
    def _():
        acc_ref[...] += part

    @pl.when(kk == nk - 1)
    def _():
        o_ref[...] = _residual_epilogue(acc_ref[...], h_ref[...], gate_ref[...], gain_ref[...])


def matmul_resnorm(a, b, h, gain, mods, gate_sec, seq, batch, tk):
    m, k = a.shape
    d = b.shape[1]
    tm = ROW_TILE
    nk = k // tk
    row = functools.partial(_mod_row, tm=tm, seq=seq, batch=batch)
    return pl.pallas_call(
        functools.partial(_mm_resnorm_kernel, nk=nk),
        grid=(m // tm, nk),
        in_specs=[
            pl.BlockSpec((tm, tk), lambda i, kk: (i, kk)),
            pl.BlockSpec((tk, d), lambda i, kk: (kk, 0)),
            pl.BlockSpec((tm, d), lambda i, kk: (i, 0)),
            pl.BlockSpec((None, 1, d), lambda i, kk: (row(i), 0, gate_sec)),
            pl.BlockSpec((1, d), lambda i, kk: (0, 0)),
        ],
        out_specs=pl.BlockSpec((tm, d), lambda i, kk: (i, 0)),
        out_shape=jax.ShapeDtypeStruct((m, d), F32),
        scratch_shapes=[pltpu.VMEM((tm, d), F32)],
        compiler_params=_cparams("arbitrary", "arbitrary"),
        name="matmul_resnorm",
    )(a, b, h, mods, gain.reshape(1, d))


def rope_tables(seq, tile):
    rows = seq // GRID_W
    row = jnp.repeat(jnp.arange(rows, dtype=F32), GRID_W)
    col = jnp.tile(jnp.arange(GRID_W, dtype=F32), rows)
    inv_freq = ROPE_THETA ** (-jnp.arange(0, AXIS_ROT_DIM, 2, dtype=F32) / AXIS_ROT_DIM)
    ar, ac = row[:, None] * inv_freq, col[:, None] * inv_freq
    zeros = jnp.zeros_like(ar)
    cos = jnp.concatenate([jnp.cos(ar), jnp.cos(ar), jnp.cos(ac), jnp.cos(ac)], axis=-1)
    sin_lo = jnp.concatenate([-jnp.sin(ar), zeros, -jnp.sin(ac), zeros], axis=-1)
    sin_hi = jnp.concatenate([zeros, jnp.sin(ar), zeros, jnp.sin(ac)], axis=-1)
    pad0 = jnp.zeros((tile, HEAD_DIM), F32)
    return (jnp.concatenate([cos, jnp.ones((tile, HEAD_DIM), F32)], axis=0),
            jnp.concatenate([sin_lo, pad0], axis=0),
            jnp.concatenate([sin_hi, pad0], axis=0))


def _qk_prep_kernel(qkv_ref, qg_ref, kg_ref, cos_ref, slo_ref, shi_ref, q_ref, k_ref, v_ref, *, n_q, n_k, q_scale):
    cos, slo, shi = cos_ref[...], slo_ref[...], shi_ref[...]
    quarter = AXIS_ROT_DIM // 2

    def head(x, gain):
        x = x.astype(F32)
        ms = jnp.mean(x * x, axis=-1, keepdims=True)
        xn = x * lax.rsqrt(ms + EPS) * gain
        return (xn * cos + pltpu.roll(xn, HEAD_DIM - quarter, axis=1) * slo
                + pltpu.roll(xn, quarter, axis=1) * shi)

    for hh in range(n_q):
        sl = slice(hh * HEAD_DIM, (hh + 1) * HEAD_DIM)
        q_ref[:, sl] = (head(qkv_ref[:, sl], qg_ref[...]) * q_scale).astype(q_ref.dtype)
    for hh in range(n_k):
        src = slice((n_q + hh) * HEAD_DIM, (n_q + hh + 1) * HEAD_DIM)
        k_ref[:, hh * HEAD_DIM:(hh + 1) * HEAD_DIM] = head(qkv_ref[:, src], kg_ref[...]).astype(k_ref.dtype)
    for hh in range(n_k):
        src = slice((n_q + n_k + hh) * HEAD_DIM, (n_q + n_k + hh + 1) * HEAD_DIM)
        v_ref[:, 2 * hh * HEAD_DIM:(2 * hh + 1) * HEAD_DIM] = qkv_ref[:, src]
        v_ref[:, (2 * hh + 1) * HEAD_DIM:(2 * hh + 2) * HEAD_DIM] = jnp.ones((v_ref.shape[0], HEAD_DIM), v_ref.dtype)


def qk_prep(qkv, q_gain, k_gain, tables, seq, n_lat_rows):
    rows = qkv.shape[0]
    tm = ROW_TILE
    n_q = 4 * N_KV_HEADS
    per_batch = seq // tm
    n_lat_tiles = n_lat_rows // tm
    tab = lambda i: (jnp.where(i < n_lat_tiles, i % per_batch, per_batch), 0)
    q_scale = HEAD_DIM ** -0.5 * float(np.log2(np.e))
    return pl.pallas_call(
        functools.partial(_qk_prep_kernel, n_q=n_q, n_k=N_KV_HEADS, q_scale=q_scale),
        grid=(rows // tm,),
        in_specs=[
            pl.BlockSpec((tm, (n_q + 2 * N_KV_HEADS) * HEAD_DIM), lambda i: (i, 0)),
            pl.BlockSpec((1, HEAD_DIM), lambda i: (0, 0)),
            pl.BlockSpec((1, HEAD_DIM), lambda i: (0, 0)),
            pl.BlockSpec((tm, HEAD_DIM), tab),
            pl.BlockSpec((tm, HEAD_DIM), tab),
            pl.BlockSpec((tm, HEAD_DIM), tab),
        ],
        out_specs=[pl.BlockSpec((tm, n_q * HEAD_DIM), lambda i: (i, 0)),
                   pl.BlockSpec((tm, N_KV_HEADS * HEAD_DIM), lambda i: (i, 0)),
                   pl.BlockSpec((tm, 2 * N_KV_HEADS * HEAD_DIM), lambda i: (i, 0))],
        out_shape=[jax.ShapeDtypeStruct((rows, n_q * HEAD_DIM), BF16),
                   jax.ShapeDtypeStruct((rows, N_KV_HEADS * HEAD_DIM), BF16),
                   jax.ShapeDtypeStruct((rows, 2 * N_KV_HEADS * HEAD_DIM), BF16)],
        compiler_params=_cparams("arbitrary"),
        name="qk_prep",
    )(qkv, q_gain.reshape(1, HEAD_DIM), k_gain.reshape(1, HEAD_DIM), *tables)


def _flash_kernel(q_ref, k_ref, v_ref, kc_ref, vc_ref, o_ref, m_ref, acc_ref,
                  s0_ref, s1_ref, p0_ref, p1_ref, mx0_ref, mx1_ref, alpha0_ref, alpha1_ref, *,
                  n_lat_tiles, nkv, tq, tk, ctx):
    qi = pl.program_id(1)
    r = GROUP * tq
    rc = min(128, r)
    s_ref, p_ref = (s0_ref, s1_ref), (p0_ref, p1_ref)
    mx_ref, alpha_ref = (mx0_ref, mx1_ref), (alpha0_ref, alpha1_ref)
    q = jnp.concatenate([q_ref[:, g * HEAD_DIM:(g + 1) * HEAD_DIM] for g in range(GROUP)], axis=0)
    nt = (((1,), (1,)), ((), ()))

    def scores(k):
        return lax.dot_general(q, k, nt, preferred_element_type=F32)

    def times_values(p, v):
        return jnp.dot(p, v, preferred_element_type=F32)

    def kv_rows(j):
        start = j * tk
        return pl.ds(start if isinstance(j, int) else pl.multiple_of(start, tk), tk)

    def put_scores(slot, k):
        width = k.shape[0]
        s = scores(k)
        s_ref[slot][:, 0:width] = s
        mx = s[:, 0:LANES]
        for c0 in range(LANES, width, LANES):
            mx = jnp.maximum(mx, s[:, c0:c0 + LANES])
        mx_ref[slot][...] = mx

    def softmax_stage(slot, width, first):
        for r0 in range(0, r, rc):
            rows = slice(r0, r0 + rc)
            row_max = jnp.max(mx_ref[slot][rows, :], axis=-1, keepdims=True)
            if first:
                m_new = jnp.broadcast_to(row_max, (rc, LANES))
            else:
                m_prev = m_ref[rows, :]
                m_new = jnp.maximum(m_prev, row_max)
                alpha_ref[slot][rows, :] = jnp.exp2(m_prev - m_new)
            m_ref[rows, :] = m_new
            for c0 in range(0, width, LANES):
                cols = slice(c0, c0 + LANES)
                p_ref[slot][rows, cols] = jnp.exp2(s_ref[slot][rows, cols] - m_new).astype(BF16)

    def accumulate(slot, v):
        pv = times_values(p_ref[slot][...], v)
        for r0 in range(0, r, rc):
            rows = slice(r0, r0 + rc)
            a = alpha_ref[slot][rows, :]
            for c0 in (0, LANES):
                cols = slice(c0, c0 + LANES)
                acc_ref[rows, cols] = a * acc_ref[rows, cols] + pv[rows, cols]

    def stage(j, slot, last):
        if not last:
            put_scores(1 - slot, k_ref[kv_rows(j + 1), :])
        softmax_stage(slot, tk, False)
        accumulate(slot, v_ref[kv_rows(j), :])

    put_scores(0, kc_ref[...])
    softmax_stage(0, ctx, True)
    acc_ref[...] = times_values(p_ref[0][:, 0:ctx], vc_ref[...])

    @pl.when(qi < n_lat_tiles)
    def _():
        put_scores(0, k_ref[kv_rows(0), :])

        def pair(t, carry):
            stage(2 * t, 0, False)
            stage(2 * t + 1, 1, False)
            return carry

        lax.fori_loop(0, (nkv - 2) // 2, pair, 0)
        stage(nkv - 2, 0, False)
        stage(nkv - 1, 1, True)

    for g in range(GROUP):
        rows = slice(g * tq, (g + 1) * tq)
        o = acc_ref[rows, 0:HEAD_DIM] / acc_ref[rows, HEAD_DIM:2 * HEAD_DIM]
        o_ref[:, g * HEAD_DIM:(g + 1) * HEAD_DIM] = o.astype(o_ref.dtype)


def flash_attention(q, k, v_ext, seq, ctx, batch, tq, tk):
    rows = q.shape[0]
    n_q_heads = GROUP * N_KV_HEADS
    per_batch_q = seq // tq
    n_lat_tiles = batch * per_batch_q
    n_ctx_tiles = batch * (ctx // tq)
    nkv = seq // tk
    assert nkv >= 2 and nkv % 2 == 0 and ctx <= tk and ctx % LANES == 0
    ctx_blk0 = (batch * seq) // ctx
    r = GROUP * tq

    def batch_of(qi):
        return jnp.where(qi < n_lat_tiles, qi // per_batch_q, (qi - n_lat_tiles) // (ctx // tq))

    grid = (N_KV_HEADS, n_lat_tiles + n_ctx_tiles)
    return pl.pallas_call(
        functools.partial(_flash_kernel, n_lat_tiles=n_lat_tiles, nkv=nkv, tq=tq, tk=tk, ctx=ctx),
        grid=grid,
        in_specs=[
            pl.BlockSpec((tq, GROUP * HEAD_DIM), lambda h, qi: (qi, h)),
            pl.BlockSpec((seq, HEAD_DIM), lambda h, qi: (batch_of(qi), h)),
            pl.BlockSpec((seq, 2 * HEAD_DIM), lambda h, qi: (batch_of(qi), h)),
            pl.BlockSpec((ctx, HEAD_DIM), lambda h, qi: (ctx_blk0 + batch_of(qi), h)),
            pl.BlockSpec((ctx, 2 * HEAD_DIM), lambda h, qi: (ctx_blk0 + batch_of(qi), h)),
        ],
        out_specs=pl.BlockSpec((tq, GROUP * HEAD_DIM), lambda h, qi: (qi, h)),
        out_shape=jax.ShapeDtypeStruct((rows, n_q_heads * HEAD_DIM), BF16),
        scratch_shapes=[pltpu.VMEM((r, LANES), F32), pltpu.VMEM((r, 2 * HEAD_DIM), F32),
                        pltpu.VMEM((r, tk), F32), pltpu.VMEM((r, tk), F32),
                        pltpu.VMEM((r, tk), BF16), pltpu.VMEM((r, tk), BF16),
                        pltpu.VMEM((r, LANES), F32), pltpu.VMEM((r, LANES), F32),
                        pltpu.VMEM((r, LANES), F32), pltpu.VMEM((r, LANES), F32)],
        compiler_params=_cparams("arbitrary", "arbitrary"),
        name="flash_attention",
    )(q, k, v_ext, k, v_ext)


def _scan_constants(reverse, n_chunks):
    t = np.arange(CHUNK)[:, None]
    u = np.arange(CHUNK)[None, :]
    tri = (u >= t) if reverse else (u <= t)
    level_of = np.where(t == u, 0, -1)
    for lvl in range(N_LEVELS):
        b = CHUNK >> (lvl + 1)
        same = (t & ~(2 * b - 1)) == (u & ~(2 * b - 1))
        t_hi, s_hi = (t & b) != 0, (u & b) != 0
        level_of = np.where(same & (~t_hi & s_hi if reverse else t_hi & ~s_hi), lvl + 1, level_of)
    return jnp.asarray(np.kron(np.eye(n_chunks), tri), BF16), jnp.asarray(level_of, jnp.int32)


def _keep_bf16_bits(x):
    bits = lax.bitcast_convert_type(x, jnp.int32) & jnp.int32(-65536)
    return lax.bitcast_convert_type(bits, F32)


def _block_rows(x, block, off):
    rows, width = x.shape
    if block >= 8:
        parts = [jnp.broadcast_to(x[base + off:base + off + 1], (block, width)) for base in range(0, rows, block)]
        return parts[0] if len(parts) == 1 else jnp.concatenate(parts, axis=0)
    within = lax.broadcasted_iota(jnp.int32, x.shape, 0) & (block - 1)
    out = x
    for r in range(block):
        if r != off:
            out = jnp.where(within == r, pltpu.roll(x, (r - off) % rows, axis=0), out)
    return out


def _scan_kernel(q_ref, v_ref, f_ref, lb_ref, tri_ref, lvl_ref, o_ref,
                 st_ref, q32_ref, k32_ref, cb_ref, ql_ref, kl_ref, oi_ref, kv_ref, a_ref, *, reverse, n_chunks, hb):
    step = pl.program_id(2)

    @pl.when(step == 0)
    def _():
        st_ref[...] = jnp.zeros(st_ref.shape, F32)

    width = hb * REC_DK
    nt = (((1,), (1,)), ((), ()))
    tn = (((0,), (0,)), ((), ()))

    lb = lb_ref[...]
    qr = q_ref[...].astype(F32)
    q = (qr * (0.5 * REC_DK ** -0.5)) * (1.0 + jnp.tanh(0.5 * qr))
    fg = (0.5 + 0.5 * lb) + (0.5 - 0.5 * lb) * jnp.tanh(0.5 * f_ref[...].astype(F32))
    k = 1.0 - fg
    lf = jnp.log2(fg)
    hi = _keep_bf16_bits(lf)
    rest = lf - hi
    mid = _keep_bf16_bits(rest)
    lo = rest - mid
    sums = jnp.dot(tri_ref[...], jnp.concatenate([hi.astype(BF16), mid.astype(BF16), lo.astype(BF16)], axis=1),
                   preferred_element_type=F32)
    q32_ref[...] = q
    k32_ref[...] = k
    cb_ref[...] = sums[:, 0:width] + sums[:, width:2 * width] + sums[:, 2 * width:3 * width]
    ql_ref[0] = q.astype(BF16)
    kl_ref[0] = k.astype(BF16)

    tiles = [(c * hb + hh, slice(c * CHUNK, (c + 1) * CHUNK), slice(hh * REC_DK, (hh + 1) * REC_DK))
             for c in range(n_chunks) for hh in range(hb)]

    total = _block_rows(cb_ref[...], CHUNK, 0 if reverse else CHUNK - 1)
    qd = (q32_ref[...] * jnp.exp2(cb_ref[...])).astype(BF16)
    kd = (k32_ref[...] * jnp.exp2(total - cb_ref[...])).astype(BF16)
    for idx, rows, cols in tiles:
        kv_ref[idx] = lax.dot_general(v_ref[rows, cols], kd[rows, cols], tn, preferred_element_type=F32)

    level_of = lvl_ref[...]
    for lvl in range(N_LEVELS + 1):
        if lvl > 0:
            b = CHUNK >> lvl
            d = cb_ref[...] - _block_rows(cb_ref[...], 2 * b, b if reverse else b - 1)
            ql_ref[lvl] = (q32_ref[...] * jnp.exp2(jnp.minimum(d, 0.0))).astype(BF16)
            kl_ref[lvl] = (k32_ref[...] * jnp.exp2(jnp.minimum(-d, 0.0))).astype(BF16)
        for idx, rows, cols in tiles:
            pairs = lax.dot_general(ql_ref[lvl, rows, cols], kl_ref[lvl, rows, cols], nt,
                                    preferred_element_type=F32)
            a_ref[idx] = jnp.where(level_of == lvl, pairs, 0.0 if lvl == 0 else a_ref[idx])

    for idx, rows, cols in tiles:
        oi_ref[rows, cols] = jnp.dot(a_ref[idx].astype(BF16), v_ref[rows, cols], preferred_element_type=F32)

    decay = jnp.exp2(total)
    for c in (range(n_chunks - 1, -1, -1) if reverse else range(n_chunks)):
        rows = slice(c * CHUNK, (c + 1) * CHUNK)
        for hh in range(hb):
            cols = slice(hh * REC_DK, (hh + 1) * REC_DK)
            st = st_ref[hh]
            o = oi_ref[rows, cols] + lax.dot_general(qd[rows, cols], st.astype(BF16), nt,
                                                     preferred_element_type=F32)
            o_ref[rows, cols] = o.astype(o_ref.dtype)
            st_ref[hh] = st * decay[c * CHUNK:c * CHUNK + 1, cols] + kv_ref[c * hb + hh]


def hgrn_scan(proj, lb_dir, f_sec, reverse, seq, ctx, batch, n_heads, hb=4):
    rows = proj.shape[0]
    tb = ctx
    per_batch = seq // tb
    ctx_blk0 = (batch * seq) // tb
    groups = n_heads // hb
    width = hb * REC_DK
    n_chunks = tb // CHUNK
    tri, level_of = _scan_constants(reverse, n_chunks)

    def row_blk(b, s):
        lat = b * per_batch + (per_batch - s if reverse else s - 1)
        return jnp.where(s == 0, ctx_blk0 + b, lat)

    return pl.pallas_call(
        functools.partial(_scan_kernel, reverse=reverse, n_chunks=n_chunks, hb=hb),
        grid=(batch, groups, 1 + per_batch),
        in_specs=[
            pl.BlockSpec((tb, width), lambda b, h, s: (row_blk(b, s), h)),
            pl.BlockSpec((tb, width), lambda b, h, s: (row_blk(b, s), groups + h)),
            pl.BlockSpec((tb, width), lambda b, h, s: (row_blk(b, s), f_sec * groups + h)),
            pl.BlockSpec((None, 1, width), lambda b, h, s: (h, 0, 0)),
            pl.BlockSpec(tri.shape, lambda b, h, s: (0, 0)),
            pl.BlockSpec(level_of.shape, lambda b, h, s: (0, 0)),
        ],
        out_specs=pl.BlockSpec((tb, width), lambda b, h, s: (row_blk(b, s), h)),
        out_shape=jax.ShapeDtypeStruct((rows, n_heads * REC_DK), BF16),
        scratch_shapes=[pltpu.VMEM((hb, REC_DK, REC_DK), F32),
                        pltpu.VMEM((tb, width), F32), pltpu.VMEM((tb, width), F32), pltpu.VMEM((tb, width), F32),
                        pltpu.VMEM((N_LEVELS + 1, tb, width), BF16), pltpu.VMEM((N_LEVELS + 1, tb, width), BF16),
                        pltpu.VMEM((tb, width), F32), pltpu.VMEM((n_chunks * hb, REC_DK, REC_DK), F32),
                        pltpu.VMEM((n_chunks * hb, CHUNK, CHUNK), F32)],
        compiler_params=_cparams("arbitrary", "arbitrary", "arbitrary"),
        name="hgrn_scan_bw" if reverse else "hgrn_scan_fw",
    )(proj, proj, proj, lb_dir.reshape(groups, 1, width), tri, level_of)


def _readout_kernel(of_ref, ob_ref, g_ref, gain_ref, o_ref, *, n_heads):
    gain = gain_ref[...]
    for hh in range(n_heads):
        sl = slice(hh * REC_DK, (hh + 1) * REC_DK)
        o = of_ref[:, sl].astype(F32) + ob_ref[:, sl].astype(F32)
        ms = jnp.mean(o * o, axis=-1, keepdims=True)
        g = g_ref[:, sl].astype(F32)
        o_ref[:, sl] = (o * lax.rsqrt(ms + EPS) * gain * (g * jax.nn.sigmoid(g))).astype(o_ref.dtype)


def hgrn_readout(o_fw, o_bw, proj, o_gain, gate_sec, n_heads):
    rows, d = o_fw.shape
    tm = ROW_TILE
    return pl.pallas_call(
        functools.partial(_readout_kernel, n_heads=n_heads),
        grid=(rows // tm,),
        in_specs=[pl.BlockSpec((tm, d), lambda i: (i, 0)),
                  pl.BlockSpec((tm, d), lambda i: (i, 0)),
                  pl.BlockSpec((tm, d), lambda i: (i, gate_sec)),
                  pl.BlockSpec((1, REC_DK), lambda i: (0, 0))],
        out_specs=pl.BlockSpec((tm, d), lambda i: (i, 0)),
        out_shape=jax.ShapeDtypeStruct((rows, d), BF16),
        compiler_params=_cparams("arbitrary"),
        name="hgrn_readout",
    )(o_fw, o_bw, proj, o_gain.reshape(1, REC_DK))


def _router_kernel(y_ref, w_ref, o_ref):
    logits = jnp.dot(y_ref[...], w_ref[...], preferred_element_type=F32, precision=lax.Precision.HIGHEST)
    lane = lax.broadcasted_iota(jnp.int32, logits.shape, 1)
    neg = -jnp.inf
    logits = jnp.where(lane < N_EXPERTS, logits, neg)
    m1 = jnp.max(logits, axis=-1, keepdims=True)
    i1 = jnp.min(jnp.where(logits == m1, lane, LANES), axis=-1, keepdims=True)
    rest = jnp.where(lane == i1, neg, logits)
    m2 = jnp.max(rest, axis=-1, keepdims=True)
    i2 = jnp.min(jnp.where(rest == m2, lane, LANES), axis=-1, keepdims=True)
    e = jnp.exp(m2 - m1)
    w1 = 1.0 / (1.0 + e)
    w2 = e / (1.0 + e)
    out = jnp.where(lane == 0, i1.astype(F32),
                    jnp.where(lane == 1, i2.astype(F32),
                              jnp.where(lane == 2, w1, jnp.where(lane == 3, w2, 0.0))))
    o_ref[...] = out


def moe_router(y, w_router):
    rows, d = y.shape
    tm = ROW_TILE
    w_pad = jnp.zeros((d, LANES), F32).at[:, :N_EXPERTS].set(w_router)
    return pl.pallas_call(
        _router_kernel,
        grid=(rows // tm,),
        in_specs=[pl.BlockSpec((tm, d), lambda i: (i, 0)), pl.BlockSpec((d, LANES), lambda i: (0, 0))],
        out_specs=pl.BlockSpec((tm, LANES), lambda i: (i, 0)),
        out_shape=jax.ShapeDtypeStruct((rows, LANES), F32),
        compiler_params=_cparams("arbitrary"),
        name="moe_router",
    )(y, w_pad)


def _row_copy(src_ref, dst_ref, sem, src_row, dst_row):
    return pltpu.make_async_copy(src_ref.at[pl.ds(src_row, 1), :], dst_ref.at[pl.ds(dst_row, 1), :], sem)


def _start_row_gather(idx_ref, src_ref, dst_ref, sem):
    def start(i, carry):
        for prio in range(2):
            r = 2 * i + prio
            _row_copy(src_ref, dst_ref, sem, idx_ref[0, r], r).start(priority=prio)
        return carry

    lax.fori_loop(0, dst_ref.shape[0] // 2, start, 0, unroll=4)


def _wait_row_gather(src_ref, dst_ref, sem):
    def wait(r, carry):
        _row_copy(src_ref, dst_ref, sem, 0, r).wait()
        return carry

    lax.fori_loop(0, dst_ref.shape[0], wait, 0, unroll=8)


def _start_rows_unrolled(idx_ref, src_ref, dst_ref, sem, first, count):
    for r in range(count):
        row = first + r
        _row_copy(src_ref, dst_ref, sem, idx_ref[0, row], row).start(priority=r % 2)


def _expert_up_kernel(te_ref, ta_ref, idx_ref, idx_next_ref, y_ref, w1_ref, w3_ref, o_ref, x_ref, sem, *, n_tiles):
    i = pl.program_id(0)
    j = pl.program_id(1)
    slot = i % 2

    @pl.when(j == 0)
    def _():
        @pl.when(i == 0)
        def _():
            _start_row_gather(idx_ref, y_ref, x_ref.at[0], sem.at[0])

        @pl.when(i + 1 < n_tiles)
        def _():
            _start_row_gather(idx_next_ref, y_ref, x_ref.at[1 - slot], sem.at[1 - slot])

        _wait_row_gather(y_ref, x_ref.at[slot], sem.at[slot])

    @pl.when(ta_ref[i] > 0)
    def _():
        x = x_ref[slot].astype(BF16)
        u = jnp.dot(x, w1_ref[...], preferred_element_type=F32)
        g = jnp.dot(x, w3_ref[...], preferred_element_type=F32)
        o_ref[...] = (u * jax.nn.sigmoid(u) * g).astype(o_ref.dtype)

    @pl.when(ta_ref[i] == 0)
    def _():
        o_ref[...] = jnp.zeros(o_ref.shape, o_ref.dtype)


def expert_up(y, token_of_slot, w1, w3, tile_expert, tile_active, tn=1024):
    p = token_of_slot.shape[0]
    d = y.shape[1]
    f = w1.shape[2]
    tm = ROW_TILE
    n_tiles = p // tm
    idx = token_of_slot.reshape(n_tiles, 1, tm)
    return pl.pallas_call(
        functools.partial(_expert_up_kernel, n_tiles=n_tiles),
        grid_spec=pltpu.PrefetchScalarGridSpec(
            num_scalar_prefetch=2,
            grid=(n_tiles, f // tn),
            in_specs=[pl.BlockSpec((None, 1, tm), lambda i, j, te, ta: (i, 0, 0), memory_space=pltpu.SMEM),
                      pl.BlockSpec((None, 1, tm), lambda i, j, te, ta: (jnp.minimum(i + 1, n_tiles - 1), 0, 0),
                                   memory_space=pltpu.SMEM),
                      pl.BlockSpec(memory_space=pl.ANY),
                      pl.BlockSpec((None, d, tn), lambda i, j, te, ta: (te[i], 0, j)),
                      pl.BlockSpec((None, d, tn), lambda i, j, te, ta: (te[i], 0, j))],
            out_specs=pl.BlockSpec((tm, tn), lambda i, j, te, ta: (i, j)),
            scratch_shapes=[pltpu.VMEM((2, tm, d), y.dtype), pltpu.SemaphoreType.DMA((2,))],
        ),
        out_shape=jax.ShapeDtypeStruct((p, f), BF16),
        compiler_params=_cparams("arbitrary", "arbitrary"),
        name="moe_expert_up",
    )(tile_expert, tile_active, idx, idx, y, w1, w3)


def _expert_down_kernel(te_ref, ta_ref, a_ref, w_ref, o_ref, acc_ref, *, nk):
    i = pl.program_id(0)
    kk = pl.program_id(1)

    @pl.when(ta_ref[i] > 0)
    def _():
        part = jnp.dot(a_ref[...], w_ref[...], preferred_element_type=F32)

        @pl.when(kk == 0)
        def _():
            acc_ref[...] = part

        @pl.when(jnp.logical_and(kk > 0, kk < nk - 1))
        def _():
            acc_ref[...] += part

        @pl.when(kk == nk - 1)
        def _():
            o_ref[...] = acc_ref[...] + part

    @pl.when(jnp.logical_and(ta_ref[i] == 0, kk == nk - 1))
    def _():
        o_ref[...] = jnp.zeros(o_ref.shape, o_ref.dtype)


def expert_down(a, w2, tile_expert, tile_active, tk=2048):
    p, f = a.shape
    d = w2.shape[2]
    tm = ROW_TILE
    nk = f // tk
    assert nk >= 2
    return pl.pallas_call(
        functools.partial(_expert_down_kernel, nk=nk),
        grid_spec=pltpu.PrefetchScalarGridSpec(
            num_scalar_prefetch=2,
            grid=(p // tm, nk),
            in_specs=[pl.BlockSpec((tm, tk), lambda i, kk, te, ta: (i, kk)),
                      pl.BlockSpec((None, tk, d), lambda i, kk, te, ta: (te[i], kk, 0))],
            out_specs=pl.BlockSpec((tm, d), lambda i, kk, te, ta: (i, 0)),
            scratch_shapes=[pltpu.VMEM((tm, d), F32)],
        ),
        out_shape=jax.ShapeDtypeStruct((p, d), F32),
        compiler_params=_cparams("arbitrary", "arbitrary"),
        name="moe_expert_down",
    )(tile_expert, tile_active, a, w2)


def _combine_kernel(s0_ref, s1_ref, s0_next_ref, s1_next_ref, src_ref, route_ref, h_ref, gate_ref, gain_ref, o_ref,
                    buf0, buf1, sem, *, n_tiles):
    i = pl.program_id(0)
    slot = i % 2
    tile = o_ref.shape[0]

    @pl.when(i == 0)
    def _():
        _start_row_gather(s0_ref, src_ref, buf0.at[0], sem.at[0])
        _start_row_gather(s1_ref, src_ref, buf1.at[0], sem.at[2])

    _wait_row_gather(src_ref, buf0.at[slot], sem.at[slot])
    _wait_row_gather(src_ref, buf1.at[slot], sem.at[2 + slot])
    _start_rows_unrolled(s0_next_ref, src_ref, buf0.at[1 - slot], sem.at[1 - slot], 0, tile)
    _start_rows_unrolled(s1_next_ref, src_ref, buf1.at[1 - slot], sem.at[3 - slot], 0, tile)
    w0 = route_ref[:, TOP_K:TOP_K + 1]
    w1 = route_ref[:, TOP_K + 1:TOP_K + 2]
    o_ref[...] = _residual_epilogue(w0 * buf0[slot] + w1 * buf1[slot], h_ref[...], gate_ref[...], gain_ref[...])

    @pl.when(i == n_tiles - 1)
    def _():
        _wait_row_gather(src_ref, buf0.at[1 - slot], sem.at[1 - slot])
        _wait_row_gather(src_ref, buf1.at[1 - slot], sem.at[3 - slot])


def moe_combine_resnorm(expert_out, slot0, slot1, route, h, gain, mods, gate_sec, seq, batch, out_rows, tile=256):
    rows, d = h.shape
    row = functools.partial(_mod_row, tm=tile, seq=seq, batch=batch)
    n_tiles = out_rows // tile
    idx_spec = pl.BlockSpec((None, 1, tile), lambda i: (i, 0, 0), memory_space=pltpu.SMEM)
    next_spec = pl.BlockSpec((None, 1, tile), lambda i: (jnp.minimum(i + 1, n_tiles - 1), 0, 0),
                             memory_space=pltpu.SMEM)
    s0 = slot0.reshape(rows // tile, 1, tile)
    s1 = slot1.reshape(rows // tile, 1, tile)
    return pl.pallas_call(
        functools.partial(_combine_kernel, n_tiles=n_tiles),
        grid=(n_tiles,),
        in_specs=[idx_spec, idx_spec, next_spec, next_spec,
                  pl.BlockSpec(memory_space=pl.ANY),
                  pl.BlockSpec((tile, LANES), lambda i: (i, 0)),
                  pl.BlockSpec((tile, d), lambda i: (i, 0)),
                  pl.BlockSpec((None, 1, d), lambda i: (row(i), 0, gate_sec)),
                  pl.BlockSpec((1, d), lambda i: (0, 0))],
        out_specs=pl.BlockSpec((tile, d), lambda i: (i, 0)),
        out_shape=jax.ShapeDtypeStruct((out_rows, d), F32),
        scratch_shapes=[pltpu.VMEM((2, tile, d), F32), pltpu.VMEM((2, tile, d), F32),
                        pltpu.SemaphoreType.DMA((4,))],
        compiler_params=_cparams("arbitrary"),
        name="moe_combine_resnorm",
    )(s0, s1, s0, s1, expert_out, route, h, mods, gain.reshape(1, d))


def _routing_plan(route, tm):
    rows = route.shape[0]
    experts = route[:, :TOP_K].astype(jnp.int32).reshape(-1)
    onehot = (experts[:, None] == jnp.arange(N_EXPERTS)[None, :]).astype(jnp.int32)
    rank = jnp.sum((jnp.cumsum(onehot, axis=0) - onehot) * onehot, axis=1)
    counts = jnp.sum(onehot, axis=0)
    tiles_per = (counts + tm - 1) // tm
    tile_end = jnp.cumsum(tiles_per)
    start = (tile_end - tiles_per) * tm
    slot = jnp.sum(onehot * start[None, :], axis=1) + rank
    n_slots = rows * TOP_K + N_EXPERTS * tm
    n_tiles = n_slots // tm
    token_of_slot = jnp.zeros((n_slots,), jnp.int32).at[slot].set(jnp.arange(rows * TOP_K, dtype=jnp.int32) // TOP_K)
    tile_ids = jnp.arange(n_tiles, dtype=jnp.int32)
    tile_expert = jnp.minimum(jnp.sum((tile_ids[:, None] >= tile_end[None, :]).astype(jnp.int32), axis=1),
                              N_EXPERTS - 1)
    tile_active = (tile_ids < tile_end[-1]).astype(jnp.int32)
    slots = slot.reshape(rows, TOP_K).astype(jnp.int32)
    return token_of_slot, tile_expert, tile_active, slots[:, 0], slots[:, 1]


def moe_ffn_resnorm(y, h, w_router, w1, w3, w2, gain, mods, gate_sec, seq, batch, out_rows):
    route = moe_router(y, w_router)
    token_of_slot, tile_expert, tile_active, slot0, slot1 = _routing_plan(route, ROW_TILE)
    hid = expert_up(y, token_of_slot, w1, w3, tile_expert, tile_active)
    out = expert_down(hid, w2, tile_expert, tile_active)
    return moe_combine_resnorm(out, slot0, slot1, route, h, gain, mods, gate_sec, seq, batch, out_rows)


def _lower_bounds(logits):
    p = jax.nn.softmax(logits.astype(F32), axis=0)
    return jnp.cumsum(p, axis=0) - p[0]


def kernel(x, c, ctx, c_ctx, w_mod, b_mod, norm_gains, attn_w_in, attn_w_o, attn_q_gain, attn_k_gain,
           rec_w_in, rec_w_o, rec_lb_logits, rec_o_gain, ffn_w1, ffn_w3, ffn_w2,
           moe_w_router, moe_w1, moe_w3, moe_w2):
    batch, seq, d = x.shape
    ctx_len = ctx.shape[1]
    depth = w_mod.shape[0]
    n_lat = batch * seq
    n_heads = d // REC_DK
    assert seq % ROW_TILE == 0 and (batch * ctx_len) % ROW_TILE == 0 and batch + 1 <= MOD_ROWS

    hg = jnp.concatenate([x.reshape(n_lat, d), ctx.reshape(batch * ctx_len, d)], axis=0)
    cond = jnp.zeros((MOD_ROWS, d), F32).at[:batch].set(c).at[batch].set(c_ctx)
    mods_all = modulation_table(cond, w_mod, b_mod)
    lower = _lower_bounds(rec_lb_logits)
    tables = rope_tables(seq, ROW_TILE)
    tq = min(256, ctx_len)
    tk = min(2048, seq // 4)

    for layer in range(depth):
        j = layer // 2
        ng = norm_gains[layer]
        mods = mods_all[layer]
        if layer % 2 == 0:
            qkv = norm_matmul(hg, ng[0], mods, 0, 1, seq, batch, attn_w_in[j].astype(BF16), BF16, tn=1024)
            q, k, v_ext = qk_prep(qkv, attn_q_gain[j], attn_k_gain[j], tables, seq, n_lat)
            mix = flash_attention(q, k, v_ext, seq, ctx_len, batch, tq, tk)
            hg = matmul_resnorm(mix, attn_w_o[j].astype(BF16), hg, ng[1], mods, 2, seq, batch, tk=d)
        else:
            proj = norm_matmul(hg, ng[0], mods, 0, 1, seq, batch, rec_w_in[j].astype(BF16), BF16, tn=1024)
            o_fw = hgrn_scan(proj, lower[layer, 0], 2, False, seq, ctx_len, batch, n_heads)
            o_bw = hgrn_scan(proj, lower[layer, 1], 3, True, seq, ctx_len, batch, n_heads)
            mix = hgrn_readout(o_fw, o_bw, proj, rec_o_gain[j], 4, n_heads)
            hg = matmul_resnorm(mix, rec_w_o[j].astype(BF16), hg, ng[1], mods, 2, seq, batch, tk=d)
        if layer % 2 == 0:
            hid = norm_swiglu_up(hg, ng[2], mods, 3, 4, seq, batch, ffn_w1[j].astype(BF16), ffn_w3[j].astype(BF16))
            f = hid.shape[1]
            hg = matmul_resnorm(hid, ffn_w2[j].astype(BF16), hg, ng[3], mods, 5, seq, batch, tk=f // 2)
        else:
            y = norm_modulate(hg, ng[2], mods, 3, 4, seq, batch, F32)
            out_rows = n_lat if layer == depth - 1 else hg.shape[0]
            hg = moe_ffn_resnorm(y, hg, moe_w_router[j], moe_w1[j].astype(BF16), moe_w3[j].astype(BF16),
                                 moe_w2[j].astype(BF16), ng[3], mods, 5, seq, batch, out_rows)
    return hg[:n_lat].reshape(batch, seq, d)
```

```python
import functools

import numpy as np
import jax
import jax.numpy as jnp
from jax import lax
from jax.experimental import pallas as pl
from jax.experimental.pallas import tpu as pltpu

F32 = jnp.float32
BF16 = jnp.bfloat16

HEAD_DIM = 128
N_KV_HEADS = 4
GROUP = 4
GRID_W = 64
ROPE_THETA = 10000.0
AXIS_ROT_DIM = HEAD_DIM // 2
REC_DK = 128
N_EXPERTS = 8
TOP_K = 2
N_MOD = 6
EPS = 1e-6
CHUNK = 64
N_LEVELS = 6

LANES = 128
VMEM_LIMIT = 56 * 2**20

ROW_TILE = 512
MOD_ROWS = 8


def _cparams(*sem):
    return pltpu.CompilerParams(dimension_semantics=sem, vmem_limit_bytes=VMEM_LIMIT)


def _mod_row(i, tm, seq, batch):
    return jnp.minimum((i * tm) // seq, batch)


def _mod_kernel(c_ref, w_ref, b_ref, o_ref):
    c = c_ref[...]
    s = (c * jax.nn.sigmoid(c)).astype(BF16)
    o_ref[...] = jnp.dot(s, w_ref[...].astype(BF16), preferred_element_type=F32) + b_ref[...]


def modulation_table(cond, w_mod, b_mod, tn=1024):
    depth, d, n = w_mod.shape
    out = pl.pallas_call(
        _mod_kernel,
        grid=(depth, n // tn),
        in_specs=[
            pl.BlockSpec((MOD_ROWS, d), lambda l, j: (0, 0)),
            pl.BlockSpec((None, d, tn), lambda l, j: (l, 0, j)),
            pl.BlockSpec((None, 1, tn), lambda l, j: (l, 0, j)),
        ],
        out_specs=pl.BlockSpec((None, MOD_ROWS, tn), lambda l, j: (l, 0, j)),
        out_shape=jax.ShapeDtypeStruct((depth, MOD_ROWS, n), F32),
        compiler_params=_cparams("arbitrary", "arbitrary"),
        name="modulation_table",
    )(cond, w_mod, b_mod.reshape(depth, 1, n))
    return out.reshape(depth, MOD_ROWS, 1, n)


def _norm_mod_kernel(h_ref, g_ref, sh_ref, sc_ref, o_ref):
    x = h_ref[...]
    ms = jnp.mean(x * x, axis=-1, keepdims=True)
    y = x * lax.rsqrt(ms + EPS) * g_ref[...]
    o_ref[...] = (y * (1.0 + sc_ref[...]) + sh_ref[...]).astype(o_ref.dtype)


def norm_modulate(h, gain, mods, shift_sec, scale_sec, seq, batch, out_dtype):
    rows, d = h.shape
    tm = ROW_TILE
    row = functools.partial(_mod_row, tm=tm, seq=seq, batch=batch)
    return pl.pallas_call(
        _norm_mod_kernel,
        grid=(rows // tm,),
        in_specs=[
            pl.BlockSpec((tm, d), lambda i: (i, 0)),
            pl.BlockSpec((1, d), lambda i: (0, 0)),
            pl.BlockSpec((None, 1, d), lambda i: (row(i), 0, shift_sec)),
            pl.BlockSpec((None, 1, d), lambda i: (row(i), 0, scale_sec)),
        ],
        out_specs=pl.BlockSpec((tm, d), lambda i: (i, 0)),
        out_shape=jax.ShapeDtypeStruct((rows, d), out_dtype),
        compiler_params=_cparams("arbitrary"),
        name="norm_modulate",
    )(h, gain.reshape(1, d), mods, mods)


def _normed_lhs(h_ref, g_ref, sh_ref, sc_ref, y_ref):
    @pl.when(pl.program_id(1) == 0)
    def _():
        _norm_mod_kernel(h_ref, g_ref, sh_ref, sc_ref, y_ref)


def _norm_specs(tm, d, row, shift_sec, scale_sec):
    return [pl.BlockSpec((tm, d), lambda i, j: (i, 0)),
            pl.BlockSpec((1, d), lambda i, j: (0, 0)),
            pl.BlockSpec((None, 1, d), lambda i, j: (row(i), 0, shift_sec)),
            pl.BlockSpec((None, 1, d), lambda i, j: (row(i), 0, scale_sec))]


def _norm_mm_kernel(h_ref, g_ref, sh_ref, sc_ref, b_ref, o_ref, y_ref):
    _normed_lhs(h_ref, g_ref, sh_ref, sc_ref, y_ref)
    o_ref[...] = jnp.dot(y_ref[...], b_ref[...], preferred_element_type=F32).astype(o_ref.dtype)


def norm_matmul(h, gain, mods, shift_sec, scale_sec, seq, batch, b, out_dtype, tn):
    m, d = h.shape
    n = b.shape[1]
    tm = ROW_TILE
    row = functools.partial(_mod_row, tm=tm, seq=seq, batch=batch)
    return pl.pallas_call(
        _norm_mm_kernel,
        grid=(m // tm, n // tn),
        in_specs=_norm_specs(tm, d, row, shift_sec, scale_sec) + [pl.BlockSpec((d, tn), lambda i, j: (0, j))],
        out_specs=pl.BlockSpec((tm, tn), lambda i, j: (i, j)),
        out_shape=jax.ShapeDtypeStruct((m, n), out_dtype),
        scratch_shapes=[pltpu.VMEM((tm, d), BF16)],
        compiler_params=_cparams("arbitrary", "arbitrary"),
        name="norm_matmul",
    )(h, gain.reshape(1, d), mods, mods, b)


def _norm_swiglu_up_kernel(h_ref, g_ref, sh_ref, sc_ref, w1_ref, w3_ref, o_ref, y_ref):
    _normed_lhs(h_ref, g_ref, sh_ref, sc_ref, y_ref)
    a = y_ref[...]
    u = jnp.dot(a, w1_ref[...], preferred_element_type=F32)
    g = jnp.dot(a, w3_ref[...], preferred_element_type=F32)
    o_ref[...] = (u * jax.nn.sigmoid(u) * g).astype(o_ref.dtype)


def norm_swiglu_up(h, gain, mods, shift_sec, scale_sec, seq, batch, w1, w3, tn=1408):
    m, d = h.shape
    n = w1.shape[1]
    tm = ROW_TILE
    row = functools.partial(_mod_row, tm=tm, seq=seq, batch=batch)
    return pl.pallas_call(
        _norm_swiglu_up_kernel,
        grid=(m // tm, n // tn),
        in_specs=_norm_specs(tm, d, row, shift_sec, scale_sec) + [pl.BlockSpec((d, tn), lambda i, j: (0, j)),
                                                                  pl.BlockSpec((d, tn), lambda i, j: (0, j))],
        out_specs=pl.BlockSpec((tm, tn), lambda i, j: (i, j)),
        out_shape=jax.ShapeDtypeStruct((m, n), BF16),
        scratch_shapes=[pltpu.VMEM((tm, d), BF16)],
        compiler_params=_cparams("arbitrary", "arbitrary"),
        name="norm_swiglu_up",
    )(h, gain.reshape(1, d), mods, mods, w1, w3)


def _residual_epilogue(r, h, gate, gain):
    ms = jnp.mean(r * r, axis=-1, keepdims=True)
    return h + gate * (r * lax.rsqrt(ms + EPS) * gain)


def _mm_resnorm_kernel(a_ref, b_ref, h_ref, gate_ref, gain_ref, o_ref, acc_ref, *, nk):
    kk = pl.program_id(1)
    part = jnp.dot(a_ref[...], b_ref[...], preferred_element_type=F32)

    if nk == 1:
        o_ref[...] = _residual_epilogue(part, h_ref[...], gate_ref[...], gain_ref[...])
        return

    @pl.when(kk == 0)
    def _():
        acc_ref[...] = part

    @pl.when(kk > 0)
    def _():
        acc_ref[...] += part

    @pl.when(kk == nk - 1)
    def _():
        o_ref[...] = _residual_epilogue(acc_ref[...], h_ref[...], gate_ref[...], gain_ref[...])


def matmul_resnorm(a, b, h, gain, mods, gate_sec, seq, batch, tk):
    m, k = a.shape
    d = b.shape[1]
    tm = ROW_TILE
    nk = k // tk
    row = functools.partial(_mod_row, tm=tm, seq=seq, batch=batch)
    return pl.pallas_call(
        functools.partial(_mm_resnorm_kernel, nk=nk),
        grid=(m // tm, nk),
        in_specs=[
            pl.BlockSpec((tm, tk), lambda i, kk: (i, kk)),
            pl.BlockSpec((tk, d), lambda i, kk: (kk, 0)),
            pl.BlockSpec((tm, d), lambda i, kk: (i, 0)),
            pl.BlockSpec((None, 1, d), lambda i, kk: (row(i), 0, gate_sec)),
            pl.BlockSpec((1, d), lambda i, kk: (0, 0)),
        ],
        out_specs=pl.BlockSpec((tm, d), lambda i, kk: (i, 0)),
        out_shape=jax.ShapeDtypeStruct((m, d), F32),
        scratch_shapes=[pltpu.VMEM((tm, d), F32)],
        compiler_params=_cparams("arbitrary", "arbitrary"),
        name="matmul_resnorm",
    )(a, b, h, mods, gain.reshape(1, d))


def rope_tables(seq, tile):
    rows = seq // GRID_W
    row = jnp.repeat(jnp.arange(rows, dtype=F32), GRID_W)
    col = jnp.tile(jnp.arange(GRID_W, dtype=F32), rows)
    inv_freq = ROPE_THETA ** (-jnp.arange(0, AXIS_ROT_DIM, 2, dtype=F32) / AXIS_ROT_DIM)
    ar, ac = row[:, None] * inv_freq, col[:, None] * inv_freq
    zeros = jnp.zeros_like(ar)
    cos = jnp.concatenate([jnp.cos(ar), jnp.cos(ar), jnp.cos(ac), jnp.cos(ac)], axis=-1)
    sin_lo = jnp.concatenate([-jnp.sin(ar), zeros, -jnp.sin(ac), zeros], axis=-1)
    sin_hi = jnp.concatenate([zeros, jnp.sin(ar), zeros, jnp.sin(ac)], axis=-1)
    pad0 = jnp.zeros((tile, HEAD_DIM), F32)
    return (jnp.concatenate([cos, jnp.ones((tile, HEAD_DIM), F32)], axis=0),
            jnp.concatenate([sin_lo, pad0], axis=0),
            jnp.concatenate([sin_hi, pad0], axis=0))


def _qk_prep_kernel(qkv_ref, qg_ref, kg_ref, cos_ref, slo_ref, shi_ref, q_ref, k_ref, v_ref, *, n_q, n_k, q_scale):
    cos, slo, shi = cos_ref[...], slo_ref[...], shi_ref[...]
    quarter = AXIS_ROT_DIM // 2

    def head(x, gain):
        x = x.astype(F32)
        ms = jnp.mean(x * x, axis=-1, keepdims=True)
        xn = x * lax.rsqrt(ms + EPS) * gain
        return (xn * cos + pltpu.roll(xn, HEAD_DIM - quarter, axis=1) * slo
                + pltpu.roll(xn, quarter, axis=1) * shi)

    for hh in range(n_q):
        sl = slice(hh * HEAD_DIM, (hh + 1) * HEAD_DIM)
        q_ref[:, sl] = (head(qkv_ref[:, sl], qg_ref[...]) * q_scale).astype(q_ref.dtype)
    for hh in range(n_k):
        src = slice((n_q + hh) * HEAD_DIM, (n_q + hh + 1) * HEAD_DIM)
        k_ref[:, hh * HEAD_DIM:(hh + 1) * HEAD_DIM] = head(qkv_ref[:, src], kg_ref[...]).astype(k_ref.dtype)
    for hh in range(n_k):
        src = slice((n_q + n_k + hh) * HEAD_DIM, (n_q + n_k + hh + 1) * HEAD_DIM)
        v_ref[:, 2 * hh * HEAD_DIM:(2 * hh + 1) * HEAD_DIM] = qkv_ref[:, src]
        v_ref[:, (2 * hh + 1) * HEAD_DIM:(2 * hh + 2) * HEAD_DIM] = jnp.ones((v_ref.shape[0], HEAD_DIM), v_ref.dtype)


def qk_prep(qkv, q_gain, k_gain, tables, seq, n_lat_rows):
    rows = qkv.shape[0]
    tm = ROW_TILE
    n_q = 4 * N_KV_HEADS
    per_batch = seq // tm
    n_lat_tiles = n_lat_rows // tm
    tab = lambda i: (jnp.where(i < n_lat_tiles, i % per_batch, per_batch), 0)
    q_scale = HEAD_DIM ** -0.5 * float(np.log2(np.e))
    return pl.pallas_call(
        functools.partial(_qk_prep_kernel, n_q=n_q, n_k=N_KV_HEADS, q_scale=q_scale),
        grid=(rows // tm,),
        in_specs=[
            pl.BlockSpec((tm, (n_q + 2 * N_KV_HEADS) * HEAD_DIM), lambda i: (i, 0)),
            pl.BlockSpec((1, HEAD_DIM), lambda i: (0, 0)),
            pl.BlockSpec((1, HEAD_DIM), lambda i: (0, 0)),
            pl.BlockSpec((tm, HEAD_DIM), tab),
            pl.BlockSpec((tm, HEAD_DIM), tab),
            pl.BlockSpec((tm, HEAD_DIM), tab),
        ],
        out_specs=[pl.BlockSpec((tm, n_q * HEAD_DIM), lambda i: (i, 0)),
                   pl.BlockSpec((tm, N_KV_HEADS * HEAD_DIM), lambda i: (i, 0)),
                   pl.BlockSpec((tm, 2 * N_KV_HEADS * HEAD_DIM), lambda i: (i, 0))],
        out_shape=[jax.ShapeDtypeStruct((rows, n_q * HEAD_DIM), BF16),
                   jax.ShapeDtypeStruct((rows, N_KV_HEADS * HEAD_DIM), BF16),
                   jax.ShapeDtypeStruct((rows, 2 * N_KV_HEADS * HEAD_DIM), BF16)],
        compiler_params=_cparams("arbitrary"),
        name="qk_prep",
    )(qkv, q_gain.reshape(1, HEAD_DIM), k_gain.reshape(1, HEAD_DIM), *tables)


def _flash_kernel(q_ref, k_ref, v_ref, kc_ref, vc_ref, o_ref, m_ref, acc_ref,
                  s0_ref, s1_ref, p0_ref, p1_ref, mx0_ref, mx1_ref, alpha0_ref, alpha1_ref, *,
                  n_lat_tiles, nkv, tq, tk, ctx):
    qi = pl.program_id(1)
    r = GROUP * tq
    rc = min(128, r)
    s_ref, p_ref = (s0_ref, s1_ref), (p0_ref, p1_ref)
    mx_ref, alpha_ref = (mx0_ref, mx1_ref), (alpha0_ref, alpha1_ref)
    q = jnp.concatenate([q_ref[:, g * HEAD_DIM:(g + 1) * HEAD_DIM] for g in range(GROUP)], axis=0)
    nt = (((1,), (1,)), ((), ()))

    def scores(k):
        return lax.dot_general(q, k, nt, preferred_element_type=F32)

    def times_values(p, v):
        return jnp.dot(p, v, preferred_element_type=F32)

    def kv_rows(j):
        start = j * tk
        return pl.ds(start if isinstance(j, int) else pl.multiple_of(start, tk), tk)

    def put_scores(slot, k):
        width = k.shape[0]
        s = scores(k)
        s_ref[slot][:, 0:width] = s
        mx = s[:, 0:LANES]
        for c0 in range(LANES, width, LANES):
            mx = jnp.maximum(mx, s[:, c0:c0 + LANES])
        mx_ref[slot][...] = mx

    def softmax_stage(slot, width, first):
        for r0 in range(0, r, rc):
            rows = slice(r0, r0 + rc)
            row_max = jnp.max(mx_ref[slot][rows, :], axis=-1, keepdims=True)
            if first:
                m_new = jnp.broadcast_to(row_max, (rc, LANES))
            else:
                m_prev = m_ref[rows, :]
                m_new = jnp.maximum(m_prev, row_max)
                alpha_ref[slot][rows, :] = jnp.exp2(m_prev - m_new)
            m_ref[rows, :] = m_new
            for c0 in range(0, width, LANES):
                cols = slice(c0, c0 + LANES)
                p_ref[slot][rows, cols] = jnp.exp2(s_ref[slot][rows, cols] - m_new).astype(BF16)

    def accumulate(slot, v):
        pv = times_values(p_ref[slot][...], v)
        for r0 in range(0, r, rc):
            rows = slice(r0, r0 + rc)
            a = alpha_ref[slot][rows, :]
            for c0 in (0, LANES):
                cols = slice(c0, c0 + LANES)
                acc_ref[rows, cols] = a * acc_ref[rows, cols] + pv[rows, cols]

    def stage(j, slot, last):
        if not last:
            put_scores(1 - slot, k_ref[kv_rows(j + 1), :])
        softmax_stage(slot, tk, False)
        accumulate(slot, v_ref[kv_rows(j), :])

    put_scores(0, kc_ref[...])
    softmax_stage(0, ctx, True)
    acc_ref[...] = times_values(p_ref[0][:, 0:ctx], vc_ref[...])

    @pl.when(qi < n_lat_tiles)
    def _():
        put_scores(0, k_ref[kv_rows(0), :])

        def pair(t, carry):
            stage(2 * t, 0, False)
            stage(2 * t + 1, 1, False)
            return carry

        lax.fori_loop(0, (nkv - 2) // 2, pair, 0)
        stage(nkv - 2, 0, False)
        stage(nkv - 1, 1, True)

    for g in range(GROUP):
        rows = slice(g * tq, (g + 1) * tq)
        o = acc_ref[rows, 0:HEAD_DIM] / acc_ref[rows, HEAD_DIM:2 * HEAD_DIM]
        o_ref[:, g * HEAD_DIM:(g + 1) * HEAD_DIM] = o.astype(o_ref.dtype)


def flash_attention(q, k, v_ext, seq, ctx, batch, tq, tk):
    rows = q.shape[0]
    n_q_heads = GROUP * N_KV_HEADS
    per_batch_q = seq // tq
    n_lat_tiles = batch * per_batch_q
    n_ctx_tiles = batch * (ctx // tq)
    nkv = seq // tk
    assert nkv >= 2 and nkv % 2 == 0 and ctx <= tk and ctx % LANES == 0
    ctx_blk0 = (batch * seq) // ctx
    r = GROUP * tq

    def batch_of(qi):
        return jnp.where(qi < n_lat_tiles, qi // per_batch_q, (qi - n_lat_tiles) // (ctx // tq))

    grid = (N_KV_HEADS, n_lat_tiles + n_ctx_tiles)
    return pl.pallas_call(
        functools.partial(_flash_kernel, n_lat_tiles=n_lat_tiles, nkv=nkv, tq=tq, tk=tk, ctx=ctx),
        grid=grid,
        in_specs=[
            pl.BlockSpec((tq, GROUP * HEAD_DIM), lambda h, qi: (qi, h)),
            pl.BlockSpec((seq, HEAD_DIM), lambda h, qi: (batch_of(qi), h)),
            pl.BlockSpec((seq, 2 * HEAD_DIM), lambda h, qi: (batch_of(qi), h)),
            pl.BlockSpec((ctx, HEAD_DIM), lambda h, qi: (ctx_blk0 + batch_of(qi), h)),
            pl.BlockSpec((ctx, 2 * HEAD_DIM), lambda h, qi: (ctx_blk0 + batch_of(qi), h)),
        ],
        out_specs=pl.BlockSpec((tq, GROUP * HEAD_DIM), lambda h, qi: (qi, h)),
        out_shape=jax.ShapeDtypeStruct((rows, n_q_heads * HEAD_DIM), BF16),
        scratch_shapes=[pltpu.VMEM((r, LANES), F32), pltpu.VMEM((r, 2 * HEAD_DIM), F32),
                        pltpu.VMEM((r, tk), F32), pltpu.VMEM((r, tk), F32),
                        pltpu.VMEM((r, tk), BF16), pltpu.VMEM((r, tk), BF16),
                        pltpu.VMEM((r, LANES), F32), pltpu.VMEM((r, LANES), F32),
                        pltpu.VMEM((r, LANES), F32), pltpu.VMEM((r, LANES), F32)],
        compiler_params=_cparams("arbitrary", "arbitrary"),
        name="flash_attention",
    )(q, k, v_ext, k, v_ext)


def _scan_constants(reverse, n_chunks):
    t = np.arange(CHUNK)[:, None]
    u = np.arange(CHUNK)[None, :]
    tri = (u >= t) if reverse else (u <= t)
    level_of = np.where(t == u, 0, -1)
    for lvl in range(N_LEVELS):
        b = CHUNK >> (lvl + 1)
        same = (t & ~(2 * b - 1)) == (u & ~(2 * b - 1))
        t_hi, s_hi = (t & b) != 0, (u & b) != 0
        level_of = np.where(same & (~t_hi & s_hi if reverse else t_hi & ~s_hi), lvl + 1, level_of)
    return jnp.asarray(np.kron(np.eye(n_chunks), tri), BF16), jnp.asarray(level_of, jnp.int32)


def _keep_bf16_bits(x):
    bits = lax.bitcast_convert_type(x, jnp.int32) & jnp.int32(-65536)
    return lax.bitcast_convert_type(bits, F32)


def _block_rows(x, block, off):
    rows, width = x.shape
    if block >= 8:
        parts = [jnp.broadcast_to(x[base + off:base + off + 1], (block, width)) for base in range(0, rows, block)]
        return parts[0] if len(parts) == 1 else jnp.concatenate(parts, axis=0)
    within = lax.broadcasted_iota(jnp.int32, x.shape, 0) & (block - 1)
    out = x
    for r in range(block):
        if r != off:
            out = jnp.where(within == r, pltpu.roll(x, (r - off) % rows, axis=0), out)
    return out


def _scan_kernel(q_ref, v_ref, f_ref, lb_ref, tri_ref, lvl_ref, o_ref,
                 st_ref, q32_ref, k32_ref, cb_ref, ql_ref, kl_ref, oi_ref, kv_ref, a_ref, *, reverse, n_chunks, hb):
    step = pl.program_id(2)

    @pl.when(step == 0)
    def _():
        st_ref[...] = jnp.zeros(st_ref.shape, F32)

    width = hb * REC_DK
    nt = (((1,), (1,)), ((), ()))
    tn = (((0,), (0,)), ((), ()))

    lb = lb_ref[...]
    qr = q_ref[...].astype(F32)
    q = (qr * (0.5 * REC_DK ** -0.5)) * (1.0 + jnp.tanh(0.5 * qr))
    fg = (0.5 + 0.5 * lb) + (0.5 - 0.5 * lb) * jnp.tanh(0.5 * f_ref[...].astype(F32))
    k = 1.0 - fg
    lf = jnp.log2(fg)
    hi = _keep_bf16_bits(lf)
    rest = lf - hi
    mid = _keep_bf16_bits(rest)
    lo = rest - mid
    sums = jnp.dot(tri_ref[...], jnp.concatenate([hi.astype(BF16), mid.astype(BF16), lo.astype(BF16)], axis=1),
                   preferred_element_type=F32)
    q32_ref[...] = q
    k32_ref[...] = k
    cb_ref[...] = sums[:, 0:width] + sums[:, width:2 * width] + sums[:, 2 * width:3 * width]
    ql_ref[0] = q.astype(BF16)
    kl_ref[0] = k.astype(BF16)

    tiles = [(c * hb + hh, slice(c * CHUNK, (c + 1) * CHUNK), slice(hh * REC_DK, (hh + 1) * REC_DK))
             for c in range(n_chunks) for hh in range(hb)]

    total = _block_rows(cb_ref[...], CHUNK, 0 if reverse else CHUNK - 1)
    qd = (q32_ref[...] * jnp.exp2(cb_ref[...])).astype(BF16)
    kd = (k32_ref[...] * jnp.exp2(total - cb_ref[...])).astype(BF16)
    for idx, rows, cols in tiles:
        kv_ref[idx] = lax.dot_general(v_ref[rows, cols], kd[rows, cols], tn, preferred_element_type=F32)

    level_of = lvl_ref[...]
    for lvl in range(N_LEVELS + 1):
        if lvl > 0:
            b = CHUNK >> lvl
            d = cb_ref[...] - _block_rows(cb_ref[...], 2 * b, b if reverse else b - 1)
            ql_ref[lvl] = (q32_ref[...] * jnp.exp2(jnp.minimum(d, 0.0))).astype(BF16)
            kl_ref[lvl] = (k32_ref[...] * jnp.exp2(jnp.minimum(-d, 0.0))).astype(BF16)
        for idx, rows, cols in tiles:
            pairs = lax.dot_general(ql_ref[lvl, rows, cols], kl_ref[lvl, rows, cols], nt,
                                    preferred_element_type=F32)
            a_ref[idx] = jnp.where(level_of == lvl, pairs, 0.0 if lvl == 0 else a_ref[idx])

    for idx, rows, cols in tiles:
        oi_ref[rows, cols] = jnp.dot(a_ref[idx].astype(BF16), v_ref[rows, cols], preferred_element_type=F32)

    decay = jnp.exp2(total)
    for c in (range(n_chunks - 1, -1, -1) if reverse else range(n_chunks)):
        rows = slice(c * CHUNK, (c + 1) * CHUNK)
        for hh in range(hb):
            cols = slice(hh * REC_DK, (hh + 1) * REC_DK)
            st = st_ref[hh]
            o = oi_ref[rows, cols] + lax.dot_general(qd[rows, cols], st.astype(BF16), nt,
                                                     preferred_element_type=F32)
            o_ref[rows, cols] = o.astype(o_ref.dtype)
            st_ref[hh] = st * decay[c * CHUNK:c * CHUNK + 1, cols] + kv_ref[c * hb + hh]


def hgrn_scan(proj, lb_dir, f_sec, reverse, seq, ctx, batch, n_heads, hb=8):
    rows = proj.shape[0]
    tb = ctx
    per_batch = seq // tb
    ctx_blk0 = (batch * seq) // tb
    groups = n_heads // hb
    width = hb * REC_DK
    n_chunks = tb // CHUNK
    tri, level_of = _scan_constants(reverse, n_chunks)

    def row_blk(b, s):
        lat = b * per_batch + (per_batch - s if reverse else s - 1)
        return jnp.where(s == 0, ctx_blk0 + b, lat)

    return pl.pallas_call(
        functools.partial(_scan_kernel, reverse=reverse, n_chunks=n_chunks, hb=hb),
        grid=(batch, groups, 1 + per_batch),
        in_specs=[
            pl.BlockSpec((tb, width), lambda b, h, s: (row_blk(b, s), h)),
            pl.BlockSpec((tb, width), lambda b, h, s: (row_blk(b, s), groups + h)),
            pl.BlockSpec((tb, width), lambda b, h, s: (row_blk(b, s), f_sec * groups + h)),
            pl.BlockSpec((None, 1, width), lambda b, h, s: (h, 0, 0)),
            pl.BlockSpec(tri.shape, lambda b, h, s: (0, 0)),
            pl.BlockSpec(level_of.shape, lambda b, h, s: (0, 0)),
        ],
        out_specs=pl.BlockSpec((tb, width), lambda b, h, s: (row_blk(b, s), h)),
        out_shape=jax.ShapeDtypeStruct((rows, n_heads * REC_DK), BF16),
        scratch_shapes=[pltpu.VMEM((hb, REC_DK, REC_DK), F32),
                        pltpu.VMEM((tb, width), F32), pltpu.VMEM((tb, width), F32), pltpu.VMEM((tb, width), F32),
                        pltpu.VMEM((N_LEVELS + 1, tb, width), BF16), pltpu.VMEM((N_LEVELS + 1, tb, width), BF16),
                        pltpu.VMEM((tb, width), F32), pltpu.VMEM((n_chunks * hb, REC_DK, REC_DK), F32),
                        pltpu.VMEM((n_chunks * hb, CHUNK, CHUNK), F32)],
        compiler_params=_cparams("arbitrary", "arbitrary", "arbitrary"),
        name="hgrn_scan_bw" if reverse else "hgrn_scan_fw",
    )(proj, proj, proj, lb_dir.reshape(groups, 1, width), tri, level_of)


def _readout_kernel(of_ref, ob_ref, g_ref, gain_ref, o_ref, *, n_heads):
    gain = gain_ref[...]
    for hh in range(n_heads):
        sl = slice(hh * REC_DK, (hh + 1) * REC_DK)
        o = of_ref[:, sl].astype(F32) + ob_ref[:, sl].astype(F32)
        ms = jnp.mean(o * o, axis=-1, keepdims=True)
        g = g_ref[:, sl].astype(F32)
        o_ref[:, sl] = (o * lax.rsqrt(ms + EPS) * gain * (g * jax.nn.sigmoid(g))).astype(o_ref.dtype)


def hgrn_readout(o_fw, o_bw, proj, o_gain, gate_sec, n_heads):
    rows, d = o_fw.shape
    tm = ROW_TILE
    return pl.pallas_call(
        functools.partial(_readout_kernel, n_heads=n_heads),
        grid=(rows // tm,),
        in_specs=[pl.BlockSpec((tm, d), lambda i: (i, 0)),
                  pl.BlockSpec((tm, d), lambda i: (i, 0)),
                  pl.BlockSpec((tm, d), lambda i: (i, gate_sec)),
                  pl.BlockSpec((1, REC_DK), lambda i: (0, 0))],
        out_specs=pl.BlockSpec((tm, d), lambda i: (i, 0)),
        out_shape=jax.ShapeDtypeStruct((rows, d), BF16),
        compiler_params=_cparams("arbitrary"),
        name="hgrn_readout",
    )(o_fw, o_bw, proj, o_gain.reshape(1, REC_DK))


def _router_kernel(y_ref, w_ref, o_ref):
    logits = jnp.dot(y_ref[...], w_ref[...], preferred_element_type=F32, precision=lax.Precision.HIGHEST)
    lane = lax.broadcasted_iota(jnp.int32, logits.shape, 1)
    neg = -jnp.inf
    logits = jnp.where(lane < N_EXPERTS, logits, neg)
    m1 = jnp.max(logits, axis=-1, keepdims=True)
    i1 = jnp.min(jnp.where(logits == m1, lane, LANES), axis=-1, keepdims=True)
    rest = jnp.where(lane == i1, neg, logits)
    m2 = jnp.max(rest, axis=-1, keepdims=True)
    i2 = jnp.min(jnp.where(rest == m2, lane, LANES), axis=-1, keepdims=True)
    e = jnp.exp(m2 - m1)
    w1 = 1.0 / (1.0 + e)
    w2 = e / (1.0 + e)
    out = jnp.where(lane == 0, i1.astype(F32),
                    jnp.where(lane == 1, i2.astype(F32),
                              jnp.where(lane == 2, w1, jnp.where(lane == 3, w2, 0.0))))
    o_ref[...] = out


def moe_router(y, w_router):
    rows, d = y.shape
    tm = ROW_TILE
    w_pad = jnp.zeros((d, LANES), F32).at[:, :N_EXPERTS].set(w_router)
    return pl.pallas_call(
        _router_kernel,
        grid=(rows // tm,),
        in_specs=[pl.BlockSpec((tm, d), lambda i: (i, 0)), pl.BlockSpec((d, LANES), lambda i: (0, 0))],
        out_specs=pl.BlockSpec((tm, LANES), lambda i: (i, 0)),
        out_shape=jax.ShapeDtypeStruct((rows, LANES), F32),
        compiler_params=_cparams("arbitrary"),
        name="moe_router",
    )(y, w_pad)


def _row_copy(src_ref, dst_ref, sem, src_row, dst_row):
    return pltpu.make_async_copy(src_ref.at[pl.ds(src_row, 1), :], dst_ref.at[pl.ds(dst_row, 1), :], sem)


def _start_row_gather(idx_ref, src_ref, dst_ref, sem):
    def start(i, carry):
        for prio in range(2):
            r = 2 * i + prio
            _row_copy(src_ref, dst_ref, sem, idx_ref[0, r], r).start(priority=prio)
        return carry

    lax.fori_loop(0, dst_ref.shape[0] // 2, start, 0, unroll=4)


def _wait_row_gather(src_ref, dst_ref, sem):
    def wait(r, carry):
        _row_copy(src_ref, dst_ref, sem, 0, r).wait()
        return carry

    lax.fori_loop(0, dst_ref.shape[0], wait, 0, unroll=8)


def _start_rows_unrolled(idx_ref, src_ref, dst_ref, sem, first, count):
    for r in range(count):
        row = first + r
        _row_copy(src_ref, dst_ref, sem, idx_ref[0, row], row).start(priority=r % 2)


def _expert_up_kernel(te_ref, ta_ref, idx_ref, idx_next_ref, y_ref, w1_ref, w3_ref, o_ref, x_ref, sem, *, n_tiles):
    i = pl.program_id(0)
    j = pl.program_id(1)
    slot = i % 2

    @pl.when(j == 0)
    def _():
        @pl.when(i == 0)
        def _():
            _start_row_gather(idx_ref, y_ref, x_ref.at[0], sem.at[0])

        @pl.when(i + 1 < n_tiles)
        def _():
            _start_row_gather(idx_next_ref, y_ref, x_ref.at[1 - slot], sem.at[1 - slot])

        _wait_row_gather(y_ref, x_ref.at[slot], sem.at[slot])

    @pl.when(ta_ref[i] > 0)
    def _():
        x = x_ref[slot].astype(BF16)
        u = jnp.dot(x, w1_ref[...], preferred_element_type=F32)
        g = jnp.dot(x, w3_ref[...], preferred_element_type=F32)
        o_ref[...] = (u * jax.nn.sigmoid(u) * g).astype(o_ref.dtype)

    @pl.when(ta_ref[i] == 0)
    def _():
        o_ref[...] = jnp.zeros(o_ref.shape, o_ref.dtype)


def expert_up(y, token_of_slot, w1, w3, tile_expert, tile_active, tn=1024):
    p = token_of_slot.shape[0]
    d = y.shape[1]
    f = w1.shape[2]
    tm = ROW_TILE
    n_tiles = p // tm
    idx = token_of_slot.reshape(n_tiles, 1, tm)
    return pl.pallas_call(
        functools.partial(_expert_up_kernel, n_tiles=n_tiles),
        grid_spec=pltpu.PrefetchScalarGridSpec(
            num_scalar_prefetch=2,
            grid=(n_tiles, f // tn),
            in_specs=[pl.BlockSpec((None, 1, tm), lambda i, j, te, ta: (i, 0, 0), memory_space=pltpu.SMEM),
                      pl.BlockSpec((None, 1, tm), lambda i, j, te, ta: (jnp.minimum(i + 1, n_tiles - 1), 0, 0),
                                   memory_space=pltpu.SMEM),
                      pl.BlockSpec(memory_space=pl.ANY),
                      pl.BlockSpec((None, d, tn), lambda i, j, te, ta: (te[i], 0, j)),
                      pl.BlockSpec((None, d, tn), lambda i, j, te, ta: (te[i], 0, j))],
            out_specs=pl.BlockSpec((tm, tn), lambda i, j, te, ta: (i, j)),
            scratch_shapes=[pltpu.VMEM((2, tm, d), y.dtype), pltpu.SemaphoreType.DMA((2,))],
        ),
        out_shape=jax.ShapeDtypeStruct((p, f), BF16),
        compiler_params=_cparams("arbitrary", "arbitrary"),
        name="moe_expert_up",
    )(tile_expert, tile_active, idx, idx, y, w1, w3)


def _expert_down_kernel(te_ref, ta_ref, a_ref, w_ref, o_ref, acc_ref, *, nk):
    i = pl.program_id(0)
    kk = pl.program_id(1)

    @pl.when(ta_ref[i] > 0)
    def _():
        part = jnp.dot(a_ref[...], w_ref[...], preferred_element_type=F32)

        @pl.when(kk == 0)
        def _():
            acc_ref[...] = part

        @pl.when(jnp.logical_and(kk > 0, kk < nk - 1))
        def _():
            acc_ref[...] += part

        @pl.when(kk == nk - 1)
        def _():
            o_ref[...] = acc_ref[...] + part

    @pl.when(jnp.logical_and(ta_ref[i] == 0, kk == nk - 1))
    def _():
        o_ref[...] = jnp.zeros(o_ref.shape, o_ref.dtype)


def expert_down(a, w2, tile_expert, tile_active, tk=2048):
    p, f = a.shape
    d = w2.shape[2]
    tm = ROW_TILE
    nk = f // tk
    assert nk >= 2
    return pl.pallas_call(
        functools.partial(_expert_down_kernel, nk=nk),
        grid_spec=pltpu.PrefetchScalarGridSpec(
            num_scalar_prefetch=2,
            grid=(p // tm, nk),
            in_specs=[pl.BlockSpec((tm, tk), lambda i, kk, te, ta: (i, kk)),
                      pl.BlockSpec((None, tk, d), lambda i, kk, te, ta: (te[i], kk, 0))],
            out_specs=pl.BlockSpec((tm, d), lambda i, kk, te, ta: (i, 0)),
            scratch_shapes=[pltpu.VMEM((tm, d), F32)],
        ),
        out_shape=jax.ShapeDtypeStruct((p, d), F32),
        compiler_params=_cparams("arbitrary", "arbitrary"),
        name="moe_expert_down",
    )(tile_expert, tile_active, a, w2)


def _combine_kernel(s0_ref, s1_ref, s0_next_ref, s1_next_ref, src_ref, route_ref, h_ref, gate_ref, gain_ref, o_ref,
                    buf0, buf1, sem, *, n_tiles):
    i = pl.program_id(0)
    slot = i % 2
    tile = o_ref.shape[0]

    @pl.when(i == 0)
    def _():
        _start_row_gather(s0_ref, src_ref, buf0.at[0], sem.at[0])
        _start_row_gather(s1_ref, src_ref, buf1.at[0], sem.at[2])

    _wait_row_gather(src_ref, buf0.at[slot], sem.at[slot])
    _wait_row_gather(src_ref, buf1.at[slot], sem.at[2 + slot])
    _start_rows_unrolled(s0_next_ref, src_ref, buf0.at[1 - slot], sem.at[1 - slot], 0, tile)
    _start_rows_unrolled(s1_next_ref, src_ref, buf1.at[1 - slot], sem.at[3 - slot], 0, tile)
    w0 = route_ref[:, TOP_K:TOP_K + 1]
    w1 = route_ref[:, TOP_K + 1:TOP_K + 2]
    o_ref[...] = _residual_epilogue(w0 * buf0[slot] + w1 * buf1[slot], h_ref[...], gate_ref[...], gain_ref[...])

    @pl.when(i == n_tiles - 1)
    def _():
        _wait_row_gather(src_ref, buf0.at[1 - slot], sem.at[1 - slot])
        _wait_row_gather(src_ref, buf1.at[1 - slot], sem.at[3 - slot])


def moe_combine_resnorm(expert_out, slot0, slot1, route, h, gain, mods, gate_sec, seq, batch, out_rows, tile=256):
    rows, d = h.shape
    row = functools.partial(_mod_row, tm=tile, seq=seq, batch=batch)
    n_tiles = out_rows // tile
    idx_spec = pl.BlockSpec((None, 1, tile), lambda i: (i, 0, 0), memory_space=pltpu.SMEM)
    next_spec = pl.BlockSpec((None, 1, tile), lambda i: (jnp.minimum(i + 1, n_tiles - 1), 0, 0),
                             memory_space=pltpu.SMEM)
    s0 = slot0.reshape(rows // tile, 1, tile)
    s1 = slot1.reshape(rows // tile, 1, tile)
    return pl.pallas_call(
        functools.partial(_combine_kernel, n_tiles=n_tiles),
        grid=(n_tiles,),
        in_specs=[idx_spec, idx_spec, next_spec, next_spec,
                  pl.BlockSpec(memory_space=pl.ANY),
                  pl.BlockSpec((tile, LANES), lambda i: (i, 0)),
                  pl.BlockSpec((tile, d), lambda i: (i, 0)),
                  pl.BlockSpec((None, 1, d), lambda i: (row(i), 0, gate_sec)),
                  pl.BlockSpec((1, d), lambda i: (0, 0))],
        out_specs=pl.BlockSpec((tile, d), lambda i: (i, 0)),
        out_shape=jax.ShapeDtypeStruct((out_rows, d), F32),
        scratch_shapes=[pltpu.VMEM((2, tile, d), F32), pltpu.VMEM((2, tile, d), F32),
                        pltpu.SemaphoreType.DMA((4,))],
        compiler_params=_cparams("arbitrary"),
        name="moe_combine_resnorm",
    )(s0, s1, s0, s1, expert_out, route, h, mods, gain.reshape(1, d))


def _routing_plan(route, tm):
    rows = route.shape[0]
    experts = route[:, :TOP_K].astype(jnp.int32).reshape(-1)
    onehot = (experts[:, None] == jnp.arange(N_EXPERTS)[None, :]).astype(jnp.int32)
    rank = jnp.sum((jnp.cumsum(onehot, axis=0) - onehot) * onehot, axis=1)
    counts = jnp.sum(onehot, axis=0)
    tiles_per = (counts + tm - 1) // tm
    tile_end = jnp.cumsum(tiles_per)
    start = (tile_end - tiles_per) * tm
    slot = jnp.sum(onehot * start[None, :], axis=1) + rank
    n_slots = rows * TOP_K + N_EXPERTS * tm
    n_tiles = n_slots // tm
    token_of_slot = jnp.zeros((n_slots,), jnp.int32).at[slot].set(jnp.arange(rows * TOP_K, dtype=jnp.int32) // TOP_K)
    tile_ids = jnp.arange(n_tiles, dtype=jnp.int32)
    tile_expert = jnp.minimum(jnp.sum((tile_ids[:, None] >= tile_end[None, :]).astype(jnp.int32), axis=1),
                              N_EXPERTS - 1)
    tile_active = (tile_ids < tile_end[-1]).astype(jnp.int32)
    slots = slot.reshape(rows, TOP_K).astype(jnp.int32)
    return token_of_slot, tile_expert, tile_active, slots[:, 0], slots[:, 1]


def moe_ffn_resnorm(y, h, w_router, w1, w3, w2, gain, mods, gate_sec, seq, batch, out_rows):
    route = moe_router(y, w_router)
    token_of_slot, tile_expert, tile_active, slot0, slot1 = _routing_plan(route, ROW_TILE)
    hid = expert_up(y, token_of_slot, w1, w3, tile_expert, tile_active)
    out = expert_down(hid, w2, tile_expert, tile_active)
    return moe_combine_resnorm(out, slot0, slot1, route, h, gain, mods, gate_sec, seq, batch, out_rows)


def _lower_bounds(logits):
    p = jax.nn.softmax(logits.astype(F32), axis=0)
    return jnp.cumsum(p, axis=0) - p[0]


def kernel(x, c, ctx, c_ctx, w_mod, b_mod, norm_gains, attn_w_in, attn_w_o, attn_q_gain, attn_k_gain,
           rec_w_in, rec_w_o, rec_lb_logits, rec_o_gain, ffn_w1, ffn_w3, ffn_w2,
           moe_w_router, moe_w1, moe_w3, moe_w2):
    batch, seq, d = x.shape
    ctx_len = ctx.shape[1]
    depth = w_mod.shape[0]
    n_lat = batch * seq
    n_heads = d // REC_DK
    assert seq % ROW_TILE == 0 and (batch * ctx_len) % ROW_TILE == 0 and batch + 1 <= MOD_ROWS

    hg = jnp.concatenate([x.reshape(n_lat, d), ctx.reshape(batch * ctx_len, d)], axis=0)
    cond = jnp.zeros((MOD_ROWS, d), F32).at[:batch].set(c).at[batch].set(c_ctx)
    mods_all = modulation_table(cond, w_mod, b_mod)
    lower = _lower_bounds(rec_lb_logits)
    tables = rope_tables(seq, ROW_TILE)
    tq = min(256, ctx_len)
    tk = min(2048, seq // 4)

    for layer in range(depth):
        j = layer // 2
        ng = norm_gains[layer]
        mods = mods_all[layer]
        if layer % 2 == 0:
            qkv = norm_matmul(hg, ng[0], mods, 0, 1, seq, batch, attn_w_in[j].astype(BF16), BF16, tn=1024)
            q, k, v_ext = qk_prep(qkv, attn_q_gain[j], attn_k_gain[j], tables, seq, n_lat)
            mix = flash_attention(q, k, v_ext, seq, ctx_len, batch, tq, tk)
            hg = matmul_resnorm(mix, attn_w_o[j].astype(BF16), hg, ng[1], mods, 2, seq, batch, tk=d)
        else:
            proj = norm_matmul(hg, ng[0], mods, 0, 1, seq, batch, rec_w_in[j].astype(BF16), BF16, tn=2048)
            o_fw = hgrn_scan(proj, lower[layer, 0], 2, False, seq, ctx_len, batch, n_heads)
            o_bw = hgrn_scan(proj, lower[layer, 1], 3, True, seq, ctx_len, batch, n_heads)
            mix = hgrn_readout(o_fw, o_bw, proj, rec_o_gain[j], 4, n_heads)
            hg = matmul_resnorm(mix, rec_w_o[j].astype(BF16), hg, ng[1], mods, 2, seq, batch, tk=d)
        if layer % 2 == 0:
            hid = norm_swiglu_up(hg, ng[2], mods, 3, 4, seq, batch, ffn_w1[j].astype(BF16), ffn_w3[j].astype(BF16))
            f = hid.shape[1]
            hg = matmul_resnorm(hid, ffn_w2[j].astype(BF16), hg, ng[3], mods, 5, seq, batch, tk=f // 2)
        else:
            y = norm_modulate(hg, ng[2], mods, 3, 4, seq, batch, F32)
            out_rows = n_lat if layer == depth - 1 else hg.shape[0]
            hg = moe_ffn_resnorm(y, hg, moe_w_router[j], moe_w1[j].astype(BF16), moe_w3[j].astype(BF16),
                                 moe_w2[j].astype(BF16), ng[3], mods, 5, seq, batch, out_rows)
    return hg[:n_lat].reshape(batch, seq, d)
```

```python
import functools

import numpy as np
import jax
import jax.numpy as jnp
from jax import lax
from jax.experimental import pallas as pl
from jax.experimental.pallas import tpu as pltpu

F32 = jnp.float32
BF16 = jnp.bfloat16

HEAD_DIM = 128
N_KV_HEADS = 4
GROUP = 4
GRID_W = 64
ROPE_THETA = 10000.0
AXIS_ROT_DIM = HEAD_DIM // 2
REC_DK = 128
N_EXPERTS = 8
TOP_K = 2
N_MOD = 6
EPS = 1e-6
CHUNK = 64
N_LEVELS = 6

LANES = 128
VMEM_LIMIT = 56 * 2**20

ROW_TILE = 512
MOD_ROWS = 8


def _cparams(*sem):
    return pltpu.CompilerParams(dimension_semantics=sem, vmem_limit_bytes=VMEM_LIMIT)


def _mod_row(i, tm, seq, batch):
    return jnp.minimum((i * tm) // seq, batch)


def _mod_kernel(c_ref, w_ref, b_ref, o_ref):
    c = c_ref[...]
    s = (c * jax.nn.sigmoid(c)).astype(BF16)
    o_ref[...] = jnp.dot(s, w_ref[...].astype(BF16), preferred_element_type=F32) + b_ref[...]


def modulation_table(cond, w_mod, b_mod, tn=1024):
    depth, d, n = w_mod.shape
    out = pl.pallas_call(
        _mod_kernel,
        grid=(depth, n // tn),
        in_specs=[
            pl.BlockSpec((MOD_ROWS, d), lambda l, j: (0, 0)),
            pl.BlockSpec((None, d, tn), lambda l, j: (l, 0, j)),
            pl.BlockSpec((None, 1, tn), lambda l, j: (l, 0, j)),
        ],
        out_specs=pl.BlockSpec((None, MOD_ROWS, tn), lambda l, j: (l, 0, j)),
        out_shape=jax.ShapeDtypeStruct((depth, MOD_ROWS, n), F32),
        compiler_params=_cparams("arbitrary", "arbitrary"),
        name="modulation_table",
    )(cond, w_mod, b_mod.reshape(depth, 1, n))
    return out.reshape(depth, MOD_ROWS, 1, n)


def _norm_mod_kernel(h_ref, g_ref, sh_ref, sc_ref, o_ref):
    x = h_ref[...]
    ms = jnp.mean(x * x, axis=-1, keepdims=True)
    y = x * lax.rsqrt(ms + EPS) * g_ref[...]
    o_ref[...] = (y * (1.0 + sc_ref[...]) + sh_ref[...]).astype(o_ref.dtype)


def norm_modulate(h, gain, mods, shift_sec, scale_sec, seq, batch, out_dtype):
    rows, d = h.shape
    tm = ROW_TILE
    row = functools.partial(_mod_row, tm=tm, seq=seq, batch=batch)
    return pl.pallas_call(
        _norm_mod_kernel,
        grid=(rows // tm,),
        in_specs=[
            pl.BlockSpec((tm, d), lambda i: (i, 0)),
            pl.BlockSpec((1, d), lambda i: (0, 0)),
            pl.BlockSpec((None, 1, d), lambda i: (row(i), 0, shift_sec)),
            pl.BlockSpec((None, 1, d), lambda i: (row(i), 0, scale_sec)),
        ],
        out_specs=pl.BlockSpec((tm, d), lambda i: (i, 0)),
        out_shape=jax.ShapeDtypeStruct((rows, d), out_dtype),
        compiler_params=_cparams("arbitrary"),
        name="norm_modulate",
    )(h, gain.reshape(1, d), mods, mods)


def _normed_lhs(h_ref, g_ref, sh_ref, sc_ref, y_ref):
    @pl.when(pl.program_id(1) == 0)
    def _():
        _norm_mod_kernel(h_ref, g_ref, sh_ref, sc_ref, y_ref)


def _norm_specs(tm, d, row, shift_sec, scale_sec):
    return [pl.BlockSpec((tm, d), lambda i, j: (i, 0)),
            pl.BlockSpec((1, d), lambda i, j: (0, 0)),
            pl.BlockSpec((None, 1, d), lambda i, j: (row(i), 0, shift_sec)),
            pl.BlockSpec((None, 1, d), lambda i, j: (row(i), 0, scale_sec))]


def _norm_mm_kernel(h_ref, g_ref, sh_ref, sc_ref, b_ref, o_ref, y_ref):
    _normed_lhs(h_ref, g_ref, sh_ref, sc_ref, y_ref)
    o_ref[...] = jnp.dot(y_ref[...], b_ref[...], preferred_element_type=F32).astype(o_ref.dtype)


def norm_matmul(h, gain, mods, shift_sec, scale_sec, seq, batch, b, out_dtype, tn):
    m, d = h.shape
    n = b.shape[1]
    tm = ROW_TILE
    row = functools.partial(_mod_row, tm=tm, seq=seq, batch=batch)
    return pl.pallas_call(
        _norm_mm_kernel,
        grid=(m // tm, n // tn),
        in_specs=_norm_specs(tm, d, row, shift_sec, scale_sec) + [pl.BlockSpec((d, tn), lambda i, j: (0, j))],
        out_specs=pl.BlockSpec((tm, tn), lambda i, j: (i, j)),
        out_shape=jax.ShapeDtypeStruct((m, n), out_dtype),
        scratch_shapes=[pltpu.VMEM((tm, d), BF16)],
        compiler_params=_cparams("arbitrary", "arbitrary"),
        name="norm_matmul",
    )(h, gain.reshape(1, d), mods, mods, b)


def _norm_swiglu_up_kernel(h_ref, g_ref, sh_ref, sc_ref, w1_ref, w3_ref, o_ref, y_ref):
    _normed_lhs(h_ref, g_ref, sh_ref, sc_ref, y_ref)
    a = y_ref[...]
    u = jnp.dot(a, w1_ref[...], preferred_element_type=F32)
    g = jnp.dot(a, w3_ref[...], preferred_element_type=F32)
    o_ref[...] = (u * jax.nn.sigmoid(u) * g).astype(o_ref.dtype)


def norm_swiglu_up(h, gain, mods, shift_sec, scale_sec, seq, batch, w1, w3, tn=1408):
    m, d = h.shape
    n = w1.shape[1]
    tm = ROW_TILE
    row = functools.partial(_mod_row, tm=tm, seq=seq, batch=batch)
    return pl.pallas_call(
        _norm_swiglu_up_kernel,
        grid=(m // tm, n // tn),
        in_specs=_norm_specs(tm, d, row, shift_sec, scale_sec) + [pl.BlockSpec((d, tn), lambda i, j: (0, j)),
                                                                  pl.BlockSpec((d, tn), lambda i, j: (0, j))],
        out_specs=pl.BlockSpec((tm, tn), lambda i, j: (i, j)),
        out_shape=jax.ShapeDtypeStruct((m, n), BF16),
        scratch_shapes=[pltpu.VMEM((tm, d), BF16)],
        compiler_params=_cparams("arbitrary", "arbitrary"),
        name="norm_swiglu_up",
    )(h, gain.reshape(1, d), mods, mods, w1, w3)


def _residual_epilogue(r, h, gate, gain):
    ms = jnp.mean(r * r, axis=-1, keepdims=True)
    return h + gate * (r * lax.rsqrt(ms + EPS) * gain)


def _mm_resnorm_kernel(a_ref, b_ref, h_ref, gate_ref, gain_ref, o_ref, acc_ref, *, nk):
    kk = pl.program_id(1)
    part = jnp.dot(a_ref[...], b_ref[...], preferred_element_type=F32)

    if nk == 1:
        o_ref[...] = _residual_epilogue(part, h_ref[...], gate_ref[...], gain_ref[...])
        return

    @pl.when(kk == 0)
    def _():
        acc_ref[...] = part

    @pl.when(kk > 0)
    def _():
        acc_ref[...] += part

    @pl.when(kk == nk - 1)
    def _():
        o_ref[...] = _residual_epilogue(acc_ref[...], h_ref[...], gate_ref[...], gain_ref[...])


def matmul_resnorm(a, b, h, gain, mods, gate_sec, seq, batch, tk):
    m, k = a.shape
    d = b.shape[1]
    tm = ROW_TILE
    nk = k // tk
    row = functools.partial(_mod_row, tm=tm, seq=seq, batch=batch)
    return pl.pallas_call(
        functools.partial(_mm_resnorm_kernel, nk=nk),
        grid=(m // tm, nk),
        in_specs=[
            pl.BlockSpec((tm, tk), lambda i, kk: (i, kk)),
            pl.BlockSpec((tk, d), lambda i, kk: (kk, 0)),
            pl.BlockSpec((tm, d), lambda i, kk: (i, 0)),
            pl.BlockSpec((None, 1, d), lambda i, kk: (row(i), 0, gate_sec)),
            pl.BlockSpec((1, d), lambda i, kk: (0, 0)),
        ],
        out_specs=pl.BlockSpec((tm, d), lambda i, kk: (i, 0)),
        out_shape=jax.ShapeDtypeStruct((m, d), F32),
        scratch_shapes=[pltpu.VMEM((tm, d), F32)],
        compiler_params=_cparams("arbitrary", "arbitrary"),
        name="matmul_resnorm",
    )(a, b, h, mods, gain.reshape(1, d))


def rope_tables(seq, tile):
    rows = seq // GRID_W
    row = jnp.repeat(jnp.arange(rows, dtype=F32), GRID_W)
    col = jnp.tile(jnp.arange(GRID_W, dtype=F32), rows)
    inv_freq = ROPE_THETA ** (-jnp.arange(0, AXIS_ROT_DIM, 2, dtype=F32) / AXIS_ROT_DIM)
    ar, ac = row[:, None] * inv_freq, col[:, None] * inv_freq
    zeros = jnp.zeros_like(ar)
    cos = jnp.concatenate([jnp.cos(ar), jnp.cos(ar), jnp.cos(ac), jnp.cos(ac)], axis=-1)
    sin_lo = jnp.concatenate([-jnp.sin(ar), zeros, -jnp.sin(ac), zeros], axis=-1)
    sin_hi = jnp.concatenate([zeros, jnp.sin(ar), zeros, jnp.sin(ac)], axis=-1)
    pad0 = jnp.zeros((tile, HEAD_DIM), F32)
    return (jnp.concatenate([cos, jnp.ones((tile, HEAD_DIM), F32)], axis=0),
            jnp.concatenate([sin_lo, pad0], axis=0),
            jnp.concatenate([sin_hi, pad0], axis=0))


def _qk_prep_kernel(qkv_ref, qg_ref, kg_ref, cos_ref, slo_ref, shi_ref, q_ref, k_ref, v_ref, *, n_q, n_k, q_scale):
    cos, slo, shi = cos_ref[...], slo_ref[...], shi_ref[...]
    quarter = AXIS_ROT_DIM // 2

    def head(x, gain):
        x = x.astype(F32)
        ms = jnp.mean(x * x, axis=-1, keepdims=True)
        xn = x * lax.rsqrt(ms + EPS) * gain
        return (xn * cos + pltpu.roll(xn, HEAD_DIM - quarter, axis=1) * slo
                + pltpu.roll(xn, quarter, axis=1) * shi)

    for hh in range(n_q):
        sl = slice(hh * HEAD_DIM, (hh + 1) * HEAD_DIM)
        q_ref[:, sl] = (head(qkv_ref[:, sl], qg_ref[...]) * q_scale).astype(q_ref.dtype)
    for hh in range(n_k):
        src = slice((n_q + hh) * HEAD_DIM, (n_q + hh + 1) * HEAD_DIM)
        k_ref[:, hh * HEAD_DIM:(hh + 1) * HEAD_DIM] = head(qkv_ref[:, src], kg_ref[...]).astype(k_ref.dtype)
    for hh in range(n_k):
        src = slice((n_q + n_k + hh) * HEAD_DIM, (n_q + n_k + hh + 1) * HEAD_DIM)
        v_ref[:, 2 * hh * HEAD_DIM:(2 * hh + 1) * HEAD_DIM] = qkv_ref[:, src]
        v_ref[:, (2 * hh + 1) * HEAD_DIM:(2 * hh + 2) * HEAD_DIM] = jnp.ones((v_ref.shape[0], HEAD_DIM), v_ref.dtype)


def qk_prep(qkv, q_gain, k_gain, tables, seq, n_lat_rows):
    rows = qkv.shape[0]
    tm = ROW_TILE
    n_q = 4 * N_KV_HEADS
    per_batch = seq // tm
    n_lat_tiles = n_lat_rows // tm
    tab = lambda i: (jnp.where(i < n_lat_tiles, i % per_batch, per_batch), 0)
    q_scale = HEAD_DIM ** -0.5 * float(np.log2(np.e))
    return pl.pallas_call(
        functools.partial(_qk_prep_kernel, n_q=n_q, n_k=N_KV_HEADS, q_scale=q_scale),
        grid=(rows // tm,),
        in_specs=[
            pl.BlockSpec((tm, (n_q + 2 * N_KV_HEADS) * HEAD_DIM), lambda i: (i, 0)),
            pl.BlockSpec((1, HEAD_DIM), lambda i: (0, 0)),
            pl.BlockSpec((1, HEAD_DIM), lambda i: (0, 0)),
            pl.BlockSpec((tm, HEAD_DIM), tab),
            pl.BlockSpec((tm, HEAD_DIM), tab),
            pl.BlockSpec((tm, HEAD_DIM), tab),
        ],
        out_specs=[pl.BlockSpec((tm, n_q * HEAD_DIM), lambda i: (i, 0)),
                   pl.BlockSpec((tm, N_KV_HEADS * HEAD_DIM), lambda i: (i, 0)),
                   pl.BlockSpec((tm, 2 * N_KV_HEADS * HEAD_DIM), lambda i: (i, 0))],
        out_shape=[jax.ShapeDtypeStruct((rows, n_q * HEAD_DIM), BF16),
                   jax.ShapeDtypeStruct((rows, N_KV_HEADS * HEAD_DIM), BF16),
                   jax.ShapeDtypeStruct((rows, 2 * N_KV_HEADS * HEAD_DIM), BF16)],
        compiler_params=_cparams("arbitrary"),
        name="qk_prep",
    )(qkv, q_gain.reshape(1, HEAD_DIM), k_gain.reshape(1, HEAD_DIM), *tables)


def _flash_kernel(q_ref, k_ref, v_ref, kc_ref, vc_ref, o_ref, m_ref, acc_ref,
                  s0_ref, s1_ref, p0_ref, p1_ref, mx0_ref, mx1_ref, alpha0_ref, alpha1_ref, *,
                  n_lat_tiles, nkv, tq, tk, ctx):
    qi = pl.program_id(1)
    r = GROUP * tq
    rc = min(128, r)
    s_ref, p_ref = (s0_ref, s1_ref), (p0_ref, p1_ref)
    mx_ref, alpha_ref = (mx0_ref, mx1_ref), (alpha0_ref, alpha1_ref)
    q = jnp.concatenate([q_ref[:, g * HEAD_DIM:(g + 1) * HEAD_DIM] for g in range(GROUP)], axis=0)
    nt = (((1,), (1,)), ((), ()))

    def scores(k):
        return lax.dot_general(q, k, nt, preferred_element_type=F32)

    def times_values(p, v):
        return jnp.dot(p, v, preferred_element_type=F32)

    def kv_rows(j):
        start = j * tk
        return pl.ds(start if isinstance(j, int) else pl.multiple_of(start, tk), tk)

    def put_scores(slot, k):
        width = k.shape[0]
        s = scores(k)
        s_ref[slot][:, 0:width] = s
        mx = s[:, 0:LANES]
        for c0 in range(LANES, width, LANES):
            mx = jnp.maximum(mx, s[:, c0:c0 + LANES])
        mx_ref[slot][...] = mx

    def softmax_stage(slot, width, first):
        for r0 in range(0, r, rc):
            rows = slice(r0, r0 + rc)
            row_max = jnp.max(mx_ref[slot][rows, :], axis=-1, keepdims=True)
            if first:
                m_new = jnp.broadcast_to(row_max, (rc, LANES))
            else:
                m_prev = m_ref[rows, :]
                m_new = jnp.maximum(m_prev, row_max)
                alpha_ref[slot][rows, :] = jnp.exp2(m_prev - m_new)
            m_ref[rows, :] = m_new
            for c0 in range(0, width, LANES):
                cols = slice(c0, c0 + LANES)
                p_ref[slot][rows, cols] = jnp.exp2(s_ref[slot][rows, cols] - m_new).astype(BF16)

    def accumulate(slot, v):
        pv = times_values(p_ref[slot][...], v)
        for r0 in range(0, r, rc):
            rows = slice(r0, r0 + rc)
            a = alpha_ref[slot][rows, :]
            for c0 in (0, LANES):
                cols = slice(c0, c0 + LANES)
                acc_ref[rows, cols] = a * acc_ref[rows, cols] + pv[rows, cols]

    def stage(j, slot, last):
        if not last:
            put_scores(1 - slot, k_ref[kv_rows(j + 1), :])
        softmax_stage(slot, tk, False)
        accumulate(slot, v_ref[kv_rows(j), :])

    put_scores(0, kc_ref[...])
    softmax_stage(0, ctx, True)
    acc_ref[...] = times_values(p_ref[0][:, 0:ctx], vc_ref[...])

    @pl.when(qi < n_lat_tiles)
    def _():
        put_scores(0, k_ref[kv_rows(0), :])

        def pair(t, carry):
            stage(2 * t, 0, False)
            stage(2 * t + 1, 1, False)
            return carry

        lax.fori_loop(0, (nkv - 2) // 2, pair, 0)
        stage(nkv - 2, 0, False)
        stage(nkv - 1, 1, True)

    for g in range(GROUP):
        rows = slice(g * tq, (g + 1) * tq)
        o = acc_ref[rows, 0:HEAD_DIM] / acc_ref[rows, HEAD_DIM:2 * HEAD_DIM]
        o_ref[:, g * HEAD_DIM:(g + 1) * HEAD_DIM] = o.astype(o_ref.dtype)


def flash_attention(q, k, v_ext, seq, ctx, batch, tq, tk):
    rows = q.shape[0]
    n_q_heads = GROUP * N_KV_HEADS
    per_batch_q = seq // tq
    n_lat_tiles = batch * per_batch_q
    n_ctx_tiles = batch * (ctx // tq)
    nkv = seq // tk
    assert nkv >= 2 and nkv % 2 == 0 and ctx <= tk and ctx % LANES == 0
    ctx_blk0 = (batch * seq) // ctx
    r = GROUP * tq

    def batch_of(qi):
        return jnp.where(qi < n_lat_tiles, qi // per_batch_q, (qi - n_lat_tiles) // (ctx // tq))

    grid = (N_KV_HEADS, n_lat_tiles + n_ctx_tiles)
    return pl.pallas_call(
        functools.partial(_flash_kernel, n_lat_tiles=n_lat_tiles, nkv=nkv, tq=tq, tk=tk, ctx=ctx),
        grid=grid,
        in_specs=[
            pl.BlockSpec((tq, GROUP * HEAD_DIM), lambda h, qi: (qi, h)),
            pl.BlockSpec((seq, HEAD_DIM), lambda h, qi: (batch_of(qi), h)),
            pl.BlockSpec((seq, 2 * HEAD_DIM), lambda h, qi: (batch_of(qi), h)),
            pl.BlockSpec((ctx, HEAD_DIM), lambda h, qi: (ctx_blk0 + batch_of(qi), h)),
            pl.BlockSpec((ctx, 2 * HEAD_DIM), lambda h, qi: (ctx_blk0 + batch_of(qi), h)),
        ],
        out_specs=pl.BlockSpec((tq, GROUP * HEAD_DIM), lambda h, qi: (qi, h)),
        out_shape=jax.ShapeDtypeStruct((rows, n_q_heads * HEAD_DIM), BF16),
        scratch_shapes=[pltpu.VMEM((r, LANES), F32), pltpu.VMEM((r, 2 * HEAD_DIM), F32),
                        pltpu.VMEM((r, tk), F32), pltpu.VMEM((r, tk), F32),
                        pltpu.VMEM((r, tk), BF16), pltpu.VMEM((r, tk), BF16),
                        pltpu.VMEM((r, LANES), F32), pltpu.VMEM((r, LANES), F32),
                        pltpu.VMEM((r, LANES), F32), pltpu.VMEM((r, LANES), F32)],
        compiler_params=_cparams("arbitrary", "arbitrary"),
        name="flash_attention",
    )(q, k, v_ext, k, v_ext)


def _scan_constants(reverse, n_chunks):
    t = np.arange(CHUNK)[:, None]
    u = np.arange(CHUNK)[None, :]
    tri = (u >= t) if reverse else (u <= t)
    level_of = np.where(t == u, 0, -1)
    for lvl in range(N_LEVELS):
        b = CHUNK >> (lvl + 1)
        same = (t & ~(2 * b - 1)) == (u & ~(2 * b - 1))
        t_hi, s_hi = (t & b) != 0, (u & b) != 0
        level_of = np.where(same & (~t_hi & s_hi if reverse else t_hi & ~s_hi), lvl + 1, level_of)
    return jnp.asarray(np.kron(np.eye(n_chunks), tri), BF16), jnp.asarray(level_of, jnp.int32)


def _keep_bf16_bits(x):
    bits = lax.bitcast_convert_type(x, jnp.int32) & jnp.int32(-65536)
    return lax.bitcast_convert_type(bits, F32)


def _block_rows(x, block, off):
    rows, width = x.shape
    if block >= 8:
        parts = [jnp.broadcast_to(x[base + off:base + off + 1], (block, width)) for base in range(0, rows, block)]
        return parts[0] if len(parts) == 1 else jnp.concatenate(parts, axis=0)
    within = lax.broadcasted_iota(jnp.int32, x.shape, 0) & (block - 1)
    out = x
    for r in range(block):
        if r != off:
            out = jnp.where(within == r, pltpu.roll(x, (r - off) % rows, axis=0), out)
    return out


def _scan_kernel(q_ref, v_ref, f_ref, lb_ref, tri_ref, lvl_ref, o_ref,
                 st_ref, q32_ref, k32_ref, cb_ref, ql_ref, kl_ref, oi_ref, kv_ref, a_ref, *, reverse, n_chunks, hb):
    step = pl.program_id(2)

    @pl.when(step == 0)
    def _():
        st_ref[...] = jnp.zeros(st_ref.shape, F32)

    width = hb * REC_DK
    nt = (((1,), (1,)), ((), ()))
    tn = (((0,), (0,)), ((), ()))

    lb = lb_ref[...]
    qr = q_ref[...].astype(F32)
    q = (qr * (0.5 * REC_DK ** -0.5)) * (1.0 + jnp.tanh(0.5 * qr))
    fg = (0.5 + 0.5 * lb) + (0.5 - 0.5 * lb) * jnp.tanh(0.5 * f_ref[...].astype(F32))
    k = 1.0 - fg
    lf = jnp.log2(fg)
    hi = _keep_bf16_bits(lf)
    rest = lf - hi
    mid = _keep_bf16_bits(rest)
    lo = rest - mid
    sums = jnp.dot(tri_ref[...], jnp.concatenate([hi.astype(BF16), mid.astype(BF16), lo.astype(BF16)], axis=1),
                   preferred_element_type=F32)
    q32_ref[...] = q
    k32_ref[...] = k
    cb_ref[...] = sums[:, 0:width] + sums[:, width:2 * width] + sums[:, 2 * width:3 * width]
    ql_ref[0] = q.astype(BF16)
    kl_ref[0] = k.astype(BF16)

    tiles = [(c * hb + hh, slice(c * CHUNK, (c + 1) * CHUNK), slice(hh * REC_DK, (hh + 1) * REC_DK))
             for c in range(n_chunks) for hh in range(hb)]

    total = _block_rows(cb_ref[...], CHUNK, 0 if reverse else CHUNK - 1)
    qd = (q32_ref[...] * jnp.exp2(cb_ref[...])).astype(BF16)
    kd = (k32_ref[...] * jnp.exp2(total - cb_ref[...])).astype(BF16)
    for idx, rows, cols in tiles:
        kv_ref[idx] = lax.dot_general(v_ref[rows, cols], kd[rows, cols], tn, preferred_element_type=F32)

    level_of = lvl_ref[...]
    for lvl in range(N_LEVELS + 1):
        if lvl > 0:
            b = CHUNK >> lvl
            d = cb_ref[...] - _block_rows(cb_ref[...], 2 * b, b if reverse else b - 1)
            ql_ref[lvl] = (q32_ref[...] * jnp.exp2(jnp.minimum(d, 0.0))).astype(BF16)
            kl_ref[lvl] = (k32_ref[...] * jnp.exp2(jnp.minimum(-d, 0.0))).astype(BF16)
        for idx, rows, cols in tiles:
            pairs = lax.dot_general(ql_ref[lvl, rows, cols], kl_ref[lvl, rows, cols], nt,
                                    preferred_element_type=F32)
            a_ref[idx] = jnp.where(level_of == lvl, pairs, 0.0 if lvl == 0 else a_ref[idx])

    for idx, rows, cols in tiles:
        oi_ref[rows, cols] = jnp.dot(a_ref[idx].astype(BF16), v_ref[rows, cols], preferred_element_type=F32)

    decay = jnp.exp2(total)
    for c in (range(n_chunks - 1, -1, -1) if reverse else range(n_chunks)):
        rows = slice(c * CHUNK, (c + 1) * CHUNK)
        for hh in range(hb):
            cols = slice(hh * REC_DK, (hh + 1) * REC_DK)
            st = st_ref[hh]
            o = oi_ref[rows, cols] + lax.dot_general(qd[rows, cols], st.astype(BF16), nt,
                                                     preferred_element_type=F32)
            o_ref[rows, cols] = o.astype(o_ref.dtype)
            st_ref[hh] = st * decay[c * CHUNK:c * CHUNK + 1, cols] + kv_ref[c * hb + hh]


def hgrn_scan(proj, lb_dir, f_sec, reverse, seq, ctx, batch, n_heads, hb=4):
    rows = proj.shape[0]
    tb = ctx
    per_batch = seq // tb
    ctx_blk0 = (batch * seq) // tb
    groups = n_heads // hb
    width = hb * REC_DK
    n_chunks = tb // CHUNK
    tri, level_of = _scan_constants(reverse, n_chunks)

    def row_blk(b, s):
        lat = b * per_batch + (per_batch - s if reverse else s - 1)
        return jnp.where(s == 0, ctx_blk0 + b, lat)

    return pl.pallas_call(
        functools.partial(_scan_kernel, reverse=reverse, n_chunks=n_chunks, hb=hb),
        grid=(batch, groups, 1 + per_batch),
        in_specs=[
            pl.BlockSpec((tb, width), lambda b, h, s: (row_blk(b, s), h)),
            pl.BlockSpec((tb, width), lambda b, h, s: (row_blk(b, s), groups + h)),
            pl.BlockSpec((tb, width), lambda b, h, s: (row_blk(b, s), f_sec * groups + h)),
            pl.BlockSpec((None, 1, width), lambda b, h, s: (h, 0, 0)),
            pl.BlockSpec(tri.shape, lambda b, h, s: (0, 0)),
            pl.BlockSpec(level_of.shape, lambda b, h, s: (0, 0)),
        ],
        out_specs=pl.BlockSpec((tb, width), lambda b, h, s: (row_blk(b, s), h)),
        out_shape=jax.ShapeDtypeStruct((rows, n_heads * REC_DK), BF16),
        scratch_shapes=[pltpu.VMEM((hb, REC_DK, REC_DK), F32),
                        pltpu.VMEM((tb, width), F32), pltpu.VMEM((tb, width), F32), pltpu.VMEM((tb, width), F32),
                        pltpu.VMEM((N_LEVELS + 1, tb, width), BF16), pltpu.VMEM((N_LEVELS + 1, tb, width), BF16),
                        pltpu.VMEM((tb, width), F32), pltpu.VMEM((n_chunks * hb, REC_DK, REC_DK), F32),
                        pltpu.VMEM((n_chunks * hb, CHUNK, CHUNK), F32)],
        compiler_params=_cparams("arbitrary", "arbitrary", "arbitrary"),
        name="hgrn_scan_bw" if reverse else "hgrn_scan_fw",
    )(proj, proj, proj, lb_dir.reshape(groups, 1, width), tri, level_of)


def _readout_kernel(of_ref, ob_ref, g_ref, gain_ref, o_ref, *, n_heads):
    gain = gain_ref[...]
    for hh in range(n_heads):
        sl = slice(hh * REC_DK, (hh + 1) * REC_DK)
        o = of_ref[:, sl].astype(F32) + ob_ref[:, sl].astype(F32)
        ms = jnp.mean(o * o, axis=-1, keepdims=True)
        g = g_ref[:, sl].astype(F32)
        o_ref[:, sl] = (o * lax.rsqrt(ms + EPS) * gain * (g * jax.nn.sigmoid(g))).astype(o_ref.dtype)


def hgrn_readout(o_fw, o_bw, proj, o_gain, gate_sec, n_heads):
    rows, d = o_fw.shape
    tm = ROW_TILE
    return pl.pallas_call(
        functools.partial(_readout_kernel, n_heads=n_heads),
        grid=(rows // tm,),
        in_specs=[pl.BlockSpec((tm, d), lambda i: (i, 0)),
                  pl.BlockSpec((tm, d), lambda i: (i, 0)),
                  pl.BlockSpec((tm, d), lambda i: (i, gate_sec)),
                  pl.BlockSpec((1, REC_DK), lambda i: (0, 0))],
        out_specs=pl.BlockSpec((tm, d), lambda i: (i, 0)),
        out_shape=jax.ShapeDtypeStruct((rows, d), BF16),
        compiler_params=_cparams("arbitrary"),
        name="hgrn_readout",
    )(o_fw, o_bw, proj, o_gain.reshape(1, REC_DK))


def _router_kernel(y_ref, w_ref, o_ref):
    logits = jnp.dot(y_ref[...], w_ref[...], preferred_element_type=F32, precision=lax.Precision.HIGHEST)
    lane = lax.broadcasted_iota(jnp.int32, logits.shape, 1)
    neg = -jnp.inf
    logits = jnp.where(lane < N_EXPERTS, logits, neg)
    m1 = jnp.max(logits, axis=-1, keepdims=True)
    i1 = jnp.min(jnp.where(logits == m1, lane, LANES), axis=-1, keepdims=True)
    rest = jnp.where(lane == i1, neg, logits)
    m2 = jnp.max(rest, axis=-1, keepdims=True)
    i2 = jnp.min(jnp.where(rest == m2, lane, LANES), axis=-1, keepdims=True)
    e = jnp.exp(m2 - m1)
    w1 = 1.0 / (1.0 + e)
    w2 = e / (1.0 + e)
    out = jnp.where(lane == 0, i1.astype(F32),
                    jnp.where(lane == 1, i2.astype(F32),
                              jnp.where(lane == 2, w1, jnp.where(lane == 3, w2, 0.0))))
    o_ref[...] = out


def moe_router(y, w_router):
    rows, d = y.shape
    tm = ROW_TILE
    w_pad = jnp.zeros((d, LANES), F32).at[:, :N_EXPERTS].set(w_router)
    return pl.pallas_call(
        _router_kernel,
        grid=(rows // tm,),
        in_specs=[pl.BlockSpec((tm, d), lambda i: (i, 0)), pl.BlockSpec((d, LANES), lambda i: (0, 0))],
        out_specs=pl.BlockSpec((tm, LANES), lambda i: (i, 0)),
        out_shape=jax.ShapeDtypeStruct((rows, LANES), F32),
        compiler_params=_cparams("arbitrary"),
        name="moe_router",
    )(y, w_pad)


def _row_copy(src_ref, dst_ref, sem, src_row, dst_row):
    return pltpu.make_async_copy(src_ref.at[pl.ds(src_row, 1), :], dst_ref.at[pl.ds(dst_row, 1), :], sem)


def _start_row_gather(idx_ref, src_ref, dst_ref, sem):
    def start(i, carry):
        for prio in range(2):
            r = 2 * i + prio
            _row_copy(src_ref, dst_ref, sem, idx_ref[0, r], r).start(priority=prio)
        return carry

    lax.fori_loop(0, dst_ref.shape[0] // 2, start, 0, unroll=4)


def _wait_row_gather(src_ref, dst_ref, sem):
    def wait(r, carry):
        _row_copy(src_ref, dst_ref, sem, 0, r).wait()
        return carry

    lax.fori_loop(0, dst_ref.shape[0], wait, 0, unroll=8)


def _start_rows_unrolled(idx_ref, src_ref, dst_ref, sem, first, count):
    for r in range(count):
        row = first + r
        _row_copy(src_ref, dst_ref, sem, idx_ref[0, row], row).start(priority=r % 2)


def _expert_up_kernel(te_ref, ta_ref, idx_ref, idx_next_ref, y_ref, w1_ref, w3_ref, o_ref, x_ref, sem, *, n_tiles):
    i = pl.program_id(0)
    j = pl.program_id(1)
    slot = i % 2

    @pl.when(j == 0)
    def _():
        @pl.when(i == 0)
        def _():
            _start_row_gather(idx_ref, y_ref, x_ref.at[0], sem.at[0])

        @pl.when(i + 1 < n_tiles)
        def _():
            _start_row_gather(idx_next_ref, y_ref, x_ref.at[1 - slot], sem.at[1 - slot])

        _wait_row_gather(y_ref, x_ref.at[slot], sem.at[slot])

    @pl.when(ta_ref[i] > 0)
    def _():
        x = x_ref[slot].astype(BF16)
        u = jnp.dot(x, w1_ref[...], preferred_element_type=F32)
        g = jnp.dot(x, w3_ref[...], preferred_element_type=F32)
        o_ref[...] = (u * jax.nn.sigmoid(u) * g).astype(o_ref.dtype)

    @pl.when(ta_ref[i] == 0)
    def _():
        o_ref[...] = jnp.zeros(o_ref.shape, o_ref.dtype)


def expert_up(y, token_of_slot, w1, w3, tile_expert, tile_active, tn=1024):
    p = token_of_slot.shape[0]
    d = y.shape[1]
    f = w1.shape[2]
    tm = ROW_TILE
    n_tiles = p // tm
    idx = token_of_slot.reshape(n_tiles, 1, tm)
    return pl.pallas_call(
        functools.partial(_expert_up_kernel, n_tiles=n_tiles),
        grid_spec=pltpu.PrefetchScalarGridSpec(
            num_scalar_prefetch=2,
            grid=(n_tiles, f // tn),
            in_specs=[pl.BlockSpec((None, 1, tm), lambda i, j, te, ta: (i, 0, 0), memory_space=pltpu.SMEM),
                      pl.BlockSpec((None, 1, tm), lambda i, j, te, ta: (jnp.minimum(i + 1, n_tiles - 1), 0, 0),
                                   memory_space=pltpu.SMEM),
                      pl.BlockSpec(memory_space=pl.ANY),
                      pl.BlockSpec((None, d, tn), lambda i, j, te, ta: (te[i], 0, j)),
                      pl.BlockSpec((None, d, tn), lambda i, j, te, ta: (te[i], 0, j))],
            out_specs=pl.BlockSpec((tm, tn), lambda i, j, te, ta: (i, j)),
            scratch_shapes=[pltpu.VMEM((2, tm, d), y.dtype), pltpu.SemaphoreType.DMA((2,))],
        ),
        out_shape=jax.ShapeDtypeStruct((p, f), BF16),
        compiler_params=_cparams("arbitrary", "arbitrary"),
        name="moe_expert_up",
    )(tile_expert, tile_active, idx, idx, y, w1, w3)


def _expert_down_kernel(te_ref, ta_ref, a_ref, w_ref, o_ref, acc_ref, *, nk):
    i = pl.program_id(0)
    kk = pl.program_id(1)

    @pl.when(ta_ref[i] > 0)
    def _():
        part = jnp.dot(a_ref[...], w_ref[...], preferred_element_type=F32)

        @pl.when(kk == 0)
        def _():
            acc_ref[...] = part

        @pl.when(jnp.logical_and(kk > 0, kk < nk - 1))
        def _():
            acc_ref[...] += part

        @pl.when(kk == nk - 1)
        def _():
            o_ref[...] = acc_ref[...] + part

    @pl.when(jnp.logical_and(ta_ref[i] == 0, kk == nk - 1))
    def _():
        o_ref[...] = jnp.zeros(o_ref.shape, o_ref.dtype)


def expert_down(a, w2, tile_expert, tile_active, tk=2048):
    p, f = a.shape
    d = w2.shape[2]
    tm = ROW_TILE
    nk = f // tk
    assert nk >= 2
    return pl.pallas_call(
        functools.partial(_expert_down_kernel, nk=nk),
        grid_spec=pltpu.PrefetchScalarGridSpec(
            num_scalar_prefetch=2,
            grid=(p // tm, nk),
            in_specs=[pl.BlockSpec((tm, tk), lambda i, kk, te, ta: (i, kk)),
                      pl.BlockSpec((None, tk, d), lambda i, kk, te, ta: (te[i], kk, 0))],
            out_specs=pl.BlockSpec((tm, d), lambda i, kk, te, ta: (i, 0)),
            scratch_shapes=[pltpu.VMEM((tm, d), F32)],
        ),
        out_shape=jax.ShapeDtypeStruct((p, d), F32),
        compiler_params=_cparams("arbitrary", "arbitrary"),
        name="moe_expert_down",
    )(tile_expert, tile_active, a, w2)


def _combine_kernel(s0_ref, s1_ref, s0_next_ref, s1_next_ref, src_ref, route_ref, h_ref, gate_ref, gain_ref, o_ref,
                    buf0, buf1, sem, *, n_tiles):
    i = pl.program_id(0)
    slot = i % 2
    tile = o_ref.shape[0]

    @pl.when(i == 0)
    def _():
        _start_row_gather(s0_ref, src_ref, buf0.at[0], sem.at[0])
        _start_row_gather(s1_ref, src_ref, buf1.at[0], sem.at[2])

    _wait_row_gather(src_ref, buf0.at[slot], sem.at[slot])
    _wait_row_gather(src_ref, buf1.at[slot], sem.at[2 + slot])
    _start_rows_unrolled(s0_next_ref, src_ref, buf0.at[1 - slot], sem.at[1 - slot], 0, tile)
    _start_rows_unrolled(s1_next_ref, src_ref, buf1.at[1 - slot], sem.at[3 - slot], 0, tile)
    w0 = route_ref[:, TOP_K:TOP_K + 1]
    w1 = route_ref[:, TOP_K + 1:TOP_K + 2]
    o_ref[...] = _residual_epilogue(w0 * buf0[slot] + w1 * buf1[slot], h_ref[...], gate_ref[...], gain_ref[...])

    @pl.when(i == n_tiles - 1)
    def _():
        _wait_row_gather(src_ref, buf0.at[1 - slot], sem.at[1 - slot])
        _wait_row_gather(src_ref, buf1.at[1 - slot], sem.at[3 - slot])


def moe_combine_resnorm(expert_out, slot0, slot1, route, h, gain, mods, gate_sec, seq, batch, out_rows, tile=256):
    rows, d = h.shape
    row = functools.partial(_mod_row, tm=tile, seq=seq, batch=batch)
    n_tiles = out_rows // tile
    idx_spec = pl.BlockSpec((None, 1, tile), lambda i: (i, 0, 0), memory_space=pltpu.SMEM)
    next_spec = pl.BlockSpec((None, 1, tile), lambda i: (jnp.minimum(i + 1, n_tiles - 1), 0, 0),
                             memory_space=pltpu.SMEM)
    s0 = slot0.reshape(rows // tile, 1, tile)
    s1 = slot1.reshape(rows // tile, 1, tile)
    return pl.pallas_call(
        functools.partial(_combine_kernel, n_tiles=n_tiles),
        grid=(n_tiles,),
        in_specs=[idx_spec, idx_spec, next_spec, next_spec,
                  pl.BlockSpec(memory_space=pl.ANY),
                  pl.BlockSpec((tile, LANES), lambda i: (i, 0)),
                  pl.BlockSpec((tile, d), lambda i: (i, 0)),
                  pl.BlockSpec((None, 1, d), lambda i: (row(i), 0, gate_sec)),
                  pl.BlockSpec((1, d), lambda i: (0, 0))],
        out_specs=pl.BlockSpec((tile, d), lambda i: (i, 0)),
        out_shape=jax.ShapeDtypeStruct((out_rows, d), F32),
        scratch_shapes=[pltpu.VMEM((2, tile, d), F32), pltpu.VMEM((2, tile, d), F32),
                        pltpu.SemaphoreType.DMA((4,))],
        compiler_params=_cparams("arbitrary"),
        name="moe_combine_resnorm",
    )(s0, s1, s0, s1, expert_out, route, h, mods, gain.reshape(1, d))


def _routing_plan(route, tm):
    rows = route.shape[0]
    experts = route[:, :TOP_K].astype(jnp.int32).reshape(-1)
    onehot = (experts[:, None] == jnp.arange(N_EXPERTS)[None, :]).astype(jnp.int32)
    rank = jnp.sum((jnp.cumsum(onehot, axis=0) - onehot) * onehot, axis=1)
    counts = jnp.sum(onehot, axis=0)
    tiles_per = (counts + tm - 1) // tm
    tile_end = jnp.cumsum(tiles_per)
    start = (tile_end - tiles_per) * tm
    slot = jnp.sum(onehot * start[None, :], axis=1) + rank
    n_slots = rows * TOP_K + N_EXPERTS * tm
    n_tiles = n_slots // tm
    token_of_slot = jnp.zeros((n_slots,), jnp.int32).at[slot].set(jnp.arange(rows * TOP_K, dtype=jnp.int32) // TOP_K)
    tile_ids = jnp.arange(n_tiles, dtype=jnp.int32)
    tile_expert = jnp.minimum(jnp.sum((tile_ids[:, None] >= tile_end[None, :]).astype(jnp.int32), axis=1),
                              N_EXPERTS - 1)
    tile_active = (tile_ids < tile_end[-1]).astype(jnp.int32)
    slots = slot.reshape(rows, TOP_K).astype(jnp.int32)
    return token_of_slot, tile_expert, tile_active, slots[:, 0], slots[:, 1]


def moe_ffn_resnorm(y, h, w_router, w1, w3, w2, gain, mods, gate_sec, seq, batch, out_rows):
    route = moe_router(y, w_router)
    token_of_slot, tile_expert, tile_active, slot0, slot1 = _routing_plan(route, ROW_TILE)
    hid = expert_up(y, token_of_slot, w1, w3, tile_expert, tile_active)
    out = expert_down(hid, w2, tile_expert, tile_active)
    return moe_combine_resnorm(out, slot0, slot1, route, h, gain, mods, gate_sec, seq, batch, out_rows)


def _lower_bounds(logits):
    p = jax.nn.softmax(logits.astype(F32), axis=0)
    return jnp.cumsum(p, axis=0) - p[0]


def kernel(x, c, ctx, c_ctx, w_mod, b_mod, norm_gains, attn_w_in, attn_w_o, attn_q_gain, attn_k_gain,
           rec_w_in, rec_w_o, rec_lb_logits, rec_o_gain, ffn_w1, ffn_w3, ffn_w2,
           moe_w_router, moe_w1, moe_w3, moe_w2):
    batch, seq, d = x.shape
    ctx_len = ctx.shape[1]
    depth = w_mod.shape[0]
    n_lat = batch * seq
    n_heads = d // REC_DK
    assert seq % ROW_TILE == 0 and (batch * ctx_len) % ROW_TILE == 0 and batch + 1 <= MOD_ROWS

    hg = jnp.concatenate([x.reshape(n_lat, d), ctx.reshape(batch * ctx_len, d)], axis=0)
    cond = jnp.zeros((MOD_ROWS, d), F32).at[:batch].set(c).at[batch].set(c_ctx)
    mods_all = modulation_table(cond, w_mod, b_mod)
    lower = _lower_bounds(rec_lb_logits)
    tables = rope_tables(seq, ROW_TILE)
    tq = min(256, ctx_len)
    tk = min(2048, seq // 4)

    for layer in range(depth):
        j = layer // 2
        ng = norm_gains[layer]
        mods = mods_all[layer]
        if layer % 2 == 0:
            qkv = norm_matmul(hg, ng[0], mods, 0, 1, seq, batch, attn_w_in[j].astype(BF16), BF16, tn=1024)
            q, k, v_ext = qk_prep(qkv, attn_q_gain[j], attn_k_gain[j], tables, seq, n_lat)
            mix = flash_attention(q, k, v_ext, seq, ctx_len, batch, tq, tk)
            hg = matmul_resnorm(mix, attn_w_o[j].astype(BF16), hg, ng[1], mods, 2, seq, batch, tk=d)
        else:
            proj = norm_matmul(hg, ng[0], mods, 0, 1, seq, batch, rec_w_in[j].astype(BF16), BF16, tn=2048)
            o_fw = hgrn_scan(proj, lower[layer, 0], 2, False, seq, ctx_len, batch, n_heads)
            o_bw = hgrn_scan(proj, lower[layer, 1], 3, True, seq, ctx_len, batch, n_heads)
            mix = hgrn_readout(o_fw, o_bw, proj, rec_o_gain[j], 4, n_heads)
            hg = matmul_resnorm(mix, rec_w_o[j].astype(BF16), hg, ng[1], mods, 2, seq, batch, tk=d)
        if layer % 2 == 0:
            hid = norm_swiglu_up(hg, ng[2], mods, 3, 4, seq, batch, ffn_w1[j].astype(BF16), ffn_w3[j].astype(BF16))
            f = hid.shape[1]
            hg = matmul_resnorm(hid, ffn_w2[j].astype(BF16), hg, ng[3], mods, 5, seq, batch, tk=f // 2)
        else:
            y = norm_modulate(hg, ng[2], mods, 3, 4, seq, batch, F32)
            out_rows = n_lat if layer == depth - 1 else hg.shape[0]
            hg = moe_ffn_resnorm(y, hg, moe_w_router[j], moe_w1[j].astype(BF16), moe_w3[j].astype(BF16),
                                 moe_w2[j].astype(BF16), ng[3], mods, 5, seq, batch, out_rows)
    return hg[:n_lat].reshape(batch, seq, d)
```

```python
import functools

import numpy as np
import jax
import jax.numpy as jnp
from jax import lax
from jax.experimental import pallas as pl
from jax.experimental.pallas import tpu as pltpu

F32 = jnp.float32
BF16 = jnp.bfloat16

HEAD_DIM = 128
N_KV_HEADS = 4
GROUP = 4
GRID_W = 64
ROPE_THETA = 10000.0
AXIS_ROT_DIM = HEAD_DIM // 2
REC_DK = 128
N_EXPERTS = 8
TOP_K = 2
N_MOD = 6
EPS = 1e-6
CHUNK = 64
N_LEVELS = 6

LANES = 128
VMEM_LIMIT = 56 * 2**20

ROW_TILE = 512
MOD_ROWS = 8


def _cparams(*sem):
    return pltpu.CompilerParams(dimension_semantics=sem, vmem_limit_bytes=VMEM_LIMIT)


def _mod_row(i, tm, seq, batch):
    return jnp.minimum((i * tm) // seq, batch)


def _mod_kernel(c_ref, w_ref, b_ref, o_ref):
    c = c_ref[...]
    s = (c * jax.nn.sigmoid(c)).astype(BF16)
    o_ref[...] = jnp.dot(s, w_ref[...].astype(BF16), preferred_element_type=F32) + b_ref[...]


def modulation_table(cond, w_mod, b_mod, tn=1024):
    depth, d, n = w_mod.shape
    out = pl.pallas_call(
        _mod_kernel,
        grid=(depth, n // tn),
        in_specs=[
            pl.BlockSpec((MOD_ROWS, d), lambda l, j: (0, 0)),
            pl.BlockSpec((None, d, tn), lambda l, j: (l, 0, j)),
            pl.BlockSpec((None, 1, tn), lambda l, j: (l, 0, j)),
        ],
        out_specs=pl.BlockSpec((None, MOD_ROWS, tn), lambda l, j: (l, 0, j)),
        out_shape=jax.ShapeDtypeStruct((depth, MOD_ROWS, n), F32),
        compiler_params=_cparams("arbitrary", "arbitrary"),
        name="modulation_table",
    )(cond, w_mod, b_mod.reshape(depth, 1, n))
    return out.reshape(depth, MOD_ROWS, 1, n)


def _norm_mod_kernel(h_ref, g_ref, sh_ref, sc_ref, o_ref):
    x = h_ref[...]
    ms = jnp.mean(x * x, axis=-1, keepdims=True)
    y = x * lax.rsqrt(ms + EPS) * g_ref[...]
    o_ref[...] = (y * (1.0 + sc_ref[...]) + sh_ref[...]).astype(o_ref.dtype)


def norm_modulate(h, gain, mods, shift_sec, scale_sec, seq, batch, out_dtype):
    rows, d = h.shape
    tm = ROW_TILE
    row = functools.partial(_mod_row, tm=tm, seq=seq, batch=batch)
    return pl.pallas_call(
        _norm_mod_kernel,
        grid=(rows // tm,),
        in_specs=[
            pl.BlockSpec((tm, d), lambda i: (i, 0)),
            pl.BlockSpec((1, d), lambda i: (0, 0)),
            pl.BlockSpec((None, 1, d), lambda i: (row(i), 0, shift_sec)),
            pl.BlockSpec((None, 1, d), lambda i: (row(i), 0, scale_sec)),
        ],
        out_specs=pl.BlockSpec((tm, d), lambda i: (i, 0)),
        out_shape=jax.ShapeDtypeStruct((rows, d), out_dtype),
        compiler_params=_cparams("arbitrary"),
        name="norm_modulate",
    )(h, gain.reshape(1, d), mods, mods)


def _normed_lhs(h_ref, g_ref, sh_ref, sc_ref, y_ref):
    @pl.when(pl.program_id(1) == 0)
    def _():
        _norm_mod_kernel(h_ref, g_ref, sh_ref, sc_ref, y_ref)


def _norm_specs(tm, d, row, shift_sec, scale_sec):
    return [pl.BlockSpec((tm, d), lambda i, j: (i, 0)),
            pl.BlockSpec((1, d), lambda i, j: (0, 0)),
            pl.BlockSpec((None, 1, d), lambda i, j: (row(i), 0, shift_sec)),
            pl.BlockSpec((None, 1, d), lambda i, j: (row(i), 0, scale_sec))]


def _norm_mm_kernel(h_ref, g_ref, sh_ref, sc_ref, b_ref, o_ref, y_ref):
    _normed_lhs(h_ref, g_ref, sh_ref, sc_ref, y_ref)
    o_ref[...] = jnp.dot(y_ref[...], b_ref[...], preferred_element_type=F32).astype(o_ref.dtype)


def norm_matmul(h, gain, mods, shift_sec, scale_sec, seq, batch, b, out_dtype, tn):
    m, d = h.shape
    n = b.shape[1]
    tm = ROW_TILE
    row = functools.partial(_mod_row, tm=tm, seq=seq, batch=batch)
    return pl.pallas_call(
        _norm_mm_kernel,
        grid=(m // tm, n // tn),
        in_specs=_norm_specs(tm, d, row, shift_sec, scale_sec) + [pl.BlockSpec((d, tn), lambda i, j: (0, j))],
        out_specs=pl.BlockSpec((tm, tn), lambda i, j: (i, j)),
        out_shape=jax.ShapeDtypeStruct((m, n), out_dtype),
        scratch_shapes=[pltpu.VMEM((tm, d), BF16)],
        compiler_params=_cparams("arbitrary", "arbitrary"),
        name="norm_matmul",
    )(h, gain.reshape(1, d), mods, mods, b)


def _norm_swiglu_up_kernel(h_ref, g_ref, sh_ref, sc_ref, w1_ref, w3_ref, o_ref, y_ref):
    _normed_lhs(h_ref, g_ref, sh_ref, sc_ref, y_ref)
    a = y_ref[...]
    u = jnp.dot(a, w1_ref[...], preferred_element_type=F32)
    g = jnp.dot(a, w3_ref[...], preferred_element_type=F32)
    o_ref[...] = (u * jax.nn.sigmoid(u) * g).astype(o_ref.dtype)


def norm_swiglu_up(h, gain, mods, shift_sec, scale_sec, seq, batch, w1, w3, tn=1408):
    m, d = h.shape
    n = w1.shape[1]
    tm = ROW_TILE
    row = functools.partial(_mod_row, tm=tm, seq=seq, batch=batch)
    return pl.pallas_call(
        _norm_swiglu_up_kernel,
        grid=(m // tm, n // tn),
        in_specs=_norm_specs(tm, d, row, shift_sec, scale_sec) + [pl.BlockSpec((d, tn), lambda i, j: (0, j)),
                                                                  pl.BlockSpec((d, tn), lambda i, j: (0, j))],
        out_specs=pl.BlockSpec((tm, tn), lambda i, j: (i, j)),
        out_shape=jax.ShapeDtypeStruct((m, n), BF16),
        scratch_shapes=[pltpu.VMEM((tm, d), BF16)],
        compiler_params=_cparams("arbitrary", "arbitrary"),
        name="norm_swiglu_up",
    )(h, gain.reshape(1, d), mods, mods, w1, w3)


def _residual_epilogue(r, h, gate, gain):
    ms = jnp.mean(r * r, axis=-1, keepdims=True)
    return h + gate * (r * lax.rsqrt(ms + EPS) * gain)


def _mm_resnorm_kernel(a_ref, b_ref, h_ref, gate_ref, gain_ref, o_ref, acc_ref, *, nk):
    kk = pl.program_id(1)
    part = jnp.dot(a_ref[...], b_ref[...], preferred_element_type=F32)

    if nk == 1:
        o_ref[...] = _residual_epilogue(part, h_ref[...], gate_ref[...], gain_ref[...])
        return

    @pl.when(kk == 0)
    def _():
        acc_ref[...] = part

    @pl.when(kk > 0)
    def _():
        acc_ref[...] += part

    @pl.when(kk == nk - 1)
    def _():
        o_ref[...] = _residual_epilogue(acc_ref[...], h_ref[...], gate_ref[...], gain_ref[...])


def matmul_resnorm(a, b, h, gain, mods, gate_sec, seq, batch, tk):
    m, k = a.shape
    d = b.shape[1]
    tm = ROW_TILE
    nk = k // tk
    row = functools.partial(_mod_row, tm=tm, seq=seq, batch=batch)
    return pl.pallas_call(
        functools.partial(_mm_resnorm_kernel, nk=nk),
        grid=(m // tm, nk),
        in_specs=[
            pl.BlockSpec((tm, tk), lambda i, kk: (i, kk)),
            pl.BlockSpec((tk, d), lambda i, kk: (kk, 0)),
            pl.BlockSpec((tm, d), lambda i, kk: (i, 0)),
            pl.BlockSpec((None, 1, d), lambda i, kk: (row(i), 0, gate_sec)),
            pl.BlockSpec((1, d), lambda i, kk: (0, 0)),
        ],
        out_specs=pl.BlockSpec((tm, d), lambda i, kk: (i, 0)),
        out_shape=jax.ShapeDtypeStruct((m, d), F32),
        scratch_shapes=[pltpu.VMEM((tm, d), F32)],
        compiler_params=_cparams("arbitrary", "arbitrary"),
        name="matmul_resnorm",
    )(a, b, h, mods, gain.reshape(1, d))


def rope_tables(seq, tile):
    rows = seq // GRID_W
    row = jnp.repeat(jnp.arange(rows, dtype=F32), GRID_W)
    col = jnp.tile(jnp.arange(GRID_W, dtype=F32), rows)
    inv_freq = ROPE_THETA ** (-jnp.arange(0, AXIS_ROT_DIM, 2, dtype=F32) / AXIS_ROT_DIM)
    ar, ac = row[:, None] * inv_freq, col[:, None] * inv_freq
    zeros = jnp.zeros_like(ar)
    cos = jnp.concatenate([jnp.cos(ar), jnp.cos(ar), jnp.cos(ac), jnp.cos(ac)], axis=-1)
    sin_lo = jnp.concatenate([-jnp.sin(ar), zeros, -jnp.sin(ac), zeros], axis=-1)
    sin_hi = jnp.concatenate([zeros, jnp.sin(ar), zeros, jnp.sin(ac)], axis=-1)
    pad0 = jnp.zeros((tile, HEAD_DIM), F32)
    return (jnp.concatenate([cos, jnp.ones((tile, HEAD_DIM), F32)], axis=0),
            jnp.concatenate([sin_lo, pad0], axis=0),
            jnp.concatenate([sin_hi, pad0], axis=0))


def _qk_prep_kernel(qkv_ref, qg_ref, kg_ref, cos_ref, slo_ref, shi_ref, q_ref, k_ref, v_ref, *, n_q, n_k, q_scale):
    cos, slo, shi = cos_ref[...], slo_ref[...], shi_ref[...]
    quarter = AXIS_ROT_DIM // 2

    def head(x, gain):
        x = x.astype(F32)
        ms = jnp.mean(x * x, axis=-1, keepdims=True)
        xn = x * lax.rsqrt(ms + EPS) * gain
        return (xn * cos + pltpu.roll(xn, HEAD_DIM - quarter, axis=1) * slo
                + pltpu.roll(xn, quarter, axis=1) * shi)

    for hh in range(n_q):
        sl = slice(hh * HEAD_DIM, (hh + 1) * HEAD_DIM)
        q_ref[:, sl] = (head(qkv_ref[:, sl], qg_ref[...]) * q_scale).astype(q_ref.dtype)
    for hh in range(n_k):
        src = slice((n_q + hh) * HEAD_DIM, (n_q + hh + 1) * HEAD_DIM)
        k_ref[:, hh * HEAD_DIM:(hh + 1) * HEAD_DIM] = head(qkv_ref[:, src], kg_ref[...]).astype(k_ref.dtype)
    for hh in range(n_k):
        src = slice((n_q + n_k + hh) * HEAD_DIM, (n_q + n_k + hh + 1) * HEAD_DIM)
        v_ref[:, 2 * hh * HEAD_DIM:(2 * hh + 1) * HEAD_DIM] = qkv_ref[:, src]
        v_ref[:, (2 * hh + 1) * HEAD_DIM:(2 * hh + 2) * HEAD_DIM] = jnp.ones((v_ref.shape[0], HEAD_DIM), v_ref.dtype)


def qk_prep(qkv, q_gain, k_gain, tables, seq, n_lat_rows):
    rows = qkv.shape[0]
    tm = ROW_TILE
    n_q = 4 * N_KV_HEADS
    per_batch = seq // tm
    n_lat_tiles = n_lat_rows // tm
    tab = lambda i: (jnp.where(i < n_lat_tiles, i % per_batch, per_batch), 0)
    q_scale = HEAD_DIM ** -0.5 * float(np.log2(np.e))
    return pl.pallas_call(
        functools.partial(_qk_prep_kernel, n_q=n_q, n_k=N_KV_HEADS, q_scale=q_scale),
        grid=(rows // tm,),
        in_specs=[
            pl.BlockSpec((tm, (n_q + 2 * N_KV_HEADS) * HEAD_DIM), lambda i: (i, 0)),
            pl.BlockSpec((1, HEAD_DIM), lambda i: (0, 0)),
            pl.BlockSpec((1, HEAD_DIM), lambda i: (0, 0)),
            pl.BlockSpec((tm, HEAD_DIM), tab),
            pl.BlockSpec((tm, HEAD_DIM), tab),
            pl.BlockSpec((tm, HEAD_DIM), tab),
        ],
        out_specs=[pl.BlockSpec((tm, n_q * HEAD_DIM), lambda i: (i, 0)),
                   pl.BlockSpec((tm, N_KV_HEADS * HEAD_DIM), lambda i: (i, 0)),
                   pl.BlockSpec((tm, 2 * N_KV_HEADS * HEAD_DIM), lambda i: (i, 0))],
        out_shape=[jax.ShapeDtypeStruct((rows, n_q * HEAD_DIM), BF16),
                   jax.ShapeDtypeStruct((rows, N_KV_HEADS * HEAD_DIM), BF16),
                   jax.ShapeDtypeStruct((rows, 2 * N_KV_HEADS * HEAD_DIM), BF16)],
        compiler_params=_cparams("arbitrary"),
        name="qk_prep",
    )(qkv, q_gain.reshape(1, HEAD_DIM), k_gain.reshape(1, HEAD_DIM), *tables)


def _flash_kernel(q_ref, k_ref, v_ref, kc_ref, vc_ref, o_ref, m_ref, acc_ref,
                  s0_ref, s1_ref, p0_ref, p1_ref, mx0_ref, mx1_ref, alpha0_ref, alpha1_ref, *,
                  n_lat_tiles, nkv, tq, tk, ctx):
    qi = pl.program_id(1)
    r = GROUP * tq
    rc = min(128, r)
    s_ref, p_ref = (s0_ref, s1_ref), (p0_ref, p1_ref)
    mx_ref, alpha_ref = (mx0_ref, mx1_ref), (alpha0_ref, alpha1_ref)
    q = jnp.concatenate([q_ref[:, g * HEAD_DIM:(g + 1) * HEAD_DIM] for g in range(GROUP)], axis=0)
    nt = (((1,), (1,)), ((), ()))

    def scores(k):
        return lax.dot_general(q, k, nt, preferred_element_type=F32)

    def times_values(p, v):
        return jnp.dot(p, v, preferred_element_type=F32)

    def kv_rows(j):
        start = j * tk
        return pl.ds(start if isinstance(j, int) else pl.multiple_of(start, tk), tk)

    def put_scores(slot, k):
        width = k.shape[0]
        s = scores(k)
        s_ref[slot][:, 0:width] = s
        mx = s[:, 0:LANES]
        for c0 in range(LANES, width, LANES):
            mx = jnp.maximum(mx, s[:, c0:c0 + LANES])
        mx_ref[slot][...] = mx

    def softmax_stage(slot, width, first):
        for r0 in range(0, r, rc):
            rows = slice(r0, r0 + rc)
            row_max = jnp.max(mx_ref[slot][rows, :], axis=-1, keepdims=True)
            if first:
                m_new = jnp.broadcast_to(row_max, (rc, LANES))
            else:
                m_prev = m_ref[rows, :]
                m_new = jnp.maximum(m_prev, row_max)
                alpha_ref[slot][rows, :] = jnp.exp2(m_prev - m_new)
            m_ref[rows, :] = m_new
            for c0 in range(0, width, LANES):
                cols = slice(c0, c0 + LANES)
                p_ref[slot][rows, cols] = jnp.exp2(s_ref[slot][rows, cols] - m_new).astype(BF16)

    def accumulate(slot, v):
        pv = times_values(p_ref[slot][...], v)
        for r0 in range(0, r, rc):
            rows = slice(r0, r0 + rc)
            a = alpha_ref[slot][rows, :]
            for c0 in (0, LANES):
                cols = slice(c0, c0 + LANES)
                acc_ref[rows, cols] = a * acc_ref[rows, cols] + pv[rows, cols]

    def stage(j, slot, last):
        if not last:
            put_scores(1 - slot, k_ref[kv_rows(j + 1), :])
        softmax_stage(slot, tk, False)
        accumulate(slot, v_ref[kv_rows(j), :])

    def context_stage():
        softmax_stage(1, ctx, True)
        acc_ref[...] = times_values(p_ref[1][:, 0:ctx], vc_ref[...])

    @pl.when(qi >= n_lat_tiles)
    def _():
        put_scores(1, kc_ref[...])
        context_stage()

    @pl.when(qi < n_lat_tiles)
    def _():
        put_scores(1, kc_ref[...])
        put_scores(0, k_ref[kv_rows(0), :])
        context_stage()

        def pair(t, carry):
            stage(2 * t, 0, False)
            stage(2 * t + 1, 1, False)
            return carry

        lax.fori_loop(0, (nkv - 2) // 2, pair, 0)
        stage(nkv - 2, 0, False)
        stage(nkv - 1, 1, True)

    for g in range(GROUP):
        rows = slice(g * tq, (g + 1) * tq)
        o = acc_ref[rows, 0:HEAD_DIM] / acc_ref[rows, HEAD_DIM:2 * HEAD_DIM]
        o_ref[:, g * HEAD_DIM:(g + 1) * HEAD_DIM] = o.astype(o_ref.dtype)


def flash_attention(q, k, v_ext, seq, ctx, batch, tq, tk):
    rows = q.shape[0]
    n_q_heads = GROUP * N_KV_HEADS
    per_batch_q = seq // tq
    n_lat_tiles = batch * per_batch_q
    n_ctx_tiles = batch * (ctx // tq)
    nkv = seq // tk
    assert nkv >= 2 and nkv % 2 == 0 and ctx <= tk and ctx % LANES == 0
    ctx_blk0 = (batch * seq) // ctx
    r = GROUP * tq

    def batch_of(qi):
        return jnp.where(qi < n_lat_tiles, qi // per_batch_q, (qi - n_lat_tiles) // (ctx // tq))

    grid = (N_KV_HEADS, n_lat_tiles + n_ctx_tiles)
    return pl.pallas_call(
        functools.partial(_flash_kernel, n_lat_tiles=n_lat_tiles, nkv=nkv, tq=tq, tk=tk, ctx=ctx),
        grid=grid,
        in_specs=[
            pl.BlockSpec((tq, GROUP * HEAD_DIM), lambda h, qi: (qi, h)),
            pl.BlockSpec((seq, HEAD_DIM), lambda h, qi: (batch_of(qi), h)),
            pl.BlockSpec((seq, 2 * HEAD_DIM), lambda h, qi: (batch_of(qi), h)),
            pl.BlockSpec((ctx, HEAD_DIM), lambda h, qi: (ctx_blk0 + batch_of(qi), h)),
            pl.BlockSpec((ctx, 2 * HEAD_DIM), lambda h, qi: (ctx_blk0 + batch_of(qi), h)),
        ],
        out_specs=pl.BlockSpec((tq, GROUP * HEAD_DIM), lambda h, qi: (qi, h)),
        out_shape=jax.ShapeDtypeStruct((rows, n_q_heads * HEAD_DIM), BF16),
        scratch_shapes=[pltpu.VMEM((r, LANES), F32), pltpu.VMEM((r, 2 * HEAD_DIM), F32),
                        pltpu.VMEM((r, tk), F32), pltpu.VMEM((r, tk), F32),
                        pltpu.VMEM((r, tk), BF16), pltpu.VMEM((r, tk), BF16),
                        pltpu.VMEM((r, LANES), F32), pltpu.VMEM((r, LANES), F32),
                        pltpu.VMEM((r, LANES), F32), pltpu.VMEM((r, LANES), F32)],
        compiler_params=_cparams("arbitrary", "arbitrary"),
        name="flash_attention",
    )(q, k, v_ext, k, v_ext)


def _scan_constants(reverse, n_chunks):
    t = np.arange(CHUNK)[:, None]
    u = np.arange(CHUNK)[None, :]
    tri = (u >= t) if reverse else (u <= t)
    level_of = np.where(t == u, 0, -1)
    for lvl in range(N_LEVELS):
        b = CHUNK >> (lvl + 1)
        same = (t & ~(2 * b - 1)) == (u & ~(2 * b - 1))
        t_hi, s_hi = (t & b) != 0, (u & b) != 0
        level_of = np.where(same & (~t_hi & s_hi if reverse else t_hi & ~s_hi), lvl + 1, level_of)
    return jnp.asarray(np.kron(np.eye(n_chunks), tri), BF16), jnp.asarray(level_of, jnp.int32)


def _keep_bf16_bits(x):
    bits = lax.bitcast_convert_type(x, jnp.int32) & jnp.int32(-65536)
    return lax.bitcast_convert_type(bits, F32)


def _block_rows(x, block, off):
    rows, width = x.shape
    if block >= 8:
        parts = [jnp.broadcast_to(x[base + off:base + off + 1], (block, width)) for base in range(0, rows, block)]
        return parts[0] if len(parts) == 1 else jnp.concatenate(parts, axis=0)
    within = lax.broadcasted_iota(jnp.int32, x.shape, 0) & (block - 1)
    out = x
    for r in range(block):
        if r != off:
            out = jnp.where(within == r, pltpu.roll(x, (r - off) % rows, axis=0), out)
    return out


def _scan_kernel(q_ref, v_ref, f_ref, lb_ref, tri_ref, lvl_ref, o_ref,
                 st_ref, q32_ref, k32_ref, cb_ref, ql_ref, kl_ref, oi_ref, kv_ref, a_ref, *, reverse, n_chunks, hb):
    step = pl.program_id(2)

    @pl.when(step == 0)
    def _():
        st_ref[...] = jnp.zeros(st_ref.shape, F32)

    width = hb * REC_DK
    nt = (((1,), (1,)), ((), ()))
    tn = (((0,), (0,)), ((), ()))

    lb = lb_ref[...]
    qr = q_ref[...].astype(F32)
    q = (qr * (0.5 * REC_DK ** -0.5)) * (1.0 + jnp.tanh(0.5 * qr))
    fg = (0.5 + 0.5 * lb) + (0.5 - 0.5 * lb) * jnp.tanh(0.5 * f_ref[...].astype(F32))
    k = 1.0 - fg
    lf = jnp.log2(fg)
    hi = _keep_bf16_bits(lf)
    rest = lf - hi
    mid = _keep_bf16_bits(rest)
    lo = rest - mid
    sums = jnp.dot(tri_ref[...], jnp.concatenate([hi.astype(BF16), mid.astype(BF16), lo.astype(BF16)], axis=1),
                   preferred_element_type=F32)
    q32_ref[...] = q
    k32_ref[...] = k
    cb_ref[...] = sums[:, 0:width] + sums[:, width:2 * width] + sums[:, 2 * width:3 * width]
    ql_ref[0] = q.astype(BF16)
    kl_ref[0] = k.astype(BF16)

    tiles = [(c * hb + hh, slice(c * CHUNK, (c + 1) * CHUNK), slice(hh * REC_DK, (hh + 1) * REC_DK))
             for c in range(n_chunks) for hh in range(hb)]

    total = _block_rows(cb_ref[...], CHUNK, 0 if reverse else CHUNK - 1)
    qd = (q32_ref[...] * jnp.exp2(cb_ref[...])).astype(BF16)
    kd = (k32_ref[...] * jnp.exp2(total - cb_ref[...])).astype(BF16)
    for idx, rows, cols in tiles:
        kv_ref[idx] = lax.dot_general(v_ref[rows, cols], kd[rows, cols], tn, preferred_element_type=F32)

    level_of = lvl_ref[...]
    for lvl in range(N_LEVELS + 1):
        if lvl > 0:
            b = CHUNK >> lvl
            d = cb_ref[...] - _block_rows(cb_ref[...], 2 * b, b if reverse else b - 1)
            ql_ref[lvl] = (q32_ref[...] * jnp.exp2(jnp.minimum(d, 0.0))).astype(BF16)
            kl_ref[lvl] = (k32_ref[...] * jnp.exp2(jnp.minimum(-d, 0.0))).astype(BF16)
        for idx, rows, cols in tiles:
            pairs = lax.dot_general(ql_ref[lvl, rows, cols], kl_ref[lvl, rows, cols], nt,
                                    preferred_element_type=F32)
            a_ref[idx] = jnp.where(level_of == lvl, pairs, 0.0 if lvl == 0 else a_ref[idx])

    for idx, rows, cols in tiles:
        oi_ref[rows, cols] = jnp.dot(a_ref[idx].astype(BF16), v_ref[rows, cols], preferred_element_type=F32)

    decay = jnp.exp2(total)
    for c in (range(n_chunks - 1, -1, -1) if reverse else range(n_chunks)):
        rows = slice(c * CHUNK, (c + 1) * CHUNK)
        for hh in range(hb):
            cols = slice(hh * REC_DK, (hh + 1) * REC_DK)
            st = st_ref[hh]
            o = oi_ref[rows, cols] + lax.dot_general(qd[rows, cols], st.astype(BF16), nt,
                                                     preferred_element_type=F32)
            o_ref[rows, cols] = o.astype(o_ref.dtype)
            st_ref[hh] = st * decay[c * CHUNK:c * CHUNK + 1, cols] + kv_ref[c * hb + hh]


def hgrn_scan(proj, lb_dir, f_sec, reverse, seq, ctx, batch, n_heads, hb=4):
    rows = proj.shape[0]
    tb = ctx
    per_batch = seq // tb
    ctx_blk0 = (batch * seq) // tb
    groups = n_heads // hb
    width = hb * REC_DK
    n_chunks = tb // CHUNK
    tri, level_of = _scan_constants(reverse, n_chunks)

    def row_blk(b, s):
        lat = b * per_batch + (per_batch - s if reverse else s - 1)
        return jnp.where(s == 0, ctx_blk0 + b, lat)

    return pl.pallas_call(
        functools.partial(_scan_kernel, reverse=reverse, n_chunks=n_chunks, hb=hb),
        grid=(batch, groups, 1 + per_batch),
        in_specs=[
            pl.BlockSpec((tb, width), lambda b, h, s: (row_blk(b, s), h)),
            pl.BlockSpec((tb, width), lambda b, h, s: (row_blk(b, s), groups + h)),
            pl.BlockSpec((tb, width), lambda b, h, s: (row_blk(b, s), f_sec * groups + h)),
            pl.BlockSpec((None, 1, width), lambda b, h, s: (h, 0, 0)),
            pl.BlockSpec(tri.shape, lambda b, h, s: (0, 0)),
            pl.BlockSpec(level_of.shape, lambda b, h, s: (0, 0)),
        ],
        out_specs=pl.BlockSpec((tb, width), lambda b, h, s: (row_blk(b, s), h)),
        out_shape=jax.ShapeDtypeStruct((rows, n_heads * REC_DK), BF16),
        scratch_shapes=[pltpu.VMEM((hb, REC_DK, REC_DK), F32),
                        pltpu.VMEM((tb, width), F32), pltpu.VMEM((tb, width), F32), pltpu.VMEM((tb, width), F32),
                        pltpu.VMEM((N_LEVELS + 1, tb, width), BF16), pltpu.VMEM((N_LEVELS + 1, tb, width), BF16),
                        pltpu.VMEM((tb, width), F32), pltpu.VMEM((n_chunks * hb, REC_DK, REC_DK), F32),
                        pltpu.VMEM((n_chunks * hb, CHUNK, CHUNK), F32)],
        compiler_params=_cparams("arbitrary", "arbitrary", "arbitrary"),
        name="hgrn_scan_bw" if reverse else "hgrn_scan_fw",
    )(proj, proj, proj, lb_dir.reshape(groups, 1, width), tri, level_of)


def _readout_kernel(of_ref, ob_ref, g_ref, gain_ref, o_ref, *, n_heads):
    gain = gain_ref[...]
    for hh in range(n_heads):
        sl = slice(hh * REC_DK, (hh + 1) * REC_DK)
        o = of_ref[:, sl].astype(F32) + ob_ref[:, sl].astype(F32)
        ms = jnp.mean(o * o, axis=-1, keepdims=True)
        g = g_ref[:, sl].astype(F32)
        o_ref[:, sl] = (o * lax.rsqrt(ms + EPS) * gain * (g * jax.nn.sigmoid(g))).astype(o_ref.dtype)


def hgrn_readout(o_fw, o_bw, proj, o_gain, gate_sec, n_heads):
    rows, d = o_fw.shape
    tm = ROW_TILE
    return pl.pallas_call(
        functools.partial(_readout_kernel, n_heads=n_heads),
        grid=(rows // tm,),
        in_specs=[pl.BlockSpec((tm, d), lambda i: (i, 0)),
                  pl.BlockSpec((tm, d), lambda i: (i, 0)),
                  pl.BlockSpec((tm, d), lambda i: (i, gate_sec)),
                  pl.BlockSpec((1, REC_DK), lambda i: (0, 0))],
        out_specs=pl.BlockSpec((tm, d), lambda i: (i, 0)),
        out_shape=jax.ShapeDtypeStruct((rows, d), BF16),
        compiler_params=_cparams("arbitrary"),
        name="hgrn_readout",
    )(o_fw, o_bw, proj, o_gain.reshape(1, REC_DK))


def _router_kernel(y_ref, w_ref, o_ref):
    logits = jnp.dot(y_ref[...], w_ref[...], preferred_element_type=F32, precision=lax.Precision.HIGHEST)
    lane = lax.broadcasted_iota(jnp.int32, logits.shape, 1)
    neg = -jnp.inf
    logits = jnp.where(lane < N_EXPERTS, logits, neg)
    m1 = jnp.max(logits, axis=-1, keepdims=True)
    i1 = jnp.min(jnp.where(logits == m1, lane, LANES), axis=-1, keepdims=True)
    rest = jnp.where(lane == i1, neg, logits)
    m2 = jnp.max(rest, axis=-1, keepdims=True)
    i2 = jnp.min(jnp.where(rest == m2, lane, LANES), axis=-1, keepdims=True)
    e = jnp.exp(m2 - m1)
    w1 = 1.0 / (1.0 + e)
    w2 = e / (1.0 + e)
    out = jnp.where(lane == 0, i1.astype(F32),
                    jnp.where(lane == 1, i2.astype(F32),
                              jnp.where(lane == 2, w1, jnp.where(lane == 3, w2, 0.0))))
    o_ref[...] = out


def moe_router(y, w_router):
    rows, d = y.shape
    tm = ROW_TILE
    w_pad = jnp.zeros((d, LANES), F32).at[:, :N_EXPERTS].set(w_router)
    return pl.pallas_call(
        _router_kernel,
        grid=(rows // tm,),
        in_specs=[pl.BlockSpec((tm, d), lambda i: (i, 0)), pl.BlockSpec((d, LANES), lambda i: (0, 0))],
        out_specs=pl.BlockSpec((tm, LANES), lambda i: (i, 0)),
        out_shape=jax.ShapeDtypeStruct((rows, LANES), F32),
        compiler_params=_cparams("arbitrary"),
        name="moe_router",
    )(y, w_pad)


def _row_copy(src_ref, dst_ref, sem, src_row, dst_row):
    return pltpu.make_async_copy(src_ref.at[pl.ds(src_row, 1), :], dst_ref.at[pl.ds(dst_row, 1), :], sem)


def _start_row_gather(idx_ref, src_ref, dst_ref, sem):
    def start(i, carry):
        for prio in range(2):
            r = 2 * i + prio
            _row_copy(src_ref, dst_ref, sem, idx_ref[0, r], r).start(priority=prio)
        return carry

    lax.fori_loop(0, dst_ref.shape[0] // 2, start, 0, unroll=4)


def _wait_row_gather(src_ref, dst_ref, sem):
    def wait(r, carry):
        _row_copy(src_ref, dst_ref, sem, 0, r).wait()
        return carry

    lax.fori_loop(0, dst_ref.shape[0], wait, 0, unroll=8)


def _start_rows_unrolled(idx_ref, src_ref, dst_ref, sem, first, count):
    for r in range(count):
        row = first + r
        _row_copy(src_ref, dst_ref, sem, idx_ref[0, row], row).start(priority=r % 2)


def _expert_up_kernel(te_ref, ta_ref, idx_ref, idx_next_ref, y_ref, w1_ref, w3_ref, o_ref, x_ref, sem, *, n_tiles):
    i = pl.program_id(0)
    j = pl.program_id(1)
    slot = i % 2

    @pl.when(j == 0)
    def _():
        @pl.when(i == 0)
        def _():
            _start_row_gather(idx_ref, y_ref, x_ref.at[0], sem.at[0])

        @pl.when(i + 1 < n_tiles)
        def _():
            _start_row_gather(idx_next_ref, y_ref, x_ref.at[1 - slot], sem.at[1 - slot])

        _wait_row_gather(y_ref, x_ref.at[slot], sem.at[slot])

    @pl.when(ta_ref[i] > 0)
    def _():
        x = x_ref[slot].astype(BF16)
        u = jnp.dot(x, w1_ref[...], preferred_element_type=F32)
        g = jnp.dot(x, w3_ref[...], preferred_element_type=F32)
        o_ref[...] = (u * jax.nn.sigmoid(u) * g).astype(o_ref.dtype)

    @pl.when(ta_ref[i] == 0)
    def _():
        o_ref[...] = jnp.zeros(o_ref.shape, o_ref.dtype)


def expert_up(y, token_of_slot, w1, w3, tile_expert, tile_active, tn=1024):
    p = token_of_slot.shape[0]
    d = y.shape[1]
    f = w1.shape[2]
    tm = ROW_TILE
    n_tiles = p // tm
    idx = token_of_slot.reshape(n_tiles, 1, tm)
    return pl.pallas_call(
        functools.partial(_expert_up_kernel, n_tiles=n_tiles),
        grid_spec=pltpu.PrefetchScalarGridSpec(
            num_scalar_prefetch=2,
            grid=(n_tiles, f // tn),
            in_specs=[pl.BlockSpec((None, 1, tm), lambda i, j, te, ta: (i, 0, 0), memory_space=pltpu.SMEM),
                      pl.BlockSpec((None, 1, tm), lambda i, j, te, ta: (jnp.minimum(i + 1, n_tiles - 1), 0, 0),
                                   memory_space=pltpu.SMEM),
                      pl.BlockSpec(memory_space=pl.ANY),
                      pl.BlockSpec((None, d, tn), lambda i, j, te, ta: (te[i], 0, j)),
                      pl.BlockSpec((None, d, tn), lambda i, j, te, ta: (te[i], 0, j))],
            out_specs=pl.BlockSpec((tm, tn), lambda i, j, te, ta: (i, j)),
            scratch_shapes=[pltpu.VMEM((2, tm, d), y.dtype), pltpu.SemaphoreType.DMA((2,))],
        ),
        out_shape=jax.ShapeDtypeStruct((p, f), BF16),
        compiler_params=_cparams("arbitrary", "arbitrary"),
        name="moe_expert_up",
    )(tile_expert, tile_active, idx, idx, y, w1, w3)


def _expert_down_kernel(te_ref, ta_ref, a_ref, w_ref, o_ref, acc_ref, *, nk):
    i = pl.program_id(0)
    kk = pl.program_id(1)

    @pl.when(ta_ref[i] > 0)
    def _():
        part = jnp.dot(a_ref[...], w_ref[...], preferred_element_type=F32)

        @pl.when(kk == 0)
        def _():
            acc_ref[...] = part

        @pl.when(jnp.logical_and(kk > 0, kk < nk - 1))
        def _():
            acc_ref[...] += part

        @pl.when(kk == nk - 1)
        def _():
            o_ref[...] = acc_ref[...] + part

    @pl.when(jnp.logical_and(ta_ref[i] == 0, kk == nk - 1))
    def _():
        o_ref[...] = jnp.zeros(o_ref.shape, o_ref.dtype)


def expert_down(a, w2, tile_expert, tile_active, tk=2048):
    p, f = a.shape
    d = w2.shape[2]
    tm = ROW_TILE
    nk = f // tk
    assert nk >= 2
    return pl.pallas_call(
        functools.partial(_expert_down_kernel, nk=nk),
        grid_spec=pltpu.PrefetchScalarGridSpec(
            num_scalar_prefetch=2,
            grid=(p // tm, nk),
            in_specs=[pl.BlockSpec((tm, tk), lambda i, kk, te, ta: (i, kk)),
                      pl.BlockSpec((None, tk, d), lambda i, kk, te, ta: (te[i], kk, 0))],
            out_specs=pl.BlockSpec((tm, d), lambda i, kk, te, ta: (i, 0)),
            scratch_shapes=[pltpu.VMEM((tm, d), F32)],
        ),
        out_shape=jax.ShapeDtypeStruct((p, d), F32),
        compiler_params=_cparams("arbitrary", "arbitrary"),
        name="moe_expert_down",
    )(tile_expert, tile_active, a, w2)


def _combine_kernel(s0_ref, s1_ref, s0_next_ref, s1_next_ref, src_ref, route_ref, h_ref, gate_ref, gain_ref, o_ref,
                    buf0, buf1, sem, *, n_tiles):
    i = pl.program_id(0)
    slot = i % 2
    tile = o_ref.shape[0]

    @pl.when(i == 0)
    def _():
        _start_row_gather(s0_ref, src_ref, buf0.at[0], sem.at[0])
        _start_row_gather(s1_ref, src_ref, buf1.at[0], sem.at[2])

    _wait_row_gather(src_ref, buf0.at[slot], sem.at[slot])
    _wait_row_gather(src_ref, buf1.at[slot], sem.at[2 + slot])
    _start_rows_unrolled(s0_next_ref, src_ref, buf0.at[1 - slot], sem.at[1 - slot], 0, tile)
    _start_rows_unrolled(s1_next_ref, src_ref, buf1.at[1 - slot], sem.at[3 - slot], 0, tile)
    w0 = route_ref[:, TOP_K:TOP_K + 1]
    w1 = route_ref[:, TOP_K + 1:TOP_K + 2]
    o_ref[...] = _residual_epilogue(w0 * buf0[slot] + w1 * buf1[slot], h_ref[...], gate_ref[...], gain_ref[...])

    @pl.when(i == n_tiles - 1)
    def _():
        _wait_row_gather(src_ref, buf0.at[1 - slot], sem.at[1 - slot])
        _wait_row_gather(src_ref, buf1.at[1 - slot], sem.at[3 - slot])


def moe_combine_resnorm(expert_out, slot0, slot1, route, h, gain, mods, gate_sec, seq, batch, out_rows, tile=256):
    rows, d = h.shape
    row = functools.partial(_mod_row, tm=tile, seq=seq, batch=batch)
    n_tiles = out_rows // tile
    idx_spec = pl.BlockSpec((None, 1, tile), lambda i: (i, 0, 0), memory_space=pltpu.SMEM)
    next_spec = pl.BlockSpec((None, 1, tile), lambda i: (jnp.minimum(i + 1, n_tiles - 1), 0, 0),
                             memory_space=pltpu.SMEM)
    s0 = slot0.reshape(rows // tile, 1, tile)
    s1 = slot1.reshape(rows // tile, 1, tile)
    return pl.pallas_call(
        functools.partial(_combine_kernel, n_tiles=n_tiles),
        grid=(n_tiles,),
        in_specs=[idx_spec, idx_spec, next_spec, next_spec,
                  pl.BlockSpec(memory_space=pl.ANY),
                  pl.BlockSpec((tile, LANES), lambda i: (i, 0)),
                  pl.BlockSpec((tile, d), lambda i: (i, 0)),
                  pl.BlockSpec((None, 1, d), lambda i: (row(i), 0, gate_sec)),
                  pl.BlockSpec((1, d), lambda i: (0, 0))],
        out_specs=pl.BlockSpec((tile, d), lambda i: (i, 0)),
        out_shape=jax.ShapeDtypeStruct((out_rows, d), F32),
        scratch_shapes=[pltpu.VMEM((2, tile, d), F32), pltpu.VMEM((2, tile, d), F32),
                        pltpu.SemaphoreType.DMA((4,))],
        compiler_params=_cparams("arbitrary"),
        name="moe_combine_resnorm",
    )(s0, s1, s0, s1, expert_out, route, h, mods, gain.reshape(1, d))


def _routing_plan(route, tm):
    rows = route.shape[0]
    experts = route[:, :TOP_K].astype(jnp.int32).reshape(-1)
    onehot = (experts[:, None] == jnp.arange(N_EXPERTS)[None, :]).astype(jnp.int32)
    rank = jnp.sum((jnp.cumsum(onehot, axis=0) - onehot) * onehot, axis=1)
    counts = jnp.sum(onehot, axis=0)
    tiles_per = (counts + tm - 1) // tm
    tile_end = jnp.cumsum(tiles_per)
    start = (tile_end - tiles_per) * tm
    slot = jnp.sum(onehot * start[None, :], axis=1) + rank
    n_slots = rows * TOP_K + N_EXPERTS * tm
    n_tiles = n_slots // tm
    token_of_slot = jnp.zeros((n_slots,), jnp.int32).at[slot].set(jnp.arange(rows * TOP_K, dtype=jnp.int32) // TOP_K)
    tile_ids = jnp.arange(n_tiles, dtype=jnp.int32)
    tile_expert = jnp.minimum(jnp.sum((tile_ids[:, None] >= tile_end[None, :]).astype(jnp.int32), axis=1),
                              N_EXPERTS - 1)
    tile_active = (tile_ids < tile_end[-1]).astype(jnp.int32)
    slots = slot.reshape(rows, TOP_K).astype(jnp.int32)
    return token_of_slot, tile_expert, tile_active, slots[:, 0], slots[:, 1]


def moe_ffn_resnorm(y, h, w_router, w1, w3, w2, gain, mods, gate_sec, seq, batch, out_rows):
    route = moe_router(y, w_router)
    token_of_slot, tile_expert, tile_active, slot0, slot1 = _routing_plan(route, ROW_TILE)
    hid = expert_up(y, token_of_slot, w1, w3, tile_expert, tile_active)
    out = expert_down(hid, w2, tile_expert, tile_active)
    return moe_combine_resnorm(out, slot0, slot1, route, h, gain, mods, gate_sec, seq, batch, out_rows)


def _lower_bounds(logits):
    p = jax.nn.softmax(logits.astype(F32), axis=0)
    return jnp.cumsum(p, axis=0) - p[0]


def kernel(x, c, ctx, c_ctx, w_mod, b_mod, norm_gains, attn_w_in, attn_w_o, attn_q_gain, attn_k_gain,
           rec_w_in, rec_w_o, rec_lb_logits, rec_o_gain, ffn_w1, ffn_w3, ffn_w2,
           moe_w_router, moe_w1, moe_w3, moe_w2):
    batch, seq, d = x.shape
    ctx_len = ctx.shape[1]
    depth = w_mod.shape[0]
    n_lat = batch * seq
    n_heads = d // REC_DK
    assert seq % ROW_TILE == 0 and (batch * ctx_len) % ROW_TILE == 0 and batch + 1 <= MOD_ROWS

    hg = jnp.concatenate([x.reshape(n_lat, d), ctx.reshape(batch * ctx_len, d)], axis=0)
    cond = jnp.zeros((MOD_ROWS, d), F32).at[:batch].set(c).at[batch].set(c_ctx)
    mods_all = modulation_table(cond, w_mod, b_mod)
    lower = _lower_bounds(rec_lb_logits)
    tables = rope_tables(seq, ROW_TILE)
    tq = min(256, ctx_len)
    tk = min(2048, seq // 4)

    for layer in range(depth):
        j = layer // 2
        ng = norm_gains[layer]
        mods = mods_all[layer]
        if layer % 2 == 0:
            qkv = norm_matmul(hg, ng[0], mods, 0, 1, seq, batch, attn_w_in[j].astype(BF16), BF16, tn=1024)
            q, k, v_ext = qk_prep(qkv, attn_q_gain[j], attn_k_gain[j], tables, seq, n_lat)
            mix = flash_attention(q, k, v_ext, seq, ctx_len, batch, tq, tk)
            hg = matmul_resnorm(mix, attn_w_o[j].astype(BF16), hg, ng[1], mods, 2, seq, batch, tk=d)
        else:
            proj = norm_matmul(hg, ng[0], mods, 0, 1, seq, batch, rec_w_in[j].astype(BF16), BF16, tn=2048)
            o_fw = hgrn_scan(proj, lower[layer, 0], 2, False, seq, ctx_len, batch, n_heads)
            o_bw = hgrn_scan(proj, lower[layer, 1], 3, True, seq, ctx_len, batch, n_heads)
            mix = hgrn_readout(o_fw, o_bw, proj, rec_o_gain[j], 4, n_heads)
            hg = matmul_resnorm(mix, rec_w_o[j].astype(BF16), hg, ng[1], mods, 2, seq, batch, tk=d)
        if layer % 2 == 0:
            hid = norm_swiglu_up(hg, ng[2], mods, 3, 4, seq, batch, ffn_w1[j].astype(BF16), ffn_w3[j].astype(BF16))
            f = hid.shape[1]
            hg = matmul_resnorm(hid, ffn_w2[j].astype(BF16), hg, ng[3], mods, 5, seq, batch, tk=f // 2)
        else:
            y = norm_modulate(hg, ng[2], mods, 3, 4, seq, batch, F32)
            out_rows = n_lat if layer == depth - 1 else hg.shape[0]
            hg = moe_ffn_resnorm(y, hg, moe_w_router[j], moe_w1[j].astype(BF16), moe_w3[j].astype(BF16),
                                 moe_w2[j].astype(BF16), ng[3], mods, 5, seq, batch, out_rows)
    return hg[:n_lat].reshape(batch, seq, d)
```
